```python
import math
import jax
import jax.numpy as jnp
from jax import lax
import numpy as np

D_MODEL = 4096
BATCH = 4
SEQ = 2048
DEPTH = 2
DEC_BATCH = 8
DEC_SEQ = 1
PAST_LEN = 16384
PAGE_SIZE = 128

HEAD_DIM = 128
D_MIX = D_MODEL
N_MIXERS = 4
HEADS_PER_MIXER = D_MIX // (N_MIXERS * HEAD_DIM)
N_HEADS_A = HEADS_PER_MIXER
KV_HEADS_A = 2
GROUP_A = N_HEADS_A // KV_HEADS_A
N_GROUPS_B = HEADS_PER_MIXER
N_HEADS_C = HEADS_PER_MIXER
KV_HEADS_C = 2
GROUP_C = N_HEADS_C // KV_HEADS_C
N_HEADS_D = HEADS_PER_MIXER
WIDTH_A = N_HEADS_A * HEAD_DIM
WIDTH_B = N_GROUPS_B * HEAD_DIM
WIDTH_C = N_HEADS_C * HEAD_DIM
WIDTH_D = N_HEADS_D * HEAD_DIM

NSA_CMP_STRIDE = 16
NSA_CMP_BLOCK = 2 * NSA_CMP_STRIDE
NSA_CMP_HIDDEN = 256
NSA_SEL_BLOCK = 64
NSA_SEL_TOP = 16
NSA_N_LOCAL = 2
NSA_WINDOW = 512
GMLP_CHUNK = 128
MOBA_BLOCK = 256
MOBA_TOP = 3
HGRN_CHUNK = 64
D_FF = ((8 * D_MODEL // 3 + 255) // 256) * 256
Q_BLOCK = 16
NEG_INF = -1e30
FORCE_SCORE = 1e4
TINY = 1e-30

SPLIT_SIZES = (WIDTH_A, 2 * KV_HEADS_A * HEAD_DIM, 2 * KV_HEADS_A * HEAD_DIM, 2 * KV_HEADS_A * HEAD_DIM,
               3 * N_HEADS_A, 2 * WIDTH_B, WIDTH_C, 2 * KV_HEADS_C * HEAD_DIM,
               WIDTH_D, WIDTH_D, WIDTH_D, WIDTH_D)
SPLIT_POINTS = tuple(sum(SPLIT_SIZES[:i + 1]) for i in range(len(SPLIT_SIZES) - 1))
N_IN = sum(SPLIT_SIZES)

kernel_name = 'hybrid_nsa_gmlp_moba_hgrn2_step'


def rms_norm(x, g, eps=1e-6):
    xf = x.astype(jnp.float32)
    y = xf * lax.rsqrt(jnp.mean(xf * xf, axis=-1, keepdims=True) + eps)
    return (y * g.astype(jnp.float32)).astype(x.dtype)


def layer_norm(x, g, b, eps=1e-5):
    xf = x.astype(jnp.float32)
    mu = jnp.mean(xf, axis=-1, keepdims=True)
    var = jnp.mean(jnp.square(xf - mu), axis=-1, keepdims=True)
    return ((xf - mu) * lax.rsqrt(var + eps) * g.astype(jnp.float32) + b.astype(jnp.float32)).astype(x.dtype)


def masked_softmax(s, mask):
    s = jnp.where(mask, s.astype(jnp.float32), NEG_INF)
    e = jnp.where(mask, jnp.exp(s - jnp.max(s, axis=-1, keepdims=True)), 0.0)
    return e / jnp.maximum(jnp.sum(e, axis=-1, keepdims=True), 1e-30)


def dense_rows(kv):
    def rows(pos, kvh, slot):
        b = jnp.arange(kv.shape[0]).reshape((-1,) + (1,) * (pos.ndim - 1))
        return kv[b, pos, slot, kvh]
    return rows


def paged_rows(pool, page_table, new_kv):
    past = page_table.shape[1] * PAGE_SIZE
    def rows(pos, kvh, slot):
        b = jnp.arange(page_table.shape[0]).reshape((-1,) + (1,) * (pos.ndim - 1))
        pp = jnp.minimum(pos, past - 1)
        old = pool[page_table[b, pp // PAGE_SIZE], pp % PAGE_SIZE, slot, kvh]
        new = new_kv[b, jnp.clip(pos - past, 0, new_kv.shape[1] - 1), slot, kvh]
        return jnp.where((pos < past)[..., None], old, new)
    return rows


def ada_mod(c, w, b):
    return (jax.nn.silu(c) @ w + b).reshape(c.shape[0], 3, 3, -1)


def modulate(x, g, shift, scale):
    return rms_norm(x, g) * (1.0 + scale[:, None]) + shift[:, None]


def swiglu(h, wg, wu, wd):
    return (jax.nn.silu(h @ wg) * (h @ wu)) @ wd


def mixer_inputs(h, w_in_l, qkg, ln_g, ln_b, lb):
    B, T, _ = h.shape
    qa, kvc, kvs, kvw, ga, uv, qc, kvm, qd, fd, vd, gd = jnp.split(h @ w_in_l, SPLIT_POINTS, axis=-1)

    def heads(a):
        return a.reshape(B, T, -1, HEAD_DIM)

    def kv_rows(a, gk):
        a = a.reshape(B, T, 2, -1, HEAD_DIM)
        if gk is None:
            return a
        return jnp.stack([rms_norm(a[:, :, 0], gk), a[:, :, 1]], axis=2)

    u, v = jnp.split(jax.nn.gelu(uv), 2, axis=-1)
    fd32 = fd.astype(jnp.float32)
    f_gate = lb + (1.0 - lb) * jax.nn.sigmoid(fd32)
    log_f = jnp.log(jnp.maximum(f_gate, TINY)).reshape(B, T, N_HEADS_D, HEAD_DIM)
    k_d = ((1.0 - lb) * jax.nn.sigmoid(-fd32)).reshape(B, T, N_HEADS_D, HEAD_DIM)
    return (rms_norm(heads(qa), qkg[0]), kv_rows(kvc, None), kv_rows(kvs, qkg[2]), kv_rows(kvw, qkg[3]),
            jax.nn.sigmoid(ga).reshape(B, T, N_HEADS_A, 3), u, layer_norm(v, ln_g, ln_b),
            rms_norm(heads(qc), qkg[4]), kv_rows(kvm, qkg[5]),
            jax.nn.silu(heads(qd)) * HEAD_DIM ** -0.5, log_f, k_d, heads(vd), gd)


def nsa_compress(kv, w1, b1, w2, pe, gk):
    B, L = kv.shape[:2]
    n_half = L // NSA_CMP_STRIDE
    halves = kv[:, :n_half * NSA_CMP_STRIDE].reshape(B, n_half, NSA_CMP_STRIDE, 2, kv.shape[3], HEAD_DIM)
    w1h = w1.reshape(2, 2, NSA_CMP_STRIDE, HEAD_DIM, NSA_CMP_HIDDEN)
    h_first = jnp.einsum('bnpskd,spdh->bnskh', halves, w1h[:, 0])
    h_second = jnp.einsum('bnpskd,spdh->bnskh', halves, w1h[:, 1])
    bias = jnp.einsum('spd,spdh->sh', pe, w1) + b1
    hid = jax.nn.gelu(h_first[:, :-1] + h_second[:, 1:] + bias[:, None, :])
    out = jnp.einsum('bnskh,shd->bnskd', hid, w2)
    return rms_norm(out[:, :, 0], gk), out[:, :, 1]


def nsa_attend(q, gates, qpos, kcmp, vcmp, sel_rows, n_total, band, kpos):
    B, Q = q.shape[:2]
    scale = HEAD_DIM ** -0.5
    qg = q.reshape(B, Q, KV_HEADS_A, GROUP_A, HEAD_DIM)
    n_cmp = kcmp.shape[1]
    cmp_end = jnp.arange(n_cmp) * NSA_CMP_STRIDE + (NSA_CMP_BLOCK - 1)
    m_cmp = cmp_end[None, :] <= qpos[:, None]
    s_cmp = jnp.einsum('bqkgd,bckd->bqkgc', qg, kcmp).astype(jnp.float32) * scale
    p_cmp = masked_softmax(s_cmp, m_cmp[None, :, None, None, :])
    o_cmp = jnp.einsum('bqkgc,bckd->bqkgd', p_cmp.astype(vcmp.dtype), vcmp)
    n_sel = -(-n_total // NSA_SEL_BLOCK)
    ci = np.arange(n_cmp)[:, None] * NSA_CMP_STRIDE
    sj = np.arange(n_sel)[None, :] * NSA_SEL_BLOCK
    cover = jnp.asarray(((ci <= sj + NSA_SEL_BLOCK - 1) & (ci + NSA_CMP_BLOCK - 1 >= sj)).astype(np.float32))
    imp = jnp.einsum('bqkgc,cj->bqkj', p_cmp, cover)
    blk = jnp.arange(n_sel)
    back = (qpos // NSA_SEL_BLOCK)[:, None] - blk[None, :]
    valid = back >= 0
    forced = (blk[None, :] == 0) | (valid & (back < NSA_N_LOCAL))
    score = jnp.where(valid[None, :, None, :], jnp.where(forced[None, :, None, :], FORCE_SCORE, imp), NEG_INF)
    n_top = min(NSA_SEL_TOP, n_sel)
    top_s, top_i = lax.top_k(score, n_top)
    pos = top_i[..., None] * NSA_SEL_BLOCK + jnp.arange(NSA_SEL_BLOCK)
    m_sel = (top_s > 0.5 * NEG_INF)[..., None] & (pos <= qpos[None, :, None, None, None])
    pos = jnp.minimum(pos, n_total - 1)
    kvh = jnp.arange(KV_HEADS_A)[None, None, :, None, None]
    k_sel = sel_rows(pos, kvh, 0)
    v_sel = sel_rows(pos, kvh, 1)
    n_rows = n_top * NSA_SEL_BLOCK
    s_sel = jnp.einsum('bqkgd,bqkjrd->bqkgjr', qg, k_sel).astype(jnp.float32).reshape(
        B, Q, KV_HEADS_A, GROUP_A, n_rows) * scale
    p_sel = masked_softmax(s_sel, m_sel.reshape(B, Q, KV_HEADS_A, 1, n_rows))
    o_sel = jnp.einsum('bqkgn,bqknd->bqkgd', p_sel.astype(v_sel.dtype),
                       v_sel.reshape(B, Q, KV_HEADS_A, n_rows, HEAD_DIM))
    m_win = (kpos[None, :] >= 0) & (kpos[None, :] <= qpos[:, None]) & (qpos[:, None] - kpos[None, :] < NSA_WINDOW)
    s_win = jnp.einsum('bqkgd,bnkd->bqkgn', qg, band[:, :, 0]).astype(jnp.float32) * scale
    p_win = masked_softmax(s_win, m_win[None, :, None, None, :])
    o_win = jnp.einsum('bqkgn,bnkd->bqkgd', p_win.astype(band.dtype), band[:, :, 1])
    g = gates.reshape(B, Q, KV_HEADS_A, GROUP_A, 3).astype(q.dtype)
    o = g[..., 0:1] * o_cmp + g[..., 1:2] * o_sel + g[..., 2:3] * o_win
    return o.reshape(B, Q, WIDTH_A)


def block_means(k):
    B, L = k.shape[:2]
    nb = -(-L // MOBA_BLOCK)
    kp = jnp.pad(k.astype(jnp.float32), ((0, 0), (0, nb * MOBA_BLOCK - L), (0, 0), (0, 0)))
    return jnp.mean(kp.reshape(B, nb, MOBA_BLOCK, k.shape[2], k.shape[3]), axis=2).astype(k.dtype)


def moba_attend(q, qpos, kmean, rows, n_total):
    B, Q = q.shape[:2]
    scale = HEAD_DIM ** -0.5
    n_blk = kmean.shape[1]
    kvh = jnp.arange(N_HEADS_C) // GROUP_C
    gate = jnp.einsum('bqhd,bnhd->bqhn', q, kmean[:, :, kvh]).astype(jnp.float32)
    cur = qpos // MOBA_BLOCK
    past_blk = jnp.arange(n_blk)[None, :] < cur[:, None]
    gate = jnp.where(past_blk[None, :, None, :], gate, NEG_INF)
    n_top = min(MOBA_TOP, n_blk)
    top_s, top_i = lax.top_k(gate, n_top)
    offs = jnp.arange(MOBA_BLOCK)
    pos_sel = (top_i[..., None] * MOBA_BLOCK + offs).reshape(B, Q, N_HEADS_C, n_top * MOBA_BLOCK)
    m_sel = jnp.broadcast_to((top_s > 0.5 * NEG_INF)[..., None],
                             (B, Q, N_HEADS_C, n_top, MOBA_BLOCK)).reshape(B, Q, N_HEADS_C, n_top * MOBA_BLOCK)
    pos_own = jnp.broadcast_to((cur * MOBA_BLOCK)[None, :, None, None] + offs, (B, Q, N_HEADS_C, MOBA_BLOCK))
    m_own = pos_own <= qpos[None, :, None, None]
    pos = jnp.minimum(jnp.concatenate([pos_sel, pos_own], axis=-1), n_total - 1)
    mask = jnp.concatenate([m_sel, m_own], axis=-1)
    kvh_b = kvh[None, None, :, None]
    k_rows = rows(pos, kvh_b, 0)
    v_rows = rows(pos, kvh_b, 1)
    s = jnp.einsum('bqhd,bqhnd->bqhn', q, k_rows).astype(jnp.float32) * scale
    p = masked_softmax(s, mask)
    o = jnp.einsum('bqhn,bqhnd->bqhd', p.astype(v_rows.dtype), v_rows)
    return o.reshape(B, Q, WIDTH_C)


def gmlp_mix(u, v, ws, bs):
    B, T, _ = u.shape
    c = min(T, GMLP_CHUNK)
    wm = jnp.tril(ws[:, :c, :c])
    vc = v.reshape(B, T // c, c, N_GROUPS_B, HEAD_DIM)
    sv = jnp.einsum('gij,bnjgd->bnigd', wm, vc) + bs[:, :c].T[None, None, :, :, None]
    return u * sv.reshape(B, T, WIDTH_B)


def hgrn_chunks(q, log_f, k, v, s0):
    B, T, H, D = q.shape
    c = math.gcd(T, HGRN_CHUNK)
    n = T // c

    def chunks(a):
        return a.astype(jnp.float32).reshape(B, n, c, H, a.shape[-1]).swapaxes(0, 1)

    causal = jnp.tril(jnp.ones((c, c), dtype=bool))[None, :, :, None, None]

    def step(S, xs):
        qc, lf, kc, vc = xs
        cum = jnp.cumsum(lf, axis=1)
        diff = jnp.minimum(cum[:, :, None] - cum[:, None, :], 0.0)
        decay = jnp.where(causal, jnp.exp(diff), 0.0)
        a = jnp.einsum('bthd,btshd,bshd->btsh', qc, decay, kc)
        o = jnp.einsum('btsh,bshv->bthv', a, vc) + jnp.einsum('bthd,bhdv->bthv', qc * jnp.exp(cum), S)
        last = cum[:, -1]
        S = jnp.exp(last)[..., None] * S + jnp.einsum('bshd,bshv->bhdv', kc * jnp.exp(last[:, None] - cum), vc)
        return S, o

    S, o = lax.scan(step, s0.astype(jnp.float32), (chunks(q), chunks(log_f), chunks(k), chunks(v)))
    return o.swapaxes(0, 1).reshape(B, T, H, v.shape[-1]), S


def hgrn_readout(o, gd, g_out):
    B, T = o.shape[:2]
    return rms_norm(o, g_out).reshape(B, T, WIDTH_D) * jax.nn.silu(gd.astype(jnp.float32))


def prompt_mix(h, w_in_l, qkg, cw1, cb1, cw2, cpe, ln_g, ln_b, ws, bs, lb, out_g):
    B, T, _ = h.shape
    qa, kvc, kvs, kvw, ga, u, v, qc, kvm, qd, log_f, kd, vd, gd = mixer_inputs(h, w_in_l, qkg, ln_g, ln_b, lb)
    kcmp, vcmp = nsa_compress(kvc, cw1, cb1, cw2, cpe, qkg[1])
    sel_rows = dense_rows(kvs)
    moba_rows = dense_rows(kvm)
    kmean = block_means(kvm[:, :, 0])
    kvw_pad = jnp.pad(kvw, ((0, 0), (NSA_WINDOW - 1, 0), (0, 0), (0, 0), (0, 0)))
    band_len = NSA_WINDOW - 1 + Q_BLOCK
    n_qb = T // Q_BLOCK
    starts = jnp.arange(n_qb, dtype=jnp.int32) * Q_BLOCK

    def blocks(a):
        return a.reshape((B, n_qb, Q_BLOCK) + a.shape[2:]).swapaxes(0, 1)

    def unblock(a):
        return a.swapaxes(0, 1).reshape(B, T, -1)

    def nsa_step(args):
        q_blk, g_blk, start = args
        band = lax.dynamic_slice_in_dim(kvw_pad, start, band_len, axis=1)
        kpos = start - (NSA_WINDOW - 1) + jnp.arange(band_len)
        return nsa_attend(q_blk, g_blk, start + jnp.arange(Q_BLOCK), kcmp, vcmp, sel_rows, T, band, kpos)

    def moba_step(args):
        q_blk, start = args
        return moba_attend(q_blk, start + jnp.arange(Q_BLOCK), kmean, moba_rows, T)

    o_a = unblock(lax.map(nsa_step, (blocks(qa), blocks(ga), starts)))
    o_b = gmlp_mix(u, v, ws, bs)
    o_c = unblock(lax.map(moba_step, (blocks(qc), starts)))
    o_d, s_d = hgrn_chunks(qd, log_f, kd, vd, jnp.zeros((B, N_HEADS_D, HEAD_DIM, HEAD_DIM), jnp.float32))
    o_d = hgrn_readout(o_d, gd, out_g).astype(h.dtype)
    mixed = jnp.concatenate([o_a, o_b, o_c, o_d], axis=-1)
    return mixed, (kvc, kvs, kvm, kvw[:, T - min(NSA_WINDOW, T):], s_d)


def sample_mix(h, w_in_l, qkg, cw1, cb1, cw2, cpe, ln_g, ln_b, ws, bs, lb, out_g,
               pool_cmp, pool_sel, pool_moba, win_buf, s0, page_table):
    B, Tn, _ = h.shape
    past = page_table.shape[1] * PAGE_SIZE
    n_total = past + Tn
    qpos = past + jnp.arange(Tn)
    qa, kvc, kvs, kvw, ga, u, v, qc, kvm, qd, log_f, kd, vd, gd = mixer_inputs(h, w_in_l, qkg, ln_g, ln_b, lb)
    past_cmp = pool_cmp[page_table].reshape((B, past) + pool_cmp.shape[2:])
    kcmp, vcmp = nsa_compress(jnp.concatenate([past_cmp, kvc.astype(past_cmp.dtype)], axis=1),
                              cw1, cb1, cw2, cpe, qkg[1])
    band = jnp.concatenate([win_buf, kvw.astype(win_buf.dtype)], axis=1)
    kpos = past - win_buf.shape[1] + jnp.arange(band.shape[1])
    o_a = nsa_attend(qa, ga, qpos, kcmp, vcmp, paged_rows(pool_sel, page_table, kvs), n_total, band, kpos)
    o_b = gmlp_mix(u, v, ws, bs)
    past_k = pool_moba[page_table[:, :, None], jnp.arange(PAGE_SIZE)[None, None, :], 0].reshape(
        B, past, KV_HEADS_C, HEAD_DIM)
    kmean = block_means(jnp.concatenate([past_k, kvm[:, :, 0].astype(past_k.dtype)], axis=1))
    o_c = moba_attend(qc, qpos, kmean, paged_rows(pool_moba, page_table, kvm), n_total)
    o_d, s_d = hgrn_chunks(qd, log_f, kd, vd, s0)
    o_d = hgrn_readout(o_d, gd, out_g).astype(h.dtype)
    mixed = jnp.concatenate([o_a, o_b, o_c, o_d], axis=-1)
    return mixed, (kvc, kvs, kvm, band[:, Tn:], s_d, v)


def trunk_layer(x, mod, g, wg, wu, wd, w_out_l, mix):
    x = x + 0.5 * mod[:, 0, 2, None] * swiglu(modulate(x, g[0], mod[:, 0, 0], mod[:, 0, 1]), wg[0], wu[0], wd[0])
    mixed, st = mix(modulate(x, g[1], mod[:, 1, 0], mod[:, 1, 1]))
    x = x + mod[:, 1, 2, None] * (mixed @ w_out_l)
    x = x + 0.5 * mod[:, 2, 2, None] * swiglu(modulate(x, g[2], mod[:, 2, 0], mod[:, 2, 1]), wg[1], wu[1], wd[1])
    return x, st


def setup_inputs(seed: int = 0) -> dict:
    key = jax.random.key(seed)
    ks = jax.random.split(key, 32)

    def nrm(i, shape, scale):
        return jax.random.normal(ks[i], shape, jnp.float32) * scale

    n_pages = PAST_LEN // PAGE_SIZE
    n_phys = (DEC_BATCH * n_pages * 5) // 4
    w_buf = min(NSA_WINDOW, PAST_LEN)
    kv_a = (DEPTH, n_phys, PAGE_SIZE, 2, KV_HEADS_A, HEAD_DIM)
    page_table = jax.random.permutation(ks[9], n_phys)[:DEC_BATCH * n_pages].reshape(
        DEC_BATCH, n_pages).astype(jnp.int32)
    return {
        'x_prompt': nrm(0, (BATCH, SEQ, D_MODEL), 1.0),
        'x_sample': nrm(1, (DEC_BATCH, DEC_SEQ, D_MODEL), 1.0),
        'c_prompt': nrm(2, (BATCH, D_MODEL), 1.0),
        'c_sample': nrm(3, (DEC_BATCH, D_MODEL), 1.0),
        'cache_nsa_cmp_kv': nrm(4, kv_a, 1.0),
        'cache_nsa_sel_kv': nrm(5, kv_a, 1.0),
        'cache_moba_kv': nrm(6, (DEPTH, n_phys, PAGE_SIZE, 2, KV_HEADS_C, HEAD_DIM), 1.0),
        'cache_nsa_win_kv': nrm(7, (DEPTH, DEC_BATCH, w_buf, 2, KV_HEADS_A, HEAD_DIM), 1.0),
        'state_hgrn': nrm(8, (DEPTH, DEC_BATCH, N_HEADS_D, HEAD_DIM, HEAD_DIM), 0.5),
        'page_table': page_table,
        'w_ada': nrm(10, (DEPTH, D_MODEL, 9 * D_MODEL), 0.5 * D_MODEL ** -0.5),
        'b_ada': nrm(11, (DEPTH, 9 * D_MODEL), 0.02),
        'norm_g': 1.0 + nrm(12, (DEPTH, 3, D_MODEL), 0.02),
        'w_ffn_gate': nrm(13, (DEPTH, 2, D_MODEL, D_FF), D_MODEL ** -0.5),
        'w_ffn_up': nrm(14, (DEPTH, 2, D_MODEL, D_FF), D_MODEL ** -0.5),
        'w_ffn_down': nrm(15, (DEPTH, 2, D_FF, D_MODEL), D_FF ** -0.5),
        'w_in': nrm(16, (DEPTH, D_MODEL, N_IN), D_MODEL ** -0.5),
        'w_out': nrm(17, (DEPTH, D_MIX, D_MODEL), D_MIX ** -0.5),
        'qk_norm_g': 1.0 + nrm(18, (DEPTH, 6, HEAD_DIM), 0.02),
        'nsa_cmp_w1': nrm(19, (DEPTH, 2, NSA_CMP_BLOCK, HEAD_DIM, NSA_CMP_HIDDEN), (NSA_CMP_BLOCK * HEAD_DIM) ** -0.5),
        'nsa_cmp_b1': nrm(20, (DEPTH, 2, NSA_CMP_HIDDEN), 0.02),
        'nsa_cmp_w2': nrm(21, (DEPTH, 2, NSA_CMP_HIDDEN, HEAD_DIM), NSA_CMP_HIDDEN ** -0.5),
        'nsa_cmp_pos': nrm(22, (DEPTH, 2, NSA_CMP_BLOCK, HEAD_DIM), 0.1),
        'gmlp_ln_g': 1.0 + nrm(23, (DEPTH, WIDTH_B), 0.02),
        'gmlp_ln_b': nrm(24, (DEPTH, WIDTH_B), 0.02),
        'gmlp_ws': nrm(25, (DEPTH, N_GROUPS_B, GMLP_CHUNK, GMLP_CHUNK), GMLP_CHUNK ** -0.5),
        'gmlp_bs': 1.0 + nrm(26, (DEPTH, N_GROUPS_B, GMLP_CHUNK), 0.1),
        'hgrn_lb_logits': nrm(27, (DEPTH, WIDTH_D), 0.5),
        'hgrn_out_g': 1.0 + nrm(28, (DEPTH, HEAD_DIM), 0.02),
    }


def reference(x_prompt, x_sample, c_prompt, c_sample, cache_nsa_cmp_kv, cache_nsa_sel_kv, cache_moba_kv,
              cache_nsa_win_kv, state_hgrn, page_table, w_ada, b_ada, norm_g, w_ffn_gate, w_ffn_up, w_ffn_down,
              w_in, w_out, qk_norm_g, nsa_cmp_w1, nsa_cmp_b1, nsa_cmp_w2, nsa_cmp_pos, gmlp_ln_g, gmlp_ln_b,
              gmlp_ws, gmlp_bs, hgrn_lb_logits, hgrn_out_g):
    lb_p = jax.nn.softmax(hgrn_lb_logits.astype(jnp.float32), axis=0)
    lb_all = jnp.clip(jnp.cumsum(lb_p, axis=0) - lb_p[0:1], 0.0, 1.0)
    xp, xs = x_prompt, x_sample
    st_p, st_s = [], []
    for l in range(DEPTH):
        mix_w = (w_in[l], qk_norm_g[l], nsa_cmp_w1[l], nsa_cmp_b1[l], nsa_cmp_w2[l], nsa_cmp_pos[l],
                 gmlp_ln_g[l], gmlp_ln_b[l], gmlp_ws[l], gmlp_bs[l], lb_all[l], hgrn_out_g[l])
        xp, sp = trunk_layer(xp, ada_mod(c_prompt, w_ada[l], b_ada[l]), norm_g[l], w_ffn_gate[l], w_ffn_up[l],
                             w_ffn_down[l], w_out[l], lambda h: prompt_mix(h, *mix_w))
        xs, ss = trunk_layer(xs, ada_mod(c_sample, w_ada[l], b_ada[l]), norm_g[l], w_ffn_gate[l], w_ffn_up[l],
                             w_ffn_down[l], w_out[l],
                             lambda h: sample_mix(h, *mix_w, cache_nsa_cmp_kv[l], cache_nsa_sel_kv[l],
                                                  cache_moba_kv[l], cache_nsa_win_kv[l], state_hgrn[l],
                                                  page_table))
        st_p.append(sp)
        st_s.append(ss)
    nsa_cmp_kv_prompt = jnp.stack([s[0] for s in st_p])
    nsa_sel_kv_prompt = jnp.stack([s[1] for s in st_p])
    moba_kv_prompt = jnp.stack([s[2] for s in st_p])
    nsa_win_kv_prompt = jnp.stack([s[3] for s in st_p])
    hgrn_state_prompt = jnp.stack([s[4] for s in st_p])
    nsa_cmp_kv_sample = jnp.stack([s[0] for s in st_s])
    nsa_sel_kv_sample = jnp.stack([s[1] for s in st_s])
    moba_kv_sample = jnp.stack([s[2] for s in st_s])
    nsa_win_kv_sample = jnp.stack([s[3] for s in st_s])
    hgrn_state_sample = jnp.stack([s[4] for s in st_s])
    gmlp_v_sample = jnp.stack([s[5] for s in st_s])
    return (xp, xs, nsa_cmp_kv_prompt, nsa_sel_kv_prompt, moba_kv_prompt, nsa_win_kv_prompt, hgrn_state_prompt,
            nsa_cmp_kv_sample, nsa_sel_kv_sample, moba_kv_sample, nsa_win_kv_sample, hgrn_state_sample,
            gmlp_v_sample)
```

```python
import functools

import jax
import jax.numpy as jnp
from jax import lax
from jax.experimental import pallas as pl
from jax.experimental.pallas import tpu as pltpu

F32 = jnp.float32
BF16 = jnp.bfloat16

HEAD_DIM = 128
N_HEADS = 8
KV_HEADS = 2
GROUP = N_HEADS // KV_HEADS
WIDTH = N_HEADS * HEAD_DIM
PAGE_SIZE = 128
PAGES_PER_STEP = 16

NSA_CMP_STRIDE = 16
NSA_CMP_BLOCK = 32
NSA_CMP_HIDDEN = 256
NSA_SEL_BLOCK = 64
NSA_SEL_TOP = 16
NSA_N_LOCAL = 2
NSA_WINDOW = 512
GMLP_CHUNK = 128
MOBA_BLOCK = 256
MOBA_TOP = 3
HGRN_CHUNK = 64
NEG_INF = -1e30
FORCE_SCORE = 1e4
TINY = 1e-30
ATT_SCALE = HEAD_DIM ** -0.5

LANE = 128
KEY_TILE = 4096
WIN_BAND = NSA_WINDOW + 128
SAMPLE_ROWS = 8
TAIL_ROWS = 256

PROJ_TN = 256
SEG = {}
_off = 0
for _name, _w in (("qa", 1024), ("u", 1024), ("v", 1024), ("qc", 1024), ("qd", 1024), ("fd", 1024),
                  ("vd", 1024), ("gd", 1024), ("kvc", 512), ("kvs", 512), ("kvw", 512), ("kvm", 512),
                  ("ga", 256)):
    SEG[_name] = _off
    _off += _w
N_PROJ = _off

_ORIG = {}
_o = 0
for _name, _w in (("qa", 1024), ("kvc", 512), ("kvs", 512), ("kvw", 512), ("ga", 24), ("u", 1024), ("v", 1024),
                  ("qc", 1024), ("kvm", 512), ("qd", 1024), ("fd", 1024), ("vd", 1024), ("gd", 1024)):
    _ORIG[_name] = (_o, _w)
    _o += _w


def _params(sem, vmem_mb):
    return pltpu.CompilerParams(dimension_semantics=sem, vmem_limit_bytes=vmem_mb << 20)


def _sigmoid(x):
    return 1.0 / (1.0 + jnp.exp(-x))


def _silu(x):
    return x * _sigmoid(x)


def _gelu(x):
    return 0.5 * x * (1.0 + jnp.tanh(0.7978845608028654 * (x + 0.044715 * (x * x * x))))


def _dot(a, b):
    return jnp.dot(a, b, preferred_element_type=F32)


def _dot_nt(a, b):
    return lax.dot_general(a, b, (((1,), (1,)), ((), ())), preferred_element_type=F32)


def _dot_tn(a, b):
    return lax.dot_general(a, b, (((0,), (0,)), ((), ())), preferred_element_type=F32)


def _split3(a):
    hi = a.astype(BF16)
    r1 = a - hi.astype(F32)
    mid = r1.astype(BF16)
    lo = (r1 - mid.astype(F32)).astype(BF16)
    return hi, mid, lo


def _dot_f32_lhs(a, b_exact):
    hi, mid, lo = _split3(a)
    return _dot(hi, b_exact) + _dot(mid, b_exact) + _dot(lo, b_exact)


def _dot_f32_rhs(a_exact, b):
    hi, mid, lo = _split3(b)
    return _dot(a_exact, hi) + _dot(a_exact, mid) + _dot(a_exact, lo)


def _dot_nt_f32(a, b):
    ah, am, _ = _split3(a)
    bh, bm, _ = _split3(b)
    return _dot_nt(ah, bh) + _dot_nt(ah, bm) + _dot_nt(am, bh)


def _rms(x, g, eps=1e-6):
    return x * lax.rsqrt(jnp.mean(x * x, axis=-1, keepdims=True) + eps) * g


def _softmax_parts(s_list, m_list):
    sm = [jnp.where(m, s, NEG_INF) for s, m in zip(s_list, m_list)]
    mx = functools.reduce(jnp.maximum, [jnp.max(s, axis=-1, keepdims=True) for s in sm])
    e = [jnp.where(m, jnp.exp(s - mx), 0.0) for s, m in zip(sm, m_list)]
    den = functools.reduce(lambda a, b: a + b, [jnp.sum(x, axis=-1, keepdims=True) for x in e])
    return e, jnp.maximum(den, 1e-30)


def _rank(score, n, col):
    rank = jnp.zeros(score.shape, jnp.int32)
    for i in range(n):
        si = score[:, i:i + 1]
        before = jnp.where(si > score, 1, jnp.where((si == score) & (col > i), 1, 0))
        rank = rank + before
    return rank


def _ada_kernel(c_ref, w_ref, b_ref, o_ref):
    a = _silu(c_ref[...]).astype(BF16)
    o_ref[...] = _dot(a, w_ref[...].astype(BF16)) + b_ref[...]


def _ada_mod(c, w_ada, b_ada):
    depth, d, n = w_ada.shape
    rows = c.shape[0]
    tn = 512
    return pl.pallas_call(
        _ada_kernel,
        grid=(depth, n // tn),
        in_specs=[pl.BlockSpec((rows, d), lambda l, j: (0, 0)),
                  pl.BlockSpec((None, d, tn), lambda l, j: (l, 0, j)),
                  pl.BlockSpec((None, 1, tn), lambda l, j: (l, 0, j))],
        out_specs=pl.BlockSpec((None, rows, tn), lambda l, j: (l, 0, j)),
        out_shape=jax.ShapeDtypeStruct((depth, rows, n), F32),
        compiler_params=_params(("parallel", "parallel"), 40),
    )(c, w_ada, b_ada.reshape(depth, 1, n))


def _mod_kernel(x_ref, g_ref, sc_ref, sh_ref, o_ref):
    y = _rms(x_ref[...], g_ref[...])
    o_ref[...] = (y * (1.0 + sc_ref[...]) + sh_ref[...]).astype(o_ref.dtype)


def _modulate(x, g, scale, shift, rows_per_group):
    m, d = x.shape
    if rows_per_group == 1:
        tm = m
        vec = pl.BlockSpec((m, d), lambda i: (0, 0))
        sc, sh = scale, shift
    else:
        tm = 256
        per = rows_per_group // tm
        vec = pl.BlockSpec((None, 1, d), lambda i: (i // per, 0, 0))
        sc, sh = scale.reshape(-1, 1, d), shift.reshape(-1, 1, d)
    return pl.pallas_call(
        _mod_kernel,
        grid=(m // tm,),
        in_specs=[pl.BlockSpec((tm, d), lambda i: (i, 0)),
                  pl.BlockSpec((1, d), lambda i: (0, 0)), vec, vec],
        out_specs=pl.BlockSpec((tm, d), lambda i: (i, 0)),
        out_shape=jax.ShapeDtypeStruct((m, d), BF16),
        compiler_params=_params(("parallel",), 40),
    )(x, g.reshape(1, d), sc, sh)


def _glu_kernel(h_ref, wg_ref, wu_ref, o_ref):
    h = h_ref[...]
    a = _dot(h, wg_ref[...].astype(BF16))
    b = _dot(h, wu_ref[...].astype(BF16))
    o_ref[...] = (_silu(a) * b).astype(o_ref.dtype)


def _glu(h, wg, wu, layer, sub):
    m, d = h.shape
    f = wg.shape[-1]
    tm = min(m, 1024)
    tn = 256
    wspec = pl.BlockSpec((None, None, d, tn), lambda i, j: (layer, sub, 0, j))
    return pl.pallas_call(
        _glu_kernel,
        grid=(m // tm, f // tn),
        in_specs=[pl.BlockSpec((tm, d), lambda i, j: (i, 0)), wspec, wspec],
        out_specs=pl.BlockSpec((tm, tn), lambda i, j: (i, j)),
        out_shape=jax.ShapeDtypeStruct((m, f), BF16),
        compiler_params=_params(("parallel", "arbitrary"), 48),
    )(h, wg, wu)


def _mm_resid_kernel(x_ref, w_ref, r_ref, g_ref, o_ref, *, coef):
    acc = _dot(x_ref[...], w_ref[...].astype(BF16))
    o_ref[...] = r_ref[...] + (coef * g_ref[...]) * acc


def _mm_resid(x, w, w_lead, resid, gate, coef, rows_per_group, k_splits):
    m, kdim = x.shape
    n = w.shape[-1]
    tk = kdim // k_splits
    tm = min(m, 1024)
    tn = 256
    nlead = len(w_lead)
    if rows_per_group == 1:
        gspec = pl.BlockSpec((m, tn), lambda i, j: (0, j))
        g = gate
    else:
        per = rows_per_group // tm
        gspec = pl.BlockSpec((None, 1, tn), lambda i, j: (i // per, 0, j))
        g = gate.reshape(-1, 1, n)
    out = resid
    for ks in range(k_splits):
        out = pl.pallas_call(
            functools.partial(_mm_resid_kernel, coef=coef),
            grid=(m // tm, n // tn),
            in_specs=[pl.BlockSpec((tm, tk), lambda i, j, ks=ks: (i, ks)),
                      pl.BlockSpec((None,) * nlead + (tk, tn), lambda i, j, ks=ks: tuple(w_lead) + (ks, j)),
                      pl.BlockSpec((tm, tn), lambda i, j: (i, j)),
                      gspec],
            out_specs=pl.BlockSpec((tm, tn), lambda i, j: (i, j)),
            out_shape=jax.ShapeDtypeStruct((m, n), F32),
            compiler_params=_params(("parallel", "arbitrary"), 48),
        )(x, w, out, g)
    return out


def _tiles(name, width):
    a = SEG[name] // PROJ_TN
    return a, a + width // PROJ_TN


_NORM_TILES = (_tiles("qa", 1024), _tiles("qc", 1024), _tiles("kvs", 256), _tiles("kvw", 256), _tiles("kvm", 256))
_GELU_TILES = (_tiles("u", 2048),)
_SILU_TILES = (_tiles("qd", 1024),)
_SIGM_TILES = (_tiles("ga", 256),)


def _in_ranges(j, ranges):
    return functools.reduce(jnp.logical_or, [(j >= a) & (j < b) for a, b in ranges])


def _proj_kernel(h_ref, w_ref, g_ref, o_ref):
    j = pl.program_id(1)
    acc = _dot(h_ref[...], w_ref[...])
    is_norm = _in_ranges(j, _NORM_TILES)
    is_gelu = _in_ranges(j, _GELU_TILES)
    is_silu = _in_ranges(j, _SILU_TILES)
    is_sigm = _in_ranges(j, _SIGM_TILES)

    @pl.when(is_norm)
    def _():
        g = g_ref[...]
        for hh in range(PROJ_TN // HEAD_DIM):
            sl = slice(hh * HEAD_DIM, (hh + 1) * HEAD_DIM)
            o_ref[:, sl] = _rms(acc[:, sl], g[:, sl])

    @pl.when(is_gelu)
    def _():
        o_ref[...] = _gelu(acc)

    @pl.when(is_silu)
    def _():
        o_ref[...] = _silu(acc) * ATT_SCALE

    @pl.when(is_sigm)
    def _():
        o_ref[...] = _sigmoid(acc)

    @pl.when(jnp.logical_not(is_norm | is_gelu | is_silu | is_sigm))
    def _():
        o_ref[...] = acc


def _proj(h, w_in_p, gains):
    m, d = h.shape
    tm = min(m, 1024)
    nt = N_PROJ // PROJ_TN
    return pl.pallas_call(
        _proj_kernel,
        grid=(m // tm, nt),
        in_specs=[pl.BlockSpec((tm, d), lambda i, j: (i, 0)),
                  pl.BlockSpec((d, PROJ_TN), lambda i, j: (0, j)),
                  pl.BlockSpec((None, 1, PROJ_TN), lambda i, j: (j, 0, 0))],
        out_specs=pl.BlockSpec((tm, PROJ_TN), lambda i, j: (i, j)),
        out_shape=jax.ShapeDtypeStruct((m, N_PROJ), F32),
        compiler_params=_params(("parallel", "arbitrary"), 40),
    )(h, w_in_p, gains)


def _permute_w_in(w_in_l):
    d = w_in_l.shape[0]

    def cols(name):
        a, w = _ORIG[name]
        return w_in_l[:, a:a + w]

    ga = cols("ga")
    pad = jnp.zeros((d, LANE - 12), w_in_l.dtype)
    parts = [cols(n) for n in ("qa", "u", "v", "qc", "qd", "fd", "vd", "gd", "kvc", "kvs", "kvw", "kvm")]
    parts += [ga[:, :12], pad, ga[:, 12:], pad]
    return jnp.concatenate(parts, axis=1).astype(BF16)


def _proj_gains(qkg):
    g = jnp.ones((N_PROJ // PROJ_TN, PROJ_TN), F32)
    two = lambda v: jnp.tile(v, PROJ_TN // HEAD_DIM)
    for name, idx, ntile in (("qa", 0, 4), ("qc", 4, 4), ("kvs", 2, 1), ("kvw", 3, 1), ("kvm", 5, 1)):
        t0 = SEG[name] // PROJ_TN
        g = g.at[t0:t0 + ntile].set(two(qkg[idx])[None, :])
    return g.reshape(-1, 1, PROJ_TN)


CMP_ROWS = 128


def _compress_kernel(r_ref, rn_ref, w1_ref, b1_ref, w2_ref, pe_ref, gk_ref, o_ref, *, n_cmp):
    i = pl.program_id(1)
    row = lax.broadcasted_iota(jnp.int32, (CMP_ROWS, 1), 0)
    valid = (i * CMP_ROWS + row) < n_cmp
    half = NSA_CMP_STRIDE * HEAD_DIM
    for s in range(2):
        w1a = w1_ref[s, 0].astype(BF16)
        w1b = w1_ref[s, 1].astype(BF16)
        pe = pe_ref[s].astype(BF16)
        bias = (_dot(pe[:, :half], w1a) + _dot(pe[:, half:], w1b))[0:1] + b1_ref[s]
        w2 = w2_ref[s].astype(BF16)
        for k in range(KV_HEADS):
            c0 = (s * KV_HEADS + k) * HEAD_DIM
            row_w = 2 * KV_HEADS * HEAD_DIM
            x = jnp.concatenate([r_ref[:, p * row_w + c0:p * row_w + c0 + HEAD_DIM]
                                 for p in range(NSA_CMP_STRIDE)], axis=1).astype(BF16)
            xn = jnp.concatenate([rn_ref[:, p * row_w + c0:p * row_w + c0 + HEAD_DIM]
                                  for p in range(NSA_CMP_STRIDE)], axis=1).astype(BF16)
            h1 = _dot(x, w1a)
            h2 = _dot(x, w1b)
            h2n = _dot(xn, w1b)
            h2s = jnp.where(row == CMP_ROWS - 1, h2n[0:1], pltpu.roll(h2, CMP_ROWS - 1, 0))
            hid = _gelu(h1 + h2s + bias)
            out = _dot(hid.astype(BF16), w2)
            if s == 0:
                out = _rms(out, gk_ref[...])
            o_ref[s, k] = jnp.where(valid, out, 0.0)


def _compress(kv_flat, batch, w1, b1, w2, pe, gk):
    n_half = kv_flat.shape[0] // batch // NSA_CMP_STRIDE
    nblk = n_half // CMP_ROWS
    width = NSA_CMP_STRIDE * kv_flat.shape[1]
    r = kv_flat.reshape(batch * n_half, width)
    last8 = batch * n_half // 8 - 1
    half = NSA_CMP_STRIDE * HEAD_DIM
    w1r = w1.reshape(2, 2, half, NSA_CMP_HIDDEN)
    pe8 = jnp.broadcast_to(pe.reshape(2, 1, 2 * half), (2, 8, 2 * half))
    full = lambda shape: pl.BlockSpec(shape, lambda b, i: (0,) * len(shape))
    return pl.pallas_call(
        functools.partial(_compress_kernel, n_cmp=n_half - 1),
        grid=(batch, nblk),
        in_specs=[pl.BlockSpec((CMP_ROWS, width), lambda b, i: (b * nblk + i, 0)),
                  pl.BlockSpec((8, width), lambda b, i: (jnp.minimum((b * nblk + i + 1) * (CMP_ROWS // 8), last8), 0)),
                  full(w1r.shape), full((2, 1, NSA_CMP_HIDDEN)), full(w2.shape), full(pe8.shape),
                  full((1, HEAD_DIM))],
        out_specs=pl.BlockSpec((None, 2, KV_HEADS, CMP_ROWS, HEAD_DIM), lambda b, i: (b, 0, 0, i, 0)),
        out_shape=jax.ShapeDtypeStruct((batch, 2, KV_HEADS, n_half, HEAD_DIM), F32),
        compiler_params=_params(("parallel", "arbitrary"), 48),
    )(r, r, w1r, b1.reshape(2, 1, NSA_CMP_HIDDEN), w2, pe8, gk.reshape(1, HEAD_DIM))


def _stack_heads(q):
    return jnp.concatenate([q[:, g * HEAD_DIM:(g + 1) * HEAD_DIM] for g in range(GROUP)], axis=0)


def _block_onehot(n_blocks, kpos, shift):
    blk = lax.broadcasted_iota(jnp.int32, (n_blocks, 1), 0)
    return jnp.where((kpos >> shift) == blk, 1.0, 0.0).astype(BF16)


def _nsa_kernel(*refs, tq, qpos0, n_sel, n_main, has_tail, win_pos0, win_len):
    if has_tail:
        q_ref, ga_ref, kc_ref, vc_ref, ks_ref, vs_ref, kt_ref, vt_ref, kw_ref, vw_ref, o_ref = refs
    else:
        q_ref, ga_ref, kc_ref, vc_ref, ks_ref, vs_ref, kw_ref, vw_ref, o_ref = refs
    q0 = qpos0 + pl.program_id(2) * tq
    rows = GROUP * tq
    q4 = _stack_heads(q_ref[...]).astype(BF16)
    qpos = q0 + (lax.broadcasted_iota(jnp.int32, (rows, 1), 0) & (tq - 1))

    kc = kc_ref[...].astype(BF16)
    vc = vc_ref[...].astype(BF16)
    n_cp = kc.shape[0]
    cidx = lax.broadcasted_iota(jnp.int32, (1, n_cp), 1)
    s_cmp = _dot_nt(q4, kc) * ATT_SCALE
    (e_cmp,), den = _softmax_parts([s_cmp], [(cidx * NSA_CMP_STRIDE + (NSA_CMP_BLOCK - 1)) <= qpos])
    p_cmp = e_cmp / den
    o_cmp = _dot(p_cmp.astype(BF16), vc)

    n_sp = -(-n_sel // LANE) * LANE
    psum = functools.reduce(lambda a, b: a + b, [p_cmp[g * tq:(g + 1) * tq] for g in range(GROUP)])
    ci = lax.broadcasted_iota(jnp.int32, (n_cp, 1), 0) * NSA_CMP_STRIDE
    sj = lax.broadcasted_iota(jnp.int32, (1, n_sp), 1) * NSA_SEL_BLOCK
    cover = jnp.where((ci <= sj + (NSA_SEL_BLOCK - 1)) & (ci + (NSA_CMP_BLOCK - 1) >= sj), 1.0, 0.0).astype(BF16)
    imp = _dot_f32_lhs(psum, cover)
    col = lax.broadcasted_iota(jnp.int32, (1, n_sp), 1)
    qpos_t = q0 + lax.broadcasted_iota(jnp.int32, (tq, 1), 0)
    back = (qpos_t >> 6) - col
    valid = back >= 0
    forced = (col == 0) | (valid & (back < NSA_N_LOCAL))
    score = jnp.where(valid, jnp.where(forced, FORCE_SCORE, imp), NEG_INF)
    rank = _rank(score, n_sel, col)
    sel = jnp.where((rank < NSA_SEL_TOP) & (score > 0.5 * NEG_INF), 1.0, 0.0).astype(BF16)
    sel4 = jnp.concatenate([sel] * GROUP, axis=0)

    s_list, m_list, v_list = [], [], []
    kt = ks_ref.shape[0] // n_main
    for t in range(n_main):
        kpos = t * kt + lax.broadcasted_iota(jnp.int32, (1, kt), 1)
        picked = _dot(sel4, _block_onehot(n_sp, kpos, 6))
        s_list.append(_dot_nt(q4, ks_ref[t * kt:(t + 1) * kt, :].astype(BF16)) * ATT_SCALE)
        m_list.append(jnp.where(kpos <= qpos, picked, 0.0) > 0.5)
        v_list.append(vs_ref[t * kt:(t + 1) * kt, :].astype(BF16))
    if has_tail:
        kpos = ks_ref.shape[0] + lax.broadcasted_iota(jnp.int32, (1, TAIL_ROWS), 1)
        picked = _dot(sel4, _block_onehot(n_sp, kpos, 6))
        s_list.append(_dot_nt(q4, kt_ref[...].astype(BF16)) * ATT_SCALE)
        m_list.append(jnp.where(kpos <= qpos, picked, 0.0) > 0.5)
        v_list.append(vt_ref[...].astype(BF16))
    e_list, den = _softmax_parts(s_list, m_list)
    o_sel = functools.reduce(lambda a, b: a + b, [_dot(e.astype(BF16), v) for e, v in zip(e_list, v_list)]) / den

    start = jnp.clip(q0 - NSA_WINDOW - win_pos0, 0, win_len - WIN_BAND)
    start = pl.multiple_of(start, LANE)
    kw = kw_ref[pl.ds(start, WIN_BAND), :].astype(BF16)
    vw = vw_ref[pl.ds(start, WIN_BAND), :].astype(BF16)
    kposw = win_pos0 + start + lax.broadcasted_iota(jnp.int32, (1, WIN_BAND), 1)
    s_win = _dot_nt(q4, kw) * ATT_SCALE
    m_win = jnp.where(kposw <= qpos, qpos - kposw, NSA_WINDOW) < NSA_WINDOW
    (e_win,), den = _softmax_parts([s_win], [m_win])
    o_win = _dot(e_win.astype(BF16), vw) / den

    ga = ga_ref[...]
    outs = []
    for g in range(GROUP):
        r = slice(g * tq, (g + 1) * tq)
        outs.append(ga[:, 3 * g:3 * g + 1] * o_cmp[r] + ga[:, 3 * g + 1:3 * g + 2] * o_sel[r]
                    + ga[:, 3 * g + 2:3 * g + 3] * o_win[r])
    o_ref[...] = jnp.concatenate(outs, axis=1).astype(o_ref.dtype)


def _nsa(zq, tq_total, tq, qpos0, cmp_kv, sel_main, sel_cols, sel_tail, win, win_cols, win_pos0, n_total):
    batch = cmp_kv.shape[0]
    nq = tq_total // tq
    n_cp = cmp_kv.shape[3]
    tk = sel_main.shape[0] // batch
    lw = win.shape[0] // batch
    n_sel = -(-n_total // NSA_SEL_BLOCK)
    kt = min(tk, KEY_TILE)
    qa_blk = SEG["qa"] // (GROUP * HEAD_DIM)
    ga_blk = SEG["ga"] // LANE
    in_specs = [
        pl.BlockSpec((tq, GROUP * HEAD_DIM), lambda b, k, i: (b * nq + i, qa_blk + k)),
        pl.BlockSpec((tq, LANE), lambda b, k, i: (b * nq + i, ga_blk + k)),
        pl.BlockSpec((None, None, None, n_cp, HEAD_DIM), lambda b, k, i: (b, 0, k, 0, 0)),
        pl.BlockSpec((None, None, None, n_cp, HEAD_DIM), lambda b, k, i: (b, 1, k, 0, 0)),
        pl.BlockSpec((tk, HEAD_DIM), lambda b, k, i: (b, sel_cols[0] + k)),
        pl.BlockSpec((tk, HEAD_DIM), lambda b, k, i: (b, sel_cols[1] + k)),
    ]
    args = [zq, zq, cmp_kv, cmp_kv, sel_main, sel_main]
    if sel_tail is not None:
        in_specs += [pl.BlockSpec((TAIL_ROWS, HEAD_DIM), lambda b, k, i: (b, k)),
                     pl.BlockSpec((TAIL_ROWS, HEAD_DIM), lambda b, k, i: (b, KV_HEADS + k))]
        args += [sel_tail, sel_tail]
    in_specs += [pl.BlockSpec((lw, HEAD_DIM), lambda b, k, i: (b, win_cols[0] + k)),
                 pl.BlockSpec((lw, HEAD_DIM), lambda b, k, i: (b, win_cols[1] + k))]
    args += [win, win]
    return pl.pallas_call(
        functools.partial(_nsa_kernel, tq=tq, qpos0=qpos0, n_sel=n_sel, n_main=tk // kt,
                          has_tail=sel_tail is not None, win_pos0=win_pos0, win_len=lw),
        grid=(batch, KV_HEADS, nq),
        in_specs=in_specs,
        out_specs=pl.BlockSpec((tq, GROUP * HEAD_DIM), lambda b, k, i: (b * nq + i, k)),
        out_shape=jax.ShapeDtypeStruct((batch * tq_total, WIDTH), BF16),
        compiler_params=_params(("parallel", "parallel", "arbitrary"), 56),
    )(*args)


def _moba_kernel(*refs, tq, qpos0, n_main, has_tail):
    if has_tail:
        q_ref, k_ref, v_ref, kt_ref, vt_ref, o_ref = refs
    else:
        q_ref, k_ref, v_ref, o_ref = refs
    q0 = qpos0 + pl.program_id(2) * tq
    rows = GROUP * tq
    q4f = _stack_heads(q_ref[...])
    q4 = q4f.astype(BF16)
    qpos = q0 + (lax.broadcasted_iota(jnp.int32, (rows, 1), 0) & (tq - 1))
    cur = qpos >> 8
    tk = k_ref.shape[0]
    kt = tk // n_main
    n_blk = tk // MOBA_BLOCK

    means = [jnp.sum(k_ref[t * kt:(t + 1) * kt, :].reshape(kt // MOBA_BLOCK, MOBA_BLOCK, HEAD_DIM), axis=1)
             * (1.0 / MOBA_BLOCK) for t in range(n_main)]
    if n_blk < LANE:
        means.append(jnp.zeros((LANE - n_blk, HEAD_DIM), F32))
    kmean = jnp.concatenate(means, axis=0)
    col = lax.broadcasted_iota(jnp.int32, (1, LANE), 1)
    gate = jnp.where(col < cur, _dot_nt_f32(q4f, kmean), NEG_INF)
    rank = _rank(gate, n_blk, col)
    sel = jnp.where((rank < MOBA_TOP) & (gate > 0.5 * NEG_INF), 1.0, 0.0).astype(BF16)

    s_list, m_list, v_list = [], [], []

    def add_tile(k_bf, v_bf, kpos):
        picked = _dot(sel, _block_onehot(LANE, kpos, 8))
        own = jnp.where((kpos >> 8) == cur, jnp.where(kpos <= qpos, 1.0, 0.0), 0.0)
        s_list.append(_dot_nt(q4, k_bf) * ATT_SCALE)
        m_list.append((picked + own) > 0.5)
        v_list.append(v_bf)

    for t in range(n_main):
        add_tile(k_ref[t * kt:(t + 1) * kt, :].astype(BF16), v_ref[t * kt:(t + 1) * kt, :].astype(BF16),
                 t * kt + lax.broadcasted_iota(jnp.int32, (1, kt), 1))
    if has_tail:
        add_tile(kt_ref[...].astype(BF16), vt_ref[...].astype(BF16),
                 tk + lax.broadcasted_iota(jnp.int32, (1, TAIL_ROWS), 1))
    e_list, den = _softmax_parts(s_list, m_list)
    o = functools.reduce(lambda a, b: a + b, [_dot(e.astype(BF16), v) for e, v in zip(e_list, v_list)]) / den
    o_ref[...] = jnp.concatenate([o[g * tq:(g + 1) * tq] for g in range(GROUP)], axis=1).astype(o_ref.dtype)


def _moba(zq, tq_total, tq, qpos0, batch, main, cols, tail):
    nq = tq_total // tq
    tk = main.shape[0] // batch
    kt = min(tk, KEY_TILE)
    qc_blk = SEG["qc"] // (GROUP * HEAD_DIM)
    in_specs = [pl.BlockSpec((tq, GROUP * HEAD_DIM), lambda b, k, i: (b * nq + i, qc_blk + k)),
                pl.BlockSpec((tk, HEAD_DIM), lambda b, k, i: (b, cols[0] + k)),
                pl.BlockSpec((tk, HEAD_DIM), lambda b, k, i: (b, cols[1] + k))]
    args = [zq, main, main]
    if tail is not None:
        in_specs += [pl.BlockSpec((TAIL_ROWS, HEAD_DIM), lambda b, k, i: (b, k)),
                     pl.BlockSpec((TAIL_ROWS, HEAD_DIM), lambda b, k, i: (b, KV_HEADS + k))]
        args += [tail, tail]
    return pl.pallas_call(
        functools.partial(_moba_kernel, tq=tq, qpos0=qpos0, n_main=tk // kt, has_tail=tail is not None),
        grid=(batch, KV_HEADS, nq),
        in_specs=in_specs,
        out_specs=pl.BlockSpec((tq, GROUP * HEAD_DIM), lambda b, k, i: (b * nq + i, k)),
        out_shape=jax.ShapeDtypeStruct((batch * tq_total, WIDTH), BF16),
        compiler_params=_params(("parallel", "parallel", "arbitrary"), 56),
    )(*args)


def _layer_norm(v, g, b, eps=1e-5):
    mu = jnp.mean(v, axis=-1, keepdims=True)
    var = jnp.mean(jnp.square(v - mu), axis=-1, keepdims=True)
    return (v - mu) * lax.rsqrt(var + eps) * g + b


def _gmlp_kernel(u_ref, v_ref, ws_ref, bst_ref, g_ref, b_ref, o_ref):
    vn = _layer_norm(v_ref[...], g_ref[...], b_ref[...]).astype(BF16)
    r = lax.broadcasted_iota(jnp.int32, (GMLP_CHUNK, GMLP_CHUNK), 0)
    c = lax.broadcasted_iota(jnp.int32, (GMLP_CHUNK, GMLP_CHUNK), 1)
    bst = bst_ref[...]
    for g in range(N_HEADS):
        sl = slice(g * HEAD_DIM, (g + 1) * HEAD_DIM)
        wm = jnp.where(c <= r, ws_ref[g], 0.0).astype(BF16)
        sv = _dot(wm, vn[:, sl]) + bst[:, g:g + 1]
        o_ref[:, sl] = (u_ref[:, sl] * sv).astype(o_ref.dtype)


def _gmlp(z, ws, bs, ln_g, ln_b):
    m = z.shape[0]
    ub, vb = SEG["u"] // WIDTH, SEG["v"] // WIDTH
    full = lambda shape: pl.BlockSpec(shape, lambda i: (0,) * len(shape))
    return pl.pallas_call(
        _gmlp_kernel,
        grid=(m // GMLP_CHUNK,),
        in_specs=[pl.BlockSpec((GMLP_CHUNK, WIDTH), lambda i: (i, ub)),
                  pl.BlockSpec((GMLP_CHUNK, WIDTH), lambda i: (i, vb)),
                  full(ws.shape), full((GMLP_CHUNK, N_HEADS)), full((1, WIDTH)), full((1, WIDTH))],
        out_specs=pl.BlockSpec((GMLP_CHUNK, WIDTH), lambda i: (i, 0)),
        out_shape=jax.ShapeDtypeStruct((m, WIDTH), BF16),
        compiler_params=_params(("parallel",), 32),
    )(z, z, ws, bs.T, ln_g.reshape(1, WIDTH), ln_b.reshape(1, WIDTH))


def _hgrn_gates(fd, lb):
    f_gate = lb + (1.0 - lb) * _sigmoid(fd)
    return jnp.log(jnp.maximum(f_gate, TINY)), (1.0 - lb) * _sigmoid(-fd)


def _hgrn_readout(o, gd, g_out):
    return _rms(o, g_out) * _silu(gd)


def _hgrn_kernel(q_ref, f_ref, v_ref, gd_ref, lb_ref, go_ref, o_ref, s_ref):
    c = HGRN_CHUNK
    t_total = q_ref.shape[0]
    lb = lb_ref[...]
    g_out = go_ref[...]
    ri = lax.broadcasted_iota(jnp.int32, (c, c), 0)
    ci = lax.broadcasted_iota(jnp.int32, (c, c), 1)
    tri = jnp.where(ci <= ri, 1.0, 0.0).astype(BF16)
    lane = lax.broadcasted_iota(jnp.int32, (1, c), 1)
    diag_mask = (ci <= ri) & ((ci >> 3) == (ri >> 3))

    def chunk(n, st):
        r0 = pl.multiple_of(n * c, c)
        q = q_ref[pl.ds(r0, c), :]
        v = v_ref[pl.ds(r0, c), :]
        lf, k = _hgrn_gates(f_ref[pl.ds(r0, c), :], lb)
        cum = _dot_f32_rhs(tri, lf)
        a = jnp.zeros((c, c), F32)
        for bs in (32, 16, 8):
            nb = c // bs
            refq = jnp.concatenate(
                [jnp.zeros((bs, HEAD_DIM), F32)]
                + [jnp.broadcast_to(cum[b * bs - 1:b * bs], (bs, HEAD_DIM)) for b in range(1, nb)], axis=0)
            refk = jnp.concatenate(
                [jnp.broadcast_to(cum[(b + 1) * bs - 1:(b + 1) * bs], (bs, HEAD_DIM)) for b in range(nb)], axis=0)
            qt = (q * jnp.exp(jnp.minimum(cum - refq, 0.0))).astype(BF16)
            kt = (k * jnp.exp(jnp.minimum(refk - cum, 0.0))).astype(BF16)
            sh = bs.bit_length() - 1
            lvl = (((ri >> sh) & 1) == 1) & ((ci >> sh) == (ri >> sh) - 1)
            a = a + jnp.where(lvl, _dot_nt(qt, kt), 0.0)
        rows = []
        for blk in range(c // 8):
            b0 = blk * 8
            qb, kb, cb = q[b0:b0 + 8], k[b0:b0 + 8], cum[b0:b0 + 8]
            acc = jnp.zeros((8, c), F32)
            for s in range(8):
                w = qb * kb[s:s + 1] * jnp.exp(jnp.minimum(cb - cb[s:s + 1], 0.0))
                acc = jnp.where(lane == b0 + s, jnp.sum(w, axis=1, keepdims=True), acc)
            rows.append(acc)
        a = a + jnp.where(diag_mask, jnp.concatenate(rows, axis=0), 0.0)
        o = _dot(a.astype(BF16), v.astype(BF16)) + _dot_nt((q * jnp.exp(cum)).astype(BF16), st.astype(BF16))
        o_ref[pl.ds(r0, c), :] = _hgrn_readout(o, gd_ref[pl.ds(r0, c), :], g_out).astype(o_ref.dtype)
        last = cum[c - 1:c]
        kk = (k * jnp.exp(last - cum)).astype(BF16)
        return st * jnp.exp(last) + _dot_tn(v.astype(BF16), kk)

    st = lax.fori_loop(0, t_total // c, chunk, jnp.zeros((HEAD_DIM, HEAD_DIM), F32))
    s_ref[...] = st.T


def _hgrn(z, batch, lb, g_out):
    t = z.shape[0] // batch
    blk = lambda name: SEG[name] // HEAD_DIM
    col = lambda name: pl.BlockSpec((t, HEAD_DIM), lambda b, h: (b, blk(name) + h))
    return pl.pallas_call(
        _hgrn_kernel,
        grid=(batch, N_HEADS),
        in_specs=[col("qd"), col("fd"), col("vd"), col("gd"),
                  pl.BlockSpec((1, HEAD_DIM), lambda b, h: (0, h)),
                  pl.BlockSpec((1, HEAD_DIM), lambda b, h: (0, 0))],
        out_specs=[pl.BlockSpec((t, HEAD_DIM), lambda b, h: (b, h)),
                   pl.BlockSpec((None, None, HEAD_DIM, HEAD_DIM), lambda b, h: (b, h, 0, 0))],
        out_shape=[jax.ShapeDtypeStruct((batch * t, WIDTH), BF16),
                   jax.ShapeDtypeStruct((batch, N_HEADS, HEAD_DIM, HEAD_DIM), F32)],
        compiler_params=_params(("parallel", "parallel"), 32),
    )(z, z, z, z, lb.reshape(1, WIDTH), g_out.reshape(1, HEAD_DIM))


def _gmlp_step_kernel(u_ref, v_ref, w_ref, b_ref, g_ref, bb_ref, o_ref, vn_ref):
    vn = _layer_norm(v_ref[...], g_ref[...], bb_ref[...])
    vn_ref[...] = vn
    o_ref[...] = (u_ref[...] * (w_ref[...] * vn + b_ref[...])).astype(o_ref.dtype)


def _gmlp_step(u, v, ws, bs, ln_g, ln_b):
    rows = u.shape[0]
    w_row = jnp.repeat(ws[:, 0, 0], HEAD_DIM).reshape(1, WIDTH)
    b_row = jnp.repeat(bs[:, 0], HEAD_DIM).reshape(1, WIDTH)
    return pl.pallas_call(
        _gmlp_step_kernel,
        out_shape=[jax.ShapeDtypeStruct((rows, WIDTH), BF16), jax.ShapeDtypeStruct((rows, WIDTH), F32)],
    )(u, v, w_row, b_row, ln_g.reshape(1, WIDTH), ln_b.reshape(1, WIDTH))


def _hgrn_step_kernel(qc_ref, fc_ref, lbc_ref, v_ref, gd_ref, go_ref, s0_ref, o_ref, s_ref):
    lf, k = _hgrn_gates(fc_ref[...], lbc_ref[...])
    s_new = jnp.exp(lf) * s0_ref[...] + k * v_ref[...]
    s_ref[...] = s_new
    o = jnp.sum(qc_ref[...] * s_new, axis=0, keepdims=True)
    o_ref[...] = _hgrn_readout(o, gd_ref[...], go_ref[...]).astype(o_ref.dtype)


def _hgrn_step(qd, fd, vd, gd, lb, g_out, state, layer):
    batch = qd.shape[0]
    colv = lambda a: a.reshape(batch, N_HEADS, HEAD_DIM, 1)
    rowv = lambda a: a.reshape(batch, N_HEADS, 1, HEAD_DIM)
    cspec = pl.BlockSpec((None, None, HEAD_DIM, 1), lambda b, h: (b, h, 0, 0))
    rspec = pl.BlockSpec((None, None, 1, HEAD_DIM), lambda b, h: (b, h, 0, 0))
    return pl.pallas_call(
        _hgrn_step_kernel,
        grid=(batch, N_HEADS),
        in_specs=[cspec, cspec, pl.BlockSpec((None, HEAD_DIM, 1), lambda b, h: (h, 0, 0)), rspec, rspec,
                  pl.BlockSpec((1, HEAD_DIM), lambda b, h: (0, 0)),
                  pl.BlockSpec((None, None, None, HEAD_DIM, HEAD_DIM), lambda b, h: (layer, b, h, 0, 0))],
        out_specs=[rspec, pl.BlockSpec((None, None, HEAD_DIM, HEAD_DIM), lambda b, h: (b, h, 0, 0))],
        out_shape=[jax.ShapeDtypeStruct((batch, N_HEADS, 1, HEAD_DIM), BF16),
                   jax.ShapeDtypeStruct((batch, N_HEADS, HEAD_DIM, HEAD_DIM), F32)],
        compiler_params=_params(("parallel", "parallel"), 32),
    )(colv(qd), colv(fd), lb.reshape(N_HEADS, HEAD_DIM, 1), rowv(vd), rowv(gd), g_out.reshape(1, HEAD_DIM), state)


def _gather_kernel(pt_ref, *refs):
    o_ref = refs[-1]
    for r in range(PAGES_PER_STEP):
        o_ref[r * PAGE_SIZE:(r + 1) * PAGE_SIZE, :] = refs[r][...]


def _gather_pages(pool, layer, page_table):
    depth, n_phys = pool.shape[:2]
    width = pool.shape[3] * pool.shape[4] * pool.shape[5]
    view = pool.reshape(depth, n_phys, PAGE_SIZE, width)
    batch, n_pages = page_table.shape
    groups = n_pages // PAGES_PER_STEP
    rows = PAGES_PER_STEP * PAGE_SIZE
    in_specs = [pl.BlockSpec((None, None, PAGE_SIZE, width),
                             lambda b, g, pt, r=r: (layer, pt[b, g * PAGES_PER_STEP + r], 0, 0))
                for r in range(PAGES_PER_STEP)]
    return pl.pallas_call(
        _gather_kernel,
        grid_spec=pltpu.PrefetchScalarGridSpec(
            num_scalar_prefetch=1, grid=(batch, groups), in_specs=in_specs,
            out_specs=pl.BlockSpec((rows, width), lambda b, g, pt: (b * groups + g, 0))),
        out_shape=jax.ShapeDtypeStruct((batch * n_pages * PAGE_SIZE, width), pool.dtype),
        compiler_params=_params(("parallel", "arbitrary"), 40),
    )(page_table, *([view] * PAGES_PER_STEP))


def _ffn(x, mod_l, sub, mod_idx, g, wg, wu, wd, layer, rows_per_group):
    h = _modulate(x, g, mod_l[:, mod_idx, 1], mod_l[:, mod_idx, 0], rows_per_group)
    act = _glu(h, wg, wu, layer, sub)
    return _mm_resid(act, wd, (layer, sub), x, mod_l[:, mod_idx, 2], 0.5, rows_per_group, 2)


def _kv5(a, batch):
    return a.reshape(batch, -1, 2, KV_HEADS, HEAD_DIM)


def _prompt_mix(h, batch, w_in_p, gains, mixw):
    t = h.shape[0] // batch
    z = _proj(h, w_in_p, gains)
    seg = lambda name, w: z[:, SEG[name]:SEG[name] + w]
    kvc, kvs, kvw, kvm = seg("kvc", 512), seg("kvs", 512), seg("kvw", 512), seg("kvm", 512)
    cmp_kv = _compress(kvc, batch, mixw["cw1"], mixw["cb1"], mixw["cw2"], mixw["cpe"], mixw["qkg"][1])
    kb = lambda name: SEG[name] // HEAD_DIM
    o_a = _nsa(z, t, 128, 0, cmp_kv, z, (kb("kvs"), kb("kvs") + KV_HEADS), None,
               z, (kb("kvw"), kb("kvw") + KV_HEADS), 0, t)
    o_b = _gmlp(z, mixw["ws"], mixw["bs"], mixw["ln_g"], mixw["ln_b"])
    o_c = _moba(z, t, 128, 0, batch, z, (kb("kvm"), kb("kvm") + KV_HEADS), None)
    o_d, s_d = _hgrn(z, batch, mixw["lb"], mixw["out_g"])
    mixed = jnp.concatenate([o_a, o_b, o_c, o_d], axis=1)
    w = min(NSA_WINDOW, t)
    state = (_kv5(kvc, batch), _kv5(kvs, batch), _kv5(kvm, batch), _kv5(kvw, batch)[:, t - w:], s_d)
    return mixed, state


def _pad_rows(a, batch, rows):
    out = jnp.zeros((batch, rows, a.shape[1]), a.dtype).at[:, 0].set(a)
    return out.reshape(batch * rows, a.shape[1])


def _sample_mix(h, batch, w_in_p, gains, mixw, pool_cmp, pool_sel, pool_moba, win_buf, state, page_table, layer):
    past = page_table.shape[1] * PAGE_SIZE
    z = _proj(h, w_in_p, gains)[:batch]
    seg = lambda name, w: z[:, SEG[name]:SEG[name] + w]
    kvc, kvs, kvw, kvm = seg("kvc", 512), seg("kvs", 512), seg("kvw", 512), seg("kvm", 512)
    zq = _pad_rows(z, batch, SAMPLE_ROWS)
    dense_cmp = _gather_pages(pool_cmp, layer, page_table)
    dense_sel = _gather_pages(pool_sel, layer, page_table)
    dense_moba = _gather_pages(pool_moba, layer, page_table)
    cmp_kv = _compress(dense_cmp, batch, mixw["cw1"], mixw["cb1"], mixw["cw2"], mixw["cpe"], mixw["qkg"][1])
    wlen = win_buf.shape[2]
    band = jnp.concatenate([win_buf[layer].reshape(batch, wlen, 512), kvw[:, None, :]], axis=1)
    win = jnp.concatenate([band, jnp.zeros((batch, WIN_BAND - wlen - 1, 512), F32)], axis=1)
    o_a = _nsa(zq, SAMPLE_ROWS, SAMPLE_ROWS, past, cmp_kv, dense_sel, (0, KV_HEADS),
               _pad_rows(kvs, batch, TAIL_ROWS), win.reshape(batch * WIN_BAND, 512), (0, KV_HEADS),
               past - wlen, past + 1)
    o_c = _moba(zq, SAMPLE_ROWS, SAMPLE_ROWS, past, batch, dense_moba, (0, KV_HEADS),
                _pad_rows(kvm, batch, TAIL_ROWS))
    o_b, v_n = _gmlp_step(seg("u", WIDTH), seg("v", WIDTH), mixw["ws"], mixw["bs"], mixw["ln_g"], mixw["ln_b"])
    o_d, s_d = _hgrn_step(seg("qd", WIDTH), seg("fd", WIDTH), seg("vd", WIDTH), seg("gd", WIDTH),
                          mixw["lb"], mixw["out_g"], state, layer)
    mixed = jnp.concatenate([o_a[::SAMPLE_ROWS], o_b, o_c[::SAMPLE_ROWS], o_d.reshape(batch, WIDTH)], axis=1)
    mixed = jnp.concatenate([mixed, jnp.zeros((h.shape[0] - batch, mixed.shape[1]), mixed.dtype)], axis=0)
    new_state = (_kv5(kvc, batch), _kv5(kvs, batch), _kv5(kvm, batch),
                 band[:, 1:].reshape(batch, wlen, 2, KV_HEADS, HEAD_DIM), s_d, v_n.reshape(batch, 1, WIDTH))
    return mixed, new_state


def kernel(x_prompt, x_sample, c_prompt, c_sample, cache_nsa_cmp_kv, cache_nsa_sel_kv, cache_moba_kv, cache_nsa_win_kv, state_hgrn, page_table, w_ada, b_ada, norm_g, w_ffn_gate, w_ffn_up, w_ffn_down, w_in, w_out, qk_norm_g, nsa_cmp_w1, nsa_cmp_b1, nsa_cmp_w2, nsa_cmp_pos, gmlp_ln_g, gmlp_ln_b, gmlp_ws, gmlp_bs, hgrn_lb_logits, hgrn_out_g):
    bp, t, d = x_prompt.shape
    bs = x_sample.shape[0]
    depth = w_in.shape[0]
    srows = 16
    lb_p = jax.nn.softmax(hgrn_lb_logits.astype(F32), axis=0)
    lb_all = jnp.clip(jnp.cumsum(lb_p, axis=0) - lb_p[0:1], 0.0, 1.0)

    c_all = jnp.concatenate([c_prompt, c_sample, jnp.zeros((16 - bp - bs, d), F32)], axis=0)
    mod = _ada_mod(c_all, w_ada, b_ada).reshape(depth, 16, 3, 3, d)

    xp = x_prompt.reshape(bp * t, d)
    xs = jnp.concatenate([x_sample.reshape(bs, d), jnp.zeros((srows - bs, d), F32)], axis=0)
    st_p, st_s = [], []
    for l in range(depth):
        mod_p = mod[l, :bp]
        mod_s = jnp.concatenate([mod[l, bp:bp + bs], jnp.zeros((srows - bs, 3, 3, d), F32)], axis=0)
        w_in_p = _permute_w_in(w_in[l])
        gains = _proj_gains(qk_norm_g[l])
        mixw = dict(qkg=qk_norm_g[l], cw1=nsa_cmp_w1[l], cb1=nsa_cmp_b1[l], cw2=nsa_cmp_w2[l], cpe=nsa_cmp_pos[l],
                    ln_g=gmlp_ln_g[l], ln_b=gmlp_ln_b[l], ws=gmlp_ws[l], bs=gmlp_bs[l], lb=lb_all[l],
                    out_g=hgrn_out_g[l])
        g = norm_g[l]

        xp = _ffn(xp, mod_p, 0, 0, g[0], w_ffn_gate, w_ffn_up, w_ffn_down, l, t)
        hp = _modulate(xp, g[1], mod_p[:, 1, 1], mod_p[:, 1, 0], t)
        mixed, sp = _prompt_mix(hp, bp, w_in_p, gains, mixw)
        xp = _mm_resid(mixed, w_out, (l,), xp, mod_p[:, 1, 2], 1.0, t, 1)
        xp = _ffn(xp, mod_p, 1, 2, g[2], w_ffn_gate, w_ffn_up, w_ffn_down, l, t)

        xs = _ffn(xs, mod_s, 0, 0, g[0], w_ffn_gate, w_ffn_up, w_ffn_down, l, 1)
        hs = _modulate(xs, g[1], mod_s[:, 1, 1], mod_s[:, 1, 0], 1)
        mixed, ss = _sample_mix(hs, bs, w_in_p, gains, mixw, cache_nsa_cmp_kv, cache_nsa_sel_kv, cache_moba_kv,
                                cache_nsa_win_kv, state_hgrn, page_table, l)
        xs = _mm_resid(mixed, w_out, (l,), xs, mod_s[:, 1, 2], 1.0, 1, 1)
        xs = _ffn(xs, mod_s, 1, 2, g[2], w_ffn_gate, w_ffn_up, w_ffn_down, l, 1)
        st_p.append(sp)
        st_s.append(ss)

    stack = lambda sts, i: jnp.stack([s[i] for s in sts])
    return (xp.reshape(bp, t, d), xs[:bs].reshape(bs, 1, d),
            stack(st_p, 0), stack(st_p, 1), stack(st_p, 2), stack(st_p, 3), stack(st_p, 4),
            stack(st_s, 0), stack(st_s, 1), stack(st_s, 2), stack(st_s, 3), stack(st_s, 4), stack(st_s, 5))
```

```python
import functools

import jax
import jax.numpy as jnp
from jax import lax
from jax.experimental import pallas as pl
from jax.experimental.pallas import tpu as pltpu

F32 = jnp.float32
BF16 = jnp.bfloat16

HEAD_DIM = 128
N_HEADS = 8
KV_HEADS = 2
GROUP = N_HEADS // KV_HEADS
WIDTH = N_HEADS * HEAD_DIM
PAGE_SIZE = 128
PAGES_PER_STEP = 16

NSA_CMP_STRIDE = 16
NSA_CMP_BLOCK = 32
NSA_CMP_HIDDEN = 256
NSA_SEL_BLOCK = 64
NSA_SEL_TOP = 16
NSA_N_LOCAL = 2
NSA_WINDOW = 512
GMLP_CHUNK = 128
MOBA_BLOCK = 256
MOBA_TOP = 3
HGRN_CHUNK = 64
NEG_INF = -1e30
FORCE_SCORE = 1e4
TINY = 1e-30
ATT_SCALE = HEAD_DIM ** -0.5

LANE = 128
KEY_TILE = 4096
WIN_BAND = NSA_WINDOW + 128
SAMPLE_ROWS = 8
TAIL_ROWS = 256

PROJ_TN = 256
SEG = {}
_off = 0
for _name, _w in (("qa", 1024), ("u", 1024), ("v", 1024), ("qc", 1024), ("qd", 1024), ("fd", 1024),
                  ("vd", 1024), ("gd", 1024), ("kvc", 512), ("kvs", 512), ("kvw", 512), ("kvm", 512),
                  ("ga", 256)):
    SEG[_name] = _off
    _off += _w
N_PROJ = _off

_ORIG = {}
_o = 0
for _name, _w in (("qa", 1024), ("kvc", 512), ("kvs", 512), ("kvw", 512), ("ga", 24), ("u", 1024), ("v", 1024),
                  ("qc", 1024), ("kvm", 512), ("qd", 1024), ("fd", 1024), ("vd", 1024), ("gd", 1024)):
    _ORIG[_name] = (_o, _w)
    _o += _w


def _params(sem, vmem_mb):
    return pltpu.CompilerParams(dimension_semantics=sem, vmem_limit_bytes=vmem_mb << 20)


def _sigmoid(x):
    return 1.0 / (1.0 + jnp.exp(-x))


def _silu(x):
    return x * _sigmoid(x)


def _gelu(x):
    return 0.5 * x * (1.0 + jnp.tanh(0.7978845608028654 * (x + 0.044715 * (x * x * x))))


def _dot(a, b):
    return jnp.dot(a, b, preferred_element_type=F32)


def _dot_nt(a, b):
    return lax.dot_general(a, b, (((1,), (1,)), ((), ())), preferred_element_type=F32)


def _dot_tn(a, b):
    return lax.dot_general(a, b, (((0,), (0,)), ((), ())), preferred_element_type=F32)


def _split3(a):
    hi = a.astype(BF16)
    r1 = a - hi.astype(F32)
    mid = r1.astype(BF16)
    lo = (r1 - mid.astype(F32)).astype(BF16)
    return hi, mid, lo


def _dot_f32_lhs(a, b_exact):
    hi, mid, lo = _split3(a)
    return _dot(hi, b_exact) + _dot(mid, b_exact) + _dot(lo, b_exact)


def _dot_f32_rhs(a_exact, b):
    hi, mid, lo = _split3(b)
    return _dot(a_exact, hi) + _dot(a_exact, mid) + _dot(a_exact, lo)


def _dot_nt_f32(a, b):
    ah, am, _ = _split3(a)
    bh, bm, _ = _split3(b)
    return _dot_nt(ah, bh) + _dot_nt(ah, bm) + _dot_nt(am, bh)


def _rms(x, g, eps=1e-6):
    return x * lax.rsqrt(jnp.mean(x * x, axis=-1, keepdims=True) + eps) * g


def _softmax_parts(s_list, m_list, never_empty=False):
    sm = [jnp.where(m, s, NEG_INF) for s, m in zip(s_list, m_list)]
    mx = functools.reduce(jnp.maximum, [jnp.max(s, axis=-1, keepdims=True) for s in sm])
    if never_empty:
        e = [jnp.exp(s - mx) for s in sm]
    else:
        e = [jnp.where(m, jnp.exp(s - mx), 0.0) for s, m in zip(sm, m_list)]
    den = functools.reduce(lambda a, b: a + b, [jnp.sum(x, axis=-1, keepdims=True) for x in e])
    return e, jnp.maximum(den, 1e-30)


def _rank(score, n, col):
    rank = jnp.zeros(score.shape, jnp.int32)
    for i in range(n):
        si = score[:, i:i + 1]
        before = jnp.where(si > score, 1, jnp.where((si == score) & (col > i), 1, 0))
        rank = rank + before
    return rank


def _ada_kernel(c_ref, w_ref, b_ref, o_ref):
    a = _silu(c_ref[...]).astype(BF16)
    o_ref[...] = _dot(a, w_ref[...].astype(BF16)) + b_ref[...]


def _ada_mod(c, w_ada, b_ada):
    depth, d, n = w_ada.shape
    rows = c.shape[0]
    tn = 512
    return pl.pallas_call(
        _ada_kernel,
        grid=(depth, n // tn),
        in_specs=[pl.BlockSpec((rows, d), lambda l, j: (0, 0)),
                  pl.BlockSpec((None, d, tn), lambda l, j: (l, 0, j)),
                  pl.BlockSpec((None, 1, tn), lambda l, j: (l, 0, j))],
        out_specs=pl.BlockSpec((None, rows, tn), lambda l, j: (l, 0, j)),
        out_shape=jax.ShapeDtypeStruct((depth, rows, n), F32),
        compiler_params=_params(("parallel", "parallel"), 40),
    )(c, w_ada, b_ada.reshape(depth, 1, n))


def _mod_kernel(x_ref, g_ref, sc_ref, sh_ref, o_ref):
    y = _rms(x_ref[...], g_ref[...])
    o_ref[...] = (y * (1.0 + sc_ref[...]) + sh_ref[...]).astype(o_ref.dtype)


def _modulate(x, g, scale, shift, rows_per_group):
    m, d = x.shape
    if rows_per_group == 1:
        tm = m
        vec = pl.BlockSpec((m, d), lambda i: (0, 0))
        sc, sh = scale, shift
    else:
        tm = 256
        per = rows_per_group // tm
        vec = pl.BlockSpec((None, 1, d), lambda i: (i // per, 0, 0))
        sc, sh = scale.reshape(-1, 1, d), shift.reshape(-1, 1, d)
    return pl.pallas_call(
        _mod_kernel,
        grid=(m // tm,),
        in_specs=[pl.BlockSpec((tm, d), lambda i: (i, 0)),
                  pl.BlockSpec((1, d), lambda i: (0, 0)), vec, vec],
        out_specs=pl.BlockSpec((tm, d), lambda i: (i, 0)),
        out_shape=jax.ShapeDtypeStruct((m, d), BF16),
        compiler_params=_params(("parallel",), 40),
    )(x, g.reshape(1, d), sc, sh)


def _glu_kernel(h_ref, wg_ref, wu_ref, o_ref):
    h = h_ref[...]
    a = _dot(h, wg_ref[...].astype(BF16))
    b = _dot(h, wu_ref[...].astype(BF16))
    o_ref[...] = (_silu(a) * b).astype(o_ref.dtype)


def _glu(h, wg, wu, layer, sub):
    m, d = h.shape
    f = wg.shape[-1]
    tm = min(m, 1024)
    tn = 256
    wspec = pl.BlockSpec((None, None, d, tn), lambda i, j: (layer, sub, 0, j))
    return pl.pallas_call(
        _glu_kernel,
        grid=(m // tm, f // tn),
        in_specs=[pl.BlockSpec((tm, d), lambda i, j: (i, 0)), wspec, wspec],
        out_specs=pl.BlockSpec((tm, tn), lambda i, j: (i, j)),
        out_shape=jax.ShapeDtypeStruct((m, f), BF16),
        compiler_params=_params(("parallel", "arbitrary"), 48),
    )(h, wg, wu)


def _mm_resid_kernel(x_ref, w_ref, r_ref, g_ref, o_ref, *, coef):
    acc = _dot(x_ref[...], w_ref[...].astype(BF16))
    o_ref[...] = r_ref[...] + (coef * g_ref[...]) * acc


def _mm_resid_parts_kernel(*refs, coef, n_parts):
    x_refs = refs[:n_parts]
    w_ref, r_ref, g_ref, o_ref, lhs_ref = refs[n_parts:]

    @pl.when(pl.program_id(1) == 0)
    def _():
        off = 0
        for x_ref in x_refs:
            lhs_ref[:, off:off + x_ref.shape[1]] = x_ref[...]
            off += x_ref.shape[1]

    acc = _dot(lhs_ref[...], w_ref[...].astype(BF16))
    o_ref[...] = r_ref[...] + (coef * g_ref[...]) * acc


def _mm_resid_parts(xs, w, w_lead, resid, gate, coef, rows_per_group):
    m = xs[0].shape[0]
    kdim = sum(x.shape[1] for x in xs)
    n = w.shape[-1]
    tm, tn = 1024, 256
    per = rows_per_group // tm
    nlead = len(w_lead)
    in_specs = [pl.BlockSpec((tm, x.shape[1]), lambda i, j: (i, 0)) for x in xs]
    in_specs += [pl.BlockSpec((None,) * nlead + (kdim, tn), lambda i, j: tuple(w_lead) + (0, j)),
                 pl.BlockSpec((tm, tn), lambda i, j: (i, j)),
                 pl.BlockSpec((None, 1, tn), lambda i, j: (i // per, 0, j))]
    return pl.pallas_call(
        functools.partial(_mm_resid_parts_kernel, coef=coef, n_parts=len(xs)),
        grid=(m // tm, n // tn),
        in_specs=in_specs,
        out_specs=pl.BlockSpec((tm, tn), lambda i, j: (i, j)),
        out_shape=jax.ShapeDtypeStruct((m, n), F32),
        scratch_shapes=[pltpu.VMEM((tm, kdim), BF16)],
        compiler_params=_params(("parallel", "arbitrary"), 48),
    )(*xs, w, resid, gate.reshape(-1, 1, n))


def _mm_resid(x, w, w_lead, resid, gate, coef, rows_per_group, k_splits):
    m, kdim = x.shape
    n = w.shape[-1]
    tk = kdim // k_splits
    tm = min(m, 1024)
    tn = 256
    nlead = len(w_lead)
    if rows_per_group == 1:
        gspec = pl.BlockSpec((m, tn), lambda i, j: (0, j))
        g = gate
    else:
        per = rows_per_group // tm
        gspec = pl.BlockSpec((None, 1, tn), lambda i, j: (i // per, 0, j))
        g = gate.reshape(-1, 1, n)
    out = resid
    for ks in range(k_splits):
        out = pl.pallas_call(
            functools.partial(_mm_resid_kernel, coef=coef),
            grid=(m // tm, n // tn),
            in_specs=[pl.BlockSpec((tm, tk), lambda i, j, ks=ks: (i, ks)),
                      pl.BlockSpec((None,) * nlead + (tk, tn), lambda i, j, ks=ks: tuple(w_lead) + (ks, j)),
                      pl.BlockSpec((tm, tn), lambda i, j: (i, j)),
                      gspec],
            out_specs=pl.BlockSpec((tm, tn), lambda i, j: (i, j)),
            out_shape=jax.ShapeDtypeStruct((m, n), F32),
            compiler_params=_params(("parallel", "arbitrary"), 48),
        )(x, w, out, g)
    return out


def _tiles(name, width):
    a = SEG[name] // PROJ_TN
    return a, a + width // PROJ_TN


_NORM_TILES = (_tiles("qa", 1024), _tiles("qc", 1024), _tiles("kvs", 256), _tiles("kvw", 256), _tiles("kvm", 256))
_GELU_TILES = (_tiles("u", 2048),)
_SILU_TILES = (_tiles("qd", 1024),)
_SIGM_TILES = (_tiles("ga", 256),)


def _in_ranges(j, ranges):
    return functools.reduce(jnp.logical_or, [(j >= a) & (j < b) for a, b in ranges])


def _proj_kernel(h_ref, w_ref, g_ref, o_ref):
    j = pl.program_id(1)
    acc = _dot(h_ref[...], w_ref[...])
    is_norm = _in_ranges(j, _NORM_TILES)
    is_gelu = _in_ranges(j, _GELU_TILES)
    is_silu = _in_ranges(j, _SILU_TILES)
    is_sigm = _in_ranges(j, _SIGM_TILES)

    @pl.when(is_norm)
    def _():
        g = g_ref[...]
        for hh in range(PROJ_TN // HEAD_DIM):
            sl = slice(hh * HEAD_DIM, (hh + 1) * HEAD_DIM)
            o_ref[:, sl] = _rms(acc[:, sl], g[:, sl])

    @pl.when(is_gelu)
    def _():
        o_ref[...] = _gelu(acc)

    @pl.when(is_silu)
    def _():
        o_ref[...] = _silu(acc) * ATT_SCALE

    @pl.when(is_sigm)
    def _():
        o_ref[...] = _sigmoid(acc)

    @pl.when(jnp.logical_not(is_norm | is_gelu | is_silu | is_sigm))
    def _():
        o_ref[...] = acc


def _proj(h, w_in_p, gains):
    m, d = h.shape
    tm = min(m, 1024)
    nt = N_PROJ // PROJ_TN
    return pl.pallas_call(
        _proj_kernel,
        grid=(m // tm, nt),
        in_specs=[pl.BlockSpec((tm, d), lambda i, j: (i, 0)),
                  pl.BlockSpec((d, PROJ_TN), lambda i, j: (0, j)),
                  pl.BlockSpec((None, 1, PROJ_TN), lambda i, j: (j, 0, 0))],
        out_specs=pl.BlockSpec((tm, PROJ_TN), lambda i, j: (i, j)),
        out_shape=jax.ShapeDtypeStruct((m, N_PROJ), F32),
        compiler_params=_params(("parallel", "arbitrary"), 40),
    )(h, w_in_p, gains)


def _permute_w_in(w_in_l):
    d = w_in_l.shape[0]

    def cols(name):
        a, w = _ORIG[name]
        return w_in_l[:, a:a + w]

    ga = cols("ga")
    pad = jnp.zeros((d, LANE - 12), w_in_l.dtype)
    parts = [cols(n) for n in ("qa", "u", "v", "qc", "qd", "fd", "vd", "gd", "kvc", "kvs", "kvw", "kvm")]
    parts += [ga[:, :12], pad, ga[:, 12:], pad]
    return jnp.concatenate(parts, axis=1).astype(BF16)


def _proj_gains(qkg):
    g = jnp.ones((N_PROJ // PROJ_TN, PROJ_TN), F32)
    two = lambda v: jnp.tile(v, PROJ_TN // HEAD_DIM)
    for name, idx, ntile in (("qa", 0, 4), ("qc", 4, 4), ("kvs", 2, 1), ("kvw", 3, 1), ("kvm", 5, 1)):
        t0 = SEG[name] // PROJ_TN
        g = g.at[t0:t0 + ntile].set(two(qkg[idx])[None, :])
    return g.reshape(-1, 1, PROJ_TN)


CMP_ROWS = 128


def _compress_body(x_of, w1_ref, b1_ref, w2_ref, pe_ref, gk_ref, o_ref, n_cmp):
    i = pl.program_id(1)
    row = lax.broadcasted_iota(jnp.int32, (CMP_ROWS, 1), 0)
    valid = (i * CMP_ROWS + row) < n_cmp
    half = NSA_CMP_STRIDE * HEAD_DIM
    for s in range(2):
        w1a = w1_ref[s, 0].astype(BF16)
        w1b = w1_ref[s, 1].astype(BF16)
        pe = pe_ref[s].astype(BF16)
        bias = (_dot(pe[:, :half], w1a) + _dot(pe[:, half:], w1b))[0:1] + b1_ref[s]
        w2 = w2_ref[s].astype(BF16)
        for k in range(KV_HEADS):
            x, xn = x_of(s * KV_HEADS + k)
            x = x.astype(BF16)
            h1 = _dot(x, w1a)
            h2 = _dot(x, w1b)
            h2n = _dot(xn.astype(BF16), w1b)
            h2s = jnp.where(row == CMP_ROWS - 1, h2n[0:1], pltpu.roll(h2, CMP_ROWS - 1, 0))
            hid = _gelu(h1 + h2s + bias)
            out = _dot(hid.astype(BF16), w2)
            if s == 0:
                out = _rms(out, gk_ref[...])
            o_ref[s, k] = jnp.where(valid, out, 0.0)


def _compress_kernel(r_ref, rn_ref, w1_ref, b1_ref, w2_ref, pe_ref, gk_ref, o_ref, *, n_cmp):
    row_w = 2 * KV_HEADS * HEAD_DIM

    def x_of(c):
        cols = [slice(p * row_w + c * HEAD_DIM, p * row_w + (c + 1) * HEAD_DIM) for p in range(NSA_CMP_STRIDE)]
        return (jnp.concatenate([r_ref[:, sl] for sl in cols], axis=1),
                jnp.concatenate([rn_ref[:, sl] for sl in cols], axis=1))

    _compress_body(x_of, w1_ref, b1_ref, w2_ref, pe_ref, gk_ref, o_ref, n_cmp)


PAGE_HALVES = PAGE_SIZE // NSA_CMP_STRIDE
PAGE_STREAMS = 2 * KV_HEADS


def _compress_paged_kernel(pt_ref, *refs, n_cmp):
    pages = refs[:PAGES_PER_STEP]
    nxt = refs[PAGES_PER_STEP]
    w1_ref, b1_ref, w2_ref, pe_ref, gk_ref, o_ref = refs[PAGES_PER_STEP + 1:]

    def flat(pg, c):
        return jnp.concatenate([pg[pl.ds(p * PAGE_STREAMS + c, PAGE_HALVES, stride=NSA_CMP_STRIDE * PAGE_STREAMS), :]
                                for p in range(NSA_CMP_STRIDE)], axis=1)

    def x_of(c):
        return jnp.concatenate([flat(pg, c) for pg in pages], axis=0), flat(nxt, c)

    _compress_body(x_of, w1_ref, b1_ref, w2_ref, pe_ref, gk_ref, o_ref, n_cmp)


def _compress_weights(w1, b1, w2, pe, gk):
    half = NSA_CMP_STRIDE * HEAD_DIM
    w1r = w1.reshape(2, 2, half, NSA_CMP_HIDDEN)
    pe8 = jnp.broadcast_to(pe.reshape(2, 1, 2 * half), (2, 8, 2 * half))
    args = (w1r, b1.reshape(2, 1, NSA_CMP_HIDDEN), w2, pe8, gk.reshape(1, HEAD_DIM))
    return args, [a.shape for a in args]


def _compress(kv_flat, batch, w1, b1, w2, pe, gk):
    n_half = kv_flat.shape[0] // batch // NSA_CMP_STRIDE
    nblk = n_half // CMP_ROWS
    width = NSA_CMP_STRIDE * kv_flat.shape[1]
    r = kv_flat.reshape(batch * n_half, width)
    last8 = batch * n_half // 8 - 1
    wargs, wshapes = _compress_weights(w1, b1, w2, pe, gk)
    full = lambda shape: pl.BlockSpec(shape, lambda b, i: (0,) * len(shape))
    return pl.pallas_call(
        functools.partial(_compress_kernel, n_cmp=n_half - 1),
        grid=(batch, nblk),
        in_specs=[pl.BlockSpec((CMP_ROWS, width), lambda b, i: (b * nblk + i, 0)),
                  pl.BlockSpec((8, width), lambda b, i: (jnp.minimum((b * nblk + i + 1) * (CMP_ROWS // 8), last8), 0))]
                 + [full(s) for s in wshapes],
        out_specs=pl.BlockSpec((None, 2, KV_HEADS, CMP_ROWS, HEAD_DIM), lambda b, i: (b, 0, 0, i, 0)),
        out_shape=jax.ShapeDtypeStruct((batch, 2, KV_HEADS, n_half, HEAD_DIM), F32),
        compiler_params=_params(("parallel", "arbitrary"), 48),
    )(r, r, *wargs)


def _page_view(pool):
    depth, n_phys = pool.shape[:2]
    return pool.reshape(depth, n_phys, PAGE_SIZE * PAGE_STREAMS, HEAD_DIM)


def _compress_paged(pool, layer, page_table, w1, b1, w2, pe, gk):
    view = _page_view(pool)
    batch, n_pages = page_table.shape
    groups = n_pages // PAGES_PER_STEP
    n_half = n_pages * PAGE_HALVES
    wargs, wshapes = _compress_weights(w1, b1, w2, pe, gk)
    page = lambda fn: pl.BlockSpec((None, None, PAGE_SIZE * PAGE_STREAMS, HEAD_DIM), fn)
    in_specs = [page(lambda b, g, pt, r=r: (layer, pt[b, g * PAGES_PER_STEP + r], 0, 0)) for r in range(PAGES_PER_STEP)]
    in_specs.append(page(lambda b, g, pt: (layer, pt[b, jnp.minimum((g + 1) * PAGES_PER_STEP, n_pages - 1)], 0, 0)))
    in_specs += [pl.BlockSpec(s, lambda b, g, pt, n=len(s): (0,) * n) for s in wshapes]
    return pl.pallas_call(
        functools.partial(_compress_paged_kernel, n_cmp=n_half - 1),
        grid_spec=pltpu.PrefetchScalarGridSpec(
            num_scalar_prefetch=1, grid=(batch, groups), in_specs=in_specs,
            out_specs=pl.BlockSpec((None, 2, KV_HEADS, CMP_ROWS, HEAD_DIM), lambda b, g, pt: (b, 0, 0, g, 0))),
        out_shape=jax.ShapeDtypeStruct((batch, 2, KV_HEADS, n_half, HEAD_DIM), F32),
        compiler_params=_params(("parallel", "arbitrary"), 48),
    )(page_table, *([view] * (PAGES_PER_STEP + 1)), *wargs)


def _stack_heads(q):
    return jnp.concatenate([q[:, g * HEAD_DIM:(g + 1) * HEAD_DIM] for g in range(GROUP)], axis=0)


def _block_onehot(n_blocks, kpos, shift):
    blk = lax.broadcasted_iota(jnp.int32, (n_blocks, 1), 0)
    return jnp.where((kpos >> shift) == blk, 1.0, 0.0).astype(BF16)


def _nsa_kernel(*refs, tq, qpos0, n_sel, extents, has_tail, win_pos0, win_len):
    if len(extents) > 1:
        refs, osel_ref = refs[:-1], refs[-1]
    if has_tail:
        q_ref, ga_ref, kc_ref, vc_ref, ks_ref, vs_ref, kt_ref, vt_ref, kw_ref, vw_ref, o_ref = refs
    else:
        q_ref, ga_ref, kc_ref, vc_ref, ks_ref, vs_ref, kw_ref, vw_ref, o_ref = refs
    q0 = qpos0 + pl.program_id(2) * tq
    rows = GROUP * tq
    q4 = _stack_heads(q_ref[...]).astype(BF16)
    qpos = q0 + (lax.broadcasted_iota(jnp.int32, (rows, 1), 0) & (tq - 1))

    kc = kc_ref[...].astype(BF16)
    vc = vc_ref[...].astype(BF16)
    n_cp = kc.shape[0]
    cidx = lax.broadcasted_iota(jnp.int32, (1, n_cp), 1)
    s_cmp = _dot_nt(q4, kc) * ATT_SCALE
    (e_cmp,), den = _softmax_parts([s_cmp], [(cidx * NSA_CMP_STRIDE + (NSA_CMP_BLOCK - 1)) <= qpos])
    p_cmp = e_cmp / den
    o_cmp = _dot(p_cmp.astype(BF16), vc)

    n_sp = -(-n_sel // LANE) * LANE
    psum = functools.reduce(lambda a, b: a + b, [p_cmp[g * tq:(g + 1) * tq] for g in range(GROUP)])
    ci = lax.broadcasted_iota(jnp.int32, (n_cp, 1), 0) * NSA_CMP_STRIDE
    sj = lax.broadcasted_iota(jnp.int32, (1, n_sp), 1) * NSA_SEL_BLOCK
    cover = jnp.where((ci <= sj + (NSA_SEL_BLOCK - 1)) & (ci + (NSA_CMP_BLOCK - 1) >= sj), 1.0, 0.0).astype(BF16)
    imp = _dot_f32_lhs(psum, cover)
    col = lax.broadcasted_iota(jnp.int32, (1, n_sp), 1)
    qpos_t = q0 + lax.broadcasted_iota(jnp.int32, (tq, 1), 0)
    back = (qpos_t >> 6) - col
    valid = back >= 0
    forced = (col == 0) | (valid & (back < NSA_N_LOCAL))
    score = jnp.where(valid, jnp.where(forced, FORCE_SCORE, imp), NEG_INF)
    rank = _rank(score, n_sel, col)
    sel = jnp.where((rank < NSA_SEL_TOP) & (score > 0.5 * NEG_INF), 1.0, 0.0).astype(BF16)
    sel4 = jnp.concatenate([sel] * GROUP, axis=0)

    def sel_attend(n_keys):
        s_list, m_list, v_list = [], [], []

        def add_tile(k_bf, v_bf, kpos):
            picked = _dot(sel4, _block_onehot(n_sp, kpos, 6))
            s_list.append(_dot_nt(q4, k_bf) * ATT_SCALE)
            m_list.append(jnp.where(kpos <= qpos, picked, 0.0) > 0.5)
            v_list.append(v_bf)

        kt = min(n_keys, KEY_TILE)
        for t0 in range(0, n_keys, kt):
            add_tile(ks_ref[t0:t0 + kt, :].astype(BF16), vs_ref[t0:t0 + kt, :].astype(BF16),
                     t0 + lax.broadcasted_iota(jnp.int32, (1, kt), 1))
        if has_tail:
            add_tile(kt_ref[...].astype(BF16), vt_ref[...].astype(BF16),
                     n_keys + lax.broadcasted_iota(jnp.int32, (1, TAIL_ROWS), 1))
        e_list, den = _softmax_parts(s_list, m_list, never_empty=True)
        return functools.reduce(lambda a, b: a + b, [_dot(e.astype(BF16), v) for e, v in zip(e_list, v_list)]) / den

    if len(extents) == 1:
        o_sel = sel_attend(extents[0])
    else:
        need = (q0 - qpos0 + tq - 1) // extents[0]
        for idx, ext in enumerate(extents):
            @pl.when(need == idx)
            def _(ext=ext):
                osel_ref[...] = sel_attend(ext)
        o_sel = osel_ref[...]

    start = jnp.clip(q0 - NSA_WINDOW - win_pos0, 0, win_len - WIN_BAND)
    start = pl.multiple_of(start, LANE)
    kw = kw_ref[pl.ds(start, WIN_BAND), :].astype(BF16)
    vw = vw_ref[pl.ds(start, WIN_BAND), :].astype(BF16)
    kposw = win_pos0 + start + lax.broadcasted_iota(jnp.int32, (1, WIN_BAND), 1)
    s_win = _dot_nt(q4, kw) * ATT_SCALE
    m_win = jnp.where(kposw <= qpos, qpos - kposw, NSA_WINDOW) < NSA_WINDOW
    (e_win,), den = _softmax_parts([s_win], [m_win], never_empty=True)
    o_win = _dot(e_win.astype(BF16), vw) / den

    ga = ga_ref[...]
    outs = []
    for g in range(GROUP):
        r = slice(g * tq, (g + 1) * tq)
        outs.append(ga[:, 3 * g:3 * g + 1] * o_cmp[r] + ga[:, 3 * g + 1:3 * g + 2] * o_sel[r]
                    + ga[:, 3 * g + 2:3 * g + 3] * o_win[r])
    o_ref[...] = jnp.concatenate(outs, axis=1).astype(o_ref.dtype)


CAUSAL_STEP = 512


def _causal_extents(tk, tq):
    if tk % CAUSAL_STEP or CAUSAL_STEP % tq:
        return (tk,)
    return tuple(range(CAUSAL_STEP, tk + 1, CAUSAL_STEP))


def _nsa(zq, tq_total, tq, qpos0, cmp_kv, sel_main, sel_cols, sel_tail, win, win_cols, win_pos0, n_total,
         causal_skip):
    batch = cmp_kv.shape[0]
    nq = tq_total // tq
    n_cp = cmp_kv.shape[3]
    tk = sel_main.shape[0] // batch
    lw = win.shape[0] // batch
    n_sel = -(-n_total // NSA_SEL_BLOCK)
    extents = _causal_extents(tk, tq) if causal_skip else (tk,)
    scratch = [pltpu.VMEM((GROUP * tq, HEAD_DIM), F32)] if len(extents) > 1 else []
    qa_blk = SEG["qa"] // (GROUP * HEAD_DIM)
    ga_blk = SEG["ga"] // LANE
    in_specs = [
        pl.BlockSpec((tq, GROUP * HEAD_DIM), lambda b, k, i: (b * nq + i, qa_blk + k)),
        pl.BlockSpec((tq, LANE), lambda b, k, i: (b * nq + i, ga_blk + k)),
        pl.BlockSpec((None, None, None, n_cp, HEAD_DIM), lambda b, k, i: (b, 0, k, 0, 0)),
        pl.BlockSpec((None, None, None, n_cp, HEAD_DIM), lambda b, k, i: (b, 1, k, 0, 0)),
        pl.BlockSpec((tk, HEAD_DIM), lambda b, k, i: (b, sel_cols[0] + k)),
        pl.BlockSpec((tk, HEAD_DIM), lambda b, k, i: (b, sel_cols[1] + k)),
    ]
    args = [zq, zq, cmp_kv, cmp_kv, sel_main, sel_main]
    if sel_tail is not None:
        in_specs += [pl.BlockSpec((TAIL_ROWS, HEAD_DIM), lambda b, k, i: (b, k)),
                     pl.BlockSpec((TAIL_ROWS, HEAD_DIM), lambda b, k, i: (b, KV_HEADS + k))]
        args += [sel_tail, sel_tail]
    in_specs += [pl.BlockSpec((lw, HEAD_DIM), lambda b, k, i: (b, win_cols[0] + k)),
                 pl.BlockSpec((lw, HEAD_DIM), lambda b, k, i: (b, win_cols[1] + k))]
    args += [win, win]
    return pl.pallas_call(
        functools.partial(_nsa_kernel, tq=tq, qpos0=qpos0, n_sel=n_sel, extents=extents,
                          has_tail=sel_tail is not None, win_pos0=win_pos0, win_len=lw),
        grid=(batch, KV_HEADS, nq),
        in_specs=in_specs,
        out_specs=pl.BlockSpec((tq, GROUP * HEAD_DIM), lambda b, k, i: (b * nq + i, k)),
        out_shape=jax.ShapeDtypeStruct((batch * tq_total, WIDTH), BF16),
        scratch_shapes=scratch,
        compiler_params=_params(("parallel", "parallel", "arbitrary"), 56),
    )(*args)


def _moba_kernel(*refs, tq, qpos0, extents, has_tail):
    if has_tail:
        q_ref, k_ref, v_ref, kt_ref, vt_ref, o_ref = refs
    else:
        q_ref, k_ref, v_ref, o_ref = refs
    q0 = qpos0 + pl.program_id(2) * tq
    rows = GROUP * tq
    q4f = _stack_heads(q_ref[...])
    q4 = q4f.astype(BF16)
    qpos = q0 + (lax.broadcasted_iota(jnp.int32, (rows, 1), 0) & (tq - 1))
    cur = qpos >> 8
    col = lax.broadcasted_iota(jnp.int32, (1, LANE), 1)

    def attend(n_keys):
        kt = min(n_keys, KEY_TILE)
        n_blk = n_keys // MOBA_BLOCK
        means = [jnp.sum(k_ref[t0:t0 + kt, :].reshape(kt // MOBA_BLOCK, MOBA_BLOCK, HEAD_DIM), axis=1)
                 * (1.0 / MOBA_BLOCK) for t0 in range(0, n_keys, kt)]
        if n_blk < LANE:
            means.append(jnp.zeros((LANE - n_blk, HEAD_DIM), F32))
        kmean = jnp.concatenate(means, axis=0)
        gate = jnp.where(col < cur, _dot_nt_f32(q4f, kmean), NEG_INF)
        rank = _rank(gate, n_blk, col)
        sel = jnp.where((rank < MOBA_TOP) & (gate > 0.5 * NEG_INF), 1.0, 0.0).astype(BF16)

        s_list, m_list, v_list = [], [], []

        def add_tile(k_bf, v_bf, kpos):
            picked = _dot(sel, _block_onehot(LANE, kpos, 8))
            own = jnp.where((kpos >> 8) == cur, jnp.where(kpos <= qpos, 1.0, 0.0), 0.0)
            s_list.append(_dot_nt(q4, k_bf) * ATT_SCALE)
            m_list.append((picked + own) > 0.5)
            v_list.append(v_bf)

        for t0 in range(0, n_keys, kt):
            add_tile(k_ref[t0:t0 + kt, :].astype(BF16), v_ref[t0:t0 + kt, :].astype(BF16),
                     t0 + lax.broadcasted_iota(jnp.int32, (1, kt), 1))
        if has_tail:
            add_tile(kt_ref[...].astype(BF16), vt_ref[...].astype(BF16),
                     n_keys + lax.broadcasted_iota(jnp.int32, (1, TAIL_ROWS), 1))
        e_list, den = _softmax_parts(s_list, m_list, never_empty=True)
        o = functools.reduce(lambda a, b: a + b, [_dot(e.astype(BF16), v) for e, v in zip(e_list, v_list)]) / den
        o_ref[...] = jnp.concatenate([o[g * tq:(g + 1) * tq] for g in range(GROUP)], axis=1).astype(o_ref.dtype)

    if len(extents) == 1:
        attend(extents[0])
    else:
        need = (q0 - qpos0 + tq - 1) // extents[0]
        for idx, ext in enumerate(extents):
            @pl.when(need == idx)
            def _(ext=ext):
                attend(ext)


def _moba(zq, tq_total, tq, qpos0, batch, main, cols, tail, causal_skip):
    nq = tq_total // tq
    tk = main.shape[0] // batch
    extents = _causal_extents(tk, tq) if causal_skip else (tk,)
    qc_blk = SEG["qc"] // (GROUP * HEAD_DIM)
    in_specs = [pl.BlockSpec((tq, GROUP * HEAD_DIM), lambda b, k, i: (b * nq + i, qc_blk + k)),
                pl.BlockSpec((tk, HEAD_DIM), lambda b, k, i: (b, cols[0] + k)),
                pl.BlockSpec((tk, HEAD_DIM), lambda b, k, i: (b, cols[1] + k))]
    args = [zq, main, main]
    if tail is not None:
        in_specs += [pl.BlockSpec((TAIL_ROWS, HEAD_DIM), lambda b, k, i: (b, k)),
                     pl.BlockSpec((TAIL_ROWS, HEAD_DIM), lambda b, k, i: (b, KV_HEADS + k))]
        args += [tail, tail]
    return pl.pallas_call(
        functools.partial(_moba_kernel, tq=tq, qpos0=qpos0, extents=extents, has_tail=tail is not None),
        grid=(batch, KV_HEADS, nq),
        in_specs=in_specs,
        out_specs=pl.BlockSpec((tq, GROUP * HEAD_DIM), lambda b, k, i: (b * nq + i, k)),
        out_shape=jax.ShapeDtypeStruct((batch * tq_total, WIDTH), BF16),
        compiler_params=_params(("parallel", "parallel", "arbitrary"), 56),
    )(*args)


def _layer_norm(v, g, b, eps=1e-5):
    mu = jnp.mean(v, axis=-1, keepdims=True)
    var = jnp.mean(jnp.square(v - mu), axis=-1, keepdims=True)
    return (v - mu) * lax.rsqrt(var + eps) * g + b


def _gmlp_kernel(u_ref, v_ref, ws_ref, bst_ref, g_ref, b_ref, o_ref):
    vn = _layer_norm(v_ref[...], g_ref[...], b_ref[...]).astype(BF16)
    r = lax.broadcasted_iota(jnp.int32, (GMLP_CHUNK, GMLP_CHUNK), 0)
    c = lax.broadcasted_iota(jnp.int32, (GMLP_CHUNK, GMLP_CHUNK), 1)
    bst = bst_ref[...]
    for g in range(N_HEADS):
        sl = slice(g * HEAD_DIM, (g + 1) * HEAD_DIM)
        wm = jnp.where(c <= r, ws_ref[g], 0.0).astype(BF16)
        sv = _dot(wm, vn[:, sl]) + bst[:, g:g + 1]
        o_ref[:, sl] = (u_ref[:, sl] * sv).astype(o_ref.dtype)


def _gmlp(z, ws, bs, ln_g, ln_b):
    m = z.shape[0]
    ub, vb = SEG["u"] // WIDTH, SEG["v"] // WIDTH
    full = lambda shape: pl.BlockSpec(shape, lambda i: (0,) * len(shape))
    return pl.pallas_call(
        _gmlp_kernel,
        grid=(m // GMLP_CHUNK,),
        in_specs=[pl.BlockSpec((GMLP_CHUNK, WIDTH), lambda i: (i, ub)),
                  pl.BlockSpec((GMLP_CHUNK, WIDTH), lambda i: (i, vb)),
                  full(ws.shape), full((GMLP_CHUNK, N_HEADS)), full((1, WIDTH)), full((1, WIDTH))],
        out_specs=pl.BlockSpec((GMLP_CHUNK, WIDTH), lambda i: (i, 0)),
        out_shape=jax.ShapeDtypeStruct((m, WIDTH), BF16),
        compiler_params=_params(("parallel",), 32),
    )(z, z, ws, bs.T, ln_g.reshape(1, WIDTH), ln_b.reshape(1, WIDTH))


def _hgrn_gates(fd, lb):
    f_gate = lb + (1.0 - lb) * _sigmoid(fd)
    return jnp.log(jnp.maximum(f_gate, TINY)), (1.0 - lb) * _sigmoid(-fd)


def _hgrn_readout(o, gd, g_out):
    return _rms(o, g_out) * _silu(gd)


HGRN_HEADS_PER_STEP = 2


def _hgrn_kernel(q_ref, f_ref, v_ref, gd_ref, lb_ref, go_ref, o_ref, s_ref):
    c = HGRN_CHUNK
    t_total = q_ref.shape[0]
    g_out = go_ref[...]
    ri = lax.broadcasted_iota(jnp.int32, (c, c), 0)
    ci = lax.broadcasted_iota(jnp.int32, (c, c), 1)
    tri = jnp.where(ci <= ri, 1.0, 0.0).astype(BF16)
    lane = lax.broadcasted_iota(jnp.int32, (1, c), 1)
    diag_mask = (ci <= ri) & ((ci >> 3) == (ri >> 3))

    def head_chunk(r0, hh, st):
        hs = slice(hh * HEAD_DIM, (hh + 1) * HEAD_DIM)
        q = q_ref[pl.ds(r0, c), hs]
        v = v_ref[pl.ds(r0, c), hs]
        lf, k = _hgrn_gates(f_ref[pl.ds(r0, c), hs], lb_ref[:, hs])
        cum = _dot_f32_rhs(tri, lf)
        a = jnp.zeros((c, c), F32)
        for bs in (32, 16, 8):
            nb = c // bs
            refq = jnp.concatenate(
                [jnp.zeros((bs, HEAD_DIM), F32)]
                + [jnp.broadcast_to(cum[b * bs - 1:b * bs], (bs, HEAD_DIM)) for b in range(1, nb)], axis=0)
            refk = jnp.concatenate(
                [jnp.broadcast_to(cum[(b + 1) * bs - 1:(b + 1) * bs], (bs, HEAD_DIM)) for b in range(nb)], axis=0)
            qt = (q * jnp.exp(jnp.minimum(cum - refq, 0.0))).astype(BF16)
            kt = (k * jnp.exp(jnp.minimum(refk - cum, 0.0))).astype(BF16)
            sh = bs.bit_length() - 1
            lvl = (((ri >> sh) & 1) == 1) & ((ci >> sh) == (ri >> sh) - 1)
            a = a + jnp.where(lvl, _dot_nt(qt, kt), 0.0)
        rows = []
        for blk in range(c // 8):
            b0 = blk * 8
            qb, kb, cb = q[b0:b0 + 8], k[b0:b0 + 8], cum[b0:b0 + 8]
            acc = jnp.zeros((8, c), F32)
            for s in range(8):
                w = qb * kb[s:s + 1] * jnp.exp(jnp.minimum(cb - cb[s:s + 1], 0.0))
                acc = jnp.where(lane == b0 + s, jnp.sum(w, axis=1, keepdims=True), acc)
            rows.append(acc)
        a = a + jnp.where(diag_mask, jnp.concatenate(rows, axis=0), 0.0)
        o = _dot(a.astype(BF16), v.astype(BF16)) + _dot_nt((q * jnp.exp(cum)).astype(BF16), st.astype(BF16))
        o_ref[pl.ds(r0, c), hs] = _hgrn_readout(o, gd_ref[pl.ds(r0, c), hs], g_out).astype(o_ref.dtype)
        last = cum[c - 1:c]
        kk = (k * jnp.exp(last - cum)).astype(BF16)
        return st * jnp.exp(last) + _dot_tn(v.astype(BF16), kk)

    def chunk(n, sts):
        r0 = pl.multiple_of(n * c, c)
        return tuple(head_chunk(r0, hh, st) for hh, st in enumerate(sts))

    zero = jnp.zeros((HEAD_DIM, HEAD_DIM), F32)
    sts = lax.fori_loop(0, t_total // c, chunk, (zero,) * HGRN_HEADS_PER_STEP, unroll=2)
    for hh, st in enumerate(sts):
        s_ref[hh] = st.T


def _hgrn(z, batch, lb, g_out):
    t = z.shape[0] // batch
    hw = HGRN_HEADS_PER_STEP * HEAD_DIM
    blk = lambda name: SEG[name] // hw
    col = lambda name: pl.BlockSpec((t, hw), lambda b, h: (b, blk(name) + h))
    return pl.pallas_call(
        _hgrn_kernel,
        grid=(batch, N_HEADS // HGRN_HEADS_PER_STEP),
        in_specs=[col("qd"), col("fd"), col("vd"), col("gd"),
                  pl.BlockSpec((1, hw), lambda b, h: (0, h)),
                  pl.BlockSpec((1, HEAD_DIM), lambda b, h: (0, 0))],
        out_specs=[pl.BlockSpec((t, hw), lambda b, h: (b, h)),
                   pl.BlockSpec((None, HGRN_HEADS_PER_STEP, HEAD_DIM, HEAD_DIM), lambda b, h: (b, h, 0, 0))],
        out_shape=[jax.ShapeDtypeStruct((batch * t, WIDTH), BF16),
                   jax.ShapeDtypeStruct((batch, N_HEADS, HEAD_DIM, HEAD_DIM), F32)],
        compiler_params=_params(("parallel", "parallel"), 32),
    )(z, z, z, z, lb.reshape(1, WIDTH), g_out.reshape(1, HEAD_DIM))


def _gmlp_step_kernel(u_ref, v_ref, w_ref, b_ref, g_ref, bb_ref, o_ref, vn_ref):
    vn = _layer_norm(v_ref[...], g_ref[...], bb_ref[...])
    vn_ref[...] = vn
    o_ref[...] = (u_ref[...] * (w_ref[...] * vn + b_ref[...])).astype(o_ref.dtype)


def _gmlp_step(u, v, ws, bs, ln_g, ln_b):
    rows = u.shape[0]
    w_row = jnp.repeat(ws[:, 0, 0], HEAD_DIM).reshape(1, WIDTH)
    b_row = jnp.repeat(bs[:, 0], HEAD_DIM).reshape(1, WIDTH)
    return pl.pallas_call(
        _gmlp_step_kernel,
        out_shape=[jax.ShapeDtypeStruct((rows, WIDTH), BF16), jax.ShapeDtypeStruct((rows, WIDTH), F32)],
    )(u, v, w_row, b_row, ln_g.reshape(1, WIDTH), ln_b.reshape(1, WIDTH))


def _hgrn_step_kernel(qc_ref, fc_ref, lbc_ref, v_ref, gd_ref, go_ref, s0_ref, o_ref, s_ref):
    lf, k = _hgrn_gates(fc_ref[...], lbc_ref[...])
    s_new = jnp.exp(lf) * s0_ref[...] + k * v_ref[...]
    s_ref[...] = s_new
    o = jnp.sum(qc_ref[...] * s_new, axis=0, keepdims=True)
    o_ref[...] = _hgrn_readout(o, gd_ref[...], go_ref[...]).astype(o_ref.dtype)


def _hgrn_step(qd, fd, vd, gd, lb, g_out, state, layer):
    batch = qd.shape[0]
    colv = lambda a: a.reshape(batch, N_HEADS, HEAD_DIM, 1)
    rowv = lambda a: a.reshape(batch, N_HEADS, 1, HEAD_DIM)
    cspec = pl.BlockSpec((None, None, HEAD_DIM, 1), lambda b, h: (b, h, 0, 0))
    rspec = pl.BlockSpec((None, None, 1, HEAD_DIM), lambda b, h: (b, h, 0, 0))
    return pl.pallas_call(
        _hgrn_step_kernel,
        grid=(batch, N_HEADS),
        in_specs=[cspec, cspec, pl.BlockSpec((None, HEAD_DIM, 1), lambda b, h: (h, 0, 0)), rspec, rspec,
                  pl.BlockSpec((1, HEAD_DIM), lambda b, h: (0, 0)),
                  pl.BlockSpec((None, None, None, HEAD_DIM, HEAD_DIM), lambda b, h: (layer, b, h, 0, 0))],
        out_specs=[rspec, pl.BlockSpec((None, None, HEAD_DIM, HEAD_DIM), lambda b, h: (b, h, 0, 0))],
        out_shape=[jax.ShapeDtypeStruct((batch, N_HEADS, 1, HEAD_DIM), BF16),
                   jax.ShapeDtypeStruct((batch, N_HEADS, HEAD_DIM, HEAD_DIM), F32)],
        compiler_params=_params(("parallel", "parallel"), 32),
    )(colv(qd), colv(fd), lb.reshape(N_HEADS, HEAD_DIM, 1), rowv(vd), rowv(gd), g_out.reshape(1, HEAD_DIM), state)


def _gather_kernel(pt_ref, *refs):
    o_ref = refs[-1]
    for r in range(PAGES_PER_STEP):
        for c in range(PAGE_STREAMS):
            o_ref[r * PAGE_SIZE:(r + 1) * PAGE_SIZE, c * HEAD_DIM:(c + 1) * HEAD_DIM] = (
                refs[r][pl.ds(c, PAGE_SIZE, stride=PAGE_STREAMS), :])


def _gather_pages(pool, layer, page_table):
    view = _page_view(pool)
    width = PAGE_STREAMS * HEAD_DIM
    batch, n_pages = page_table.shape
    groups = n_pages // PAGES_PER_STEP
    rows = PAGES_PER_STEP * PAGE_SIZE
    in_specs = [pl.BlockSpec((None, None, PAGE_SIZE * PAGE_STREAMS, HEAD_DIM),
                             lambda b, g, pt, r=r: (layer, pt[b, g * PAGES_PER_STEP + r], 0, 0))
                for r in range(PAGES_PER_STEP)]
    return pl.pallas_call(
        _gather_kernel,
        grid_spec=pltpu.PrefetchScalarGridSpec(
            num_scalar_prefetch=1, grid=(batch, groups), in_specs=in_specs,
            out_specs=pl.BlockSpec((rows, width), lambda b, g, pt: (b * groups + g, 0))),
        out_shape=jax.ShapeDtypeStruct((batch * n_pages * PAGE_SIZE, width), pool.dtype),
        compiler_params=_params(("parallel", "arbitrary"), 40),
    )(page_table, *([view] * PAGES_PER_STEP))


def _ffn(x, mod_l, sub, mod_idx, g, wg, wu, wd, layer, rows_per_group):
    h = _modulate(x, g, mod_l[:, mod_idx, 1], mod_l[:, mod_idx, 0], rows_per_group)
    act = _glu(h, wg, wu, layer, sub)
    return _mm_resid(act, wd, (layer, sub), x, mod_l[:, mod_idx, 2], 0.5, rows_per_group, 2)


def _kv5(a, batch):
    return a.reshape(batch, -1, 2, KV_HEADS, HEAD_DIM)


def _prompt_mix(h, batch, w_in_p, gains, mixw):
    t = h.shape[0] // batch
    z = _proj(h, w_in_p, gains)
    seg = lambda name, w: z[:, SEG[name]:SEG[name] + w]
    kvc, kvs, kvw, kvm = seg("kvc", 512), seg("kvs", 512), seg("kvw", 512), seg("kvm", 512)
    cmp_kv = _compress(kvc, batch, mixw["cw1"], mixw["cb1"], mixw["cw2"], mixw["cpe"], mixw["qkg"][1])
    kb = lambda name: SEG[name] // HEAD_DIM
    o_a = _nsa(z, t, 128, 0, cmp_kv, z, (kb("kvs"), kb("kvs") + KV_HEADS), None,
               z, (kb("kvw"), kb("kvw") + KV_HEADS), 0, t, True)
    o_b = _gmlp(z, mixw["ws"], mixw["bs"], mixw["ln_g"], mixw["ln_b"])
    o_c = _moba(z, t, 128, 0, batch, z, (kb("kvm"), kb("kvm") + KV_HEADS), None, True)
    o_d, s_d = _hgrn(z, batch, mixw["lb"], mixw["out_g"])
    mixed = (o_a, o_b, o_c, o_d)
    w = min(NSA_WINDOW, t)
    state = (_kv5(kvc, batch), _kv5(kvs, batch), _kv5(kvm, batch), _kv5(kvw, batch)[:, t - w:], s_d)
    return mixed, state


def _pad_rows(a, batch, rows):
    out = jnp.zeros((batch, rows, a.shape[1]), a.dtype).at[:, 0].set(a)
    return out.reshape(batch * rows, a.shape[1])


def _sample_mix(h, batch, w_in_p, gains, mixw, pool_cmp, pool_sel, pool_moba, win_buf, state, page_table, layer):
    past = page_table.shape[1] * PAGE_SIZE
    z = _proj(h, w_in_p, gains)[:batch]
    seg = lambda name, w: z[:, SEG[name]:SEG[name] + w]
    kvc, kvs, kvw, kvm = seg("kvc", 512), seg("kvs", 512), seg("kvw", 512), seg("kvm", 512)
    zq = _pad_rows(z, batch, SAMPLE_ROWS)
    dense_sel = _gather_pages(pool_sel, layer, page_table)
    dense_moba = _gather_pages(pool_moba, layer, page_table)
    cmp_kv = _compress_paged(pool_cmp, layer, page_table, mixw["cw1"], mixw["cb1"], mixw["cw2"], mixw["cpe"],
                             mixw["qkg"][1])
    wlen = win_buf.shape[2]
    band = jnp.concatenate([win_buf[layer].reshape(batch, wlen, 512), kvw[:, None, :]], axis=1)
    win = jnp.concatenate([band, jnp.zeros((batch, WIN_BAND - wlen - 1, 512), F32)], axis=1)
    o_a = _nsa(zq, SAMPLE_ROWS, SAMPLE_ROWS, past, cmp_kv, dense_sel, (0, KV_HEADS),
               _pad_rows(kvs, batch, TAIL_ROWS), win.reshape(batch * WIN_BAND, 512), (0, KV_HEADS),
               past - wlen, past + 1, False)
    o_c = _moba(zq, SAMPLE_ROWS, SAMPLE_ROWS, past, batch, dense_moba, (0, KV_HEADS),
                _pad_rows(kvm, batch, TAIL_ROWS), False)
    o_b, v_n = _gmlp_step(seg("u", WIDTH), seg("v", WIDTH), mixw["ws"], mixw["bs"], mixw["ln_g"], mixw["ln_b"])
    o_d, s_d = _hgrn_step(seg("qd", WIDTH), seg("fd", WIDTH), seg("vd", WIDTH), seg("gd", WIDTH),
                          mixw["lb"], mixw["out_g"], state, layer)
    mixed = jnp.concatenate([o_a[::SAMPLE_ROWS], o_b, o_c[::SAMPLE_ROWS], o_d.reshape(batch, WIDTH)], axis=1)
    mixed = jnp.concatenate([mixed, jnp.zeros((h.shape[0] - batch, mixed.shape[1]), mixed.dtype)], axis=0)
    new_state = (_kv5(kvc, batch), _kv5(kvs, batch), _kv5(kvm, batch),
                 band[:, 1:].reshape(batch, wlen, 2, KV_HEADS, HEAD_DIM), s_d, v_n.reshape(batch, 1, WIDTH))
    return mixed, new_state


def kernel(x_prompt, x_sample, c_prompt, c_sample, cache_nsa_cmp_kv, cache_nsa_sel_kv, cache_moba_kv, cache_nsa_win_kv, state_hgrn, page_table, w_ada, b_ada, norm_g, w_ffn_gate, w_ffn_up, w_ffn_down, w_in, w_out, qk_norm_g, nsa_cmp_w1, nsa_cmp_b1, nsa_cmp_w2, nsa_cmp_pos, gmlp_ln_g, gmlp_ln_b, gmlp_ws, gmlp_bs, hgrn_lb_logits, hgrn_out_g):
    bp, t, d = x_prompt.shape
    bs = x_sample.shape[0]
    depth = w_in.shape[0]
    srows = 16
    lb_p = jax.nn.softmax(hgrn_lb_logits.astype(F32), axis=0)
    lb_all = jnp.clip(jnp.cumsum(lb_p, axis=0) - lb_p[0:1], 0.0, 1.0)

    c_all = jnp.concatenate([c_prompt, c_sample, jnp.zeros((16 - bp - bs, d), F32)], axis=0)
    mod = _ada_mod(c_all, w_ada, b_ada).reshape(depth, 16, 3, 3, d)

    xp = x_prompt.reshape(bp * t, d)
    xs = jnp.concatenate([x_sample.reshape(bs, d), jnp.zeros((srows - bs, d), F32)], axis=0)
    st_p, st_s = [], []
    for l in range(depth):
        mod_p = mod[l, :bp]
        mod_s = jnp.concatenate([mod[l, bp:bp + bs], jnp.zeros((srows - bs, 3, 3, d), F32)], axis=0)
        w_in_p = _permute_w_in(w_in[l])
        gains = _proj_gains(qk_norm_g[l])
        mixw = dict(qkg=qk_norm_g[l], cw1=nsa_cmp_w1[l], cb1=nsa_cmp_b1[l], cw2=nsa_cmp_w2[l], cpe=nsa_cmp_pos[l],
                    ln_g=gmlp_ln_g[l], ln_b=gmlp_ln_b[l], ws=gmlp_ws[l], bs=gmlp_bs[l], lb=lb_all[l],
                    out_g=hgrn_out_g[l])
        g = norm_g[l]

        xp = _ffn(xp, mod_p, 0, 0, g[0], w_ffn_gate, w_ffn_up, w_ffn_down, l, t)
        hp = _modulate(xp, g[1], mod_p[:, 1, 1], mod_p[:, 1, 0], t)
        mixed, sp = _prompt_mix(hp, bp, w_in_p, gains, mixw)
        xp = _mm_resid_parts(mixed, w_out, (l,), xp, mod_p[:, 1, 2], 1.0, t)
        xp = _ffn(xp, mod_p, 1, 2, g[2], w_ffn_gate, w_ffn_up, w_ffn_down, l, t)

        xs = _ffn(xs, mod_s, 0, 0, g[0], w_ffn_gate, w_ffn_up, w_ffn_down, l, 1)
        hs = _modulate(xs, g[1], mod_s[:, 1, 1], mod_s[:, 1, 0], 1)
        mixed, ss = _sample_mix(hs, bs, w_in_p, gains, mixw, cache_nsa_cmp_kv, cache_nsa_sel_kv, cache_moba_kv,
                                cache_nsa_win_kv, state_hgrn, page_table, l)
        xs = _mm_resid(mixed, w_out, (l,), xs, mod_s[:, 1, 2], 1.0, 1, 1)
        xs = _ffn(xs, mod_s, 1, 2, g[2], w_ffn_gate, w_ffn_up, w_ffn_down, l, 1)
        st_p.append(sp)
        st_s.append(ss)

    stack = lambda sts, i: jnp.stack([s[i] for s in sts])
    return (xp.reshape(bp, t, d), xs[:bs].reshape(bs, 1, d),
            stack(st_p, 0), stack(st_p, 1), stack(st_p, 2), stack(st_p, 3), stack(st_p, 4),
            stack(st_s, 0), stack(st_s, 1), stack(st_s, 2), stack(st_s, 3), stack(st_s, 4), stack(st_s, 5))
```

```python
import functools

import jax
import jax.numpy as jnp
from jax import lax
from jax.experimental import pallas as pl
from jax.experimental.pallas import tpu as pltpu

F32 = jnp.float32
BF16 = jnp.bfloat16

HEAD_DIM = 128
N_HEADS = 8
KV_HEADS = 2
GROUP = N_HEADS // KV_HEADS
WIDTH = N_HEADS * HEAD_DIM
PAGE_SIZE = 128
PAGES_PER_STEP = 16

NSA_CMP_STRIDE = 16
NSA_CMP_BLOCK = 32
NSA_CMP_HIDDEN = 256
NSA_SEL_BLOCK = 64
NSA_SEL_TOP = 16
NSA_N_LOCAL = 2
NSA_WINDOW = 512
GMLP_CHUNK = 128
MOBA_BLOCK = 256
MOBA_TOP = 3
HGRN_CHUNK = 64
NEG_INF = -1e30
FORCE_SCORE = 1e4
TINY = 1e-30
ATT_SCALE = HEAD_DIM ** -0.5

LANE = 128
KEY_TILE = 4096
WIN_BAND = NSA_WINDOW + 128
SAMPLE_ROWS = 8
TAIL_ROWS = 256

PROJ_TN = 256
SEG = {}
_off = 0
for _name, _w in (("qa", 1024), ("u", 1024), ("v", 1024), ("qc", 1024), ("qd", 1024), ("fd", 1024),
                  ("vd", 1024), ("gd", 1024), ("kvc", 512), ("kvs", 512), ("kvw", 512), ("kvm", 512),
                  ("ga", 256)):
    SEG[_name] = _off
    _off += _w
N_PROJ = _off

_ORIG = {}
_o = 0
for _name, _w in (("qa", 1024), ("kvc", 512), ("kvs", 512), ("kvw", 512), ("ga", 24), ("u", 1024), ("v", 1024),
                  ("qc", 1024), ("kvm", 512), ("qd", 1024), ("fd", 1024), ("vd", 1024), ("gd", 1024)):
    _ORIG[_name] = (_o, _w)
    _o += _w


def _params(sem, vmem_mb):
    return pltpu.CompilerParams(dimension_semantics=sem, vmem_limit_bytes=vmem_mb << 20)


def _sigmoid(x):
    return 1.0 / (1.0 + jnp.exp(-x))


def _silu(x):
    return x * _sigmoid(x)


def _gelu(x):
    return 0.5 * x * (1.0 + jnp.tanh(0.7978845608028654 * (x + 0.044715 * (x * x * x))))


def _dot(a, b):
    return jnp.dot(a, b, preferred_element_type=F32)


def _dot_nt(a, b):
    return lax.dot_general(a, b, (((1,), (1,)), ((), ())), preferred_element_type=F32)


def _dot_tn(a, b):
    return lax.dot_general(a, b, (((0,), (0,)), ((), ())), preferred_element_type=F32)


def _split3(a):
    hi = a.astype(BF16)
    r1 = a - hi.astype(F32)
    mid = r1.astype(BF16)
    lo = (r1 - mid.astype(F32)).astype(BF16)
    return hi, mid, lo


def _dot_f32_lhs(a, b_exact):
    hi, mid, lo = _split3(a)
    return _dot(hi, b_exact) + _dot(mid, b_exact) + _dot(lo, b_exact)


def _dot_f32_rhs(a_exact, b):
    hi, mid, lo = _split3(b)
    return _dot(a_exact, hi) + _dot(a_exact, mid) + _dot(a_exact, lo)


def _dot_nt_f32(a, b):
    ah, am, _ = _split3(a)
    bh, bm, _ = _split3(b)
    return _dot_nt(ah, bh) + _dot_nt(ah, bm) + _dot_nt(am, bh)


def _rms(x, g, eps=1e-6):
    return x * lax.rsqrt(jnp.mean(x * x, axis=-1, keepdims=True) + eps) * g


def _softmax_parts(s_list, m_list, never_empty=False):
    sm = [jnp.where(m, s, NEG_INF) for s, m in zip(s_list, m_list)]
    mx = functools.reduce(jnp.maximum, [jnp.max(s, axis=-1, keepdims=True) for s in sm])
    if never_empty:
        e = [jnp.exp(s - mx) for s in sm]
    else:
        e = [jnp.where(m, jnp.exp(s - mx), 0.0) for s, m in zip(sm, m_list)]
    den = functools.reduce(lambda a, b: a + b, [jnp.sum(x, axis=-1, keepdims=True) for x in e])
    return e, jnp.maximum(den, 1e-30)


def _rank(score, n, col):
    rank = jnp.zeros(score.shape, jnp.int32)
    for i in range(n):
        si = score[:, i:i + 1]
        before = jnp.where(si > score, 1, jnp.where((si == score) & (col > i), 1, 0))
        rank = rank + before
    return rank


def _ada_kernel(c_ref, w_ref, b_ref, o_ref):
    a = _silu(c_ref[...]).astype(BF16)
    o_ref[...] = _dot(a, w_ref[...].astype(BF16)) + b_ref[...]


def _ada_mod(c, w_ada, b_ada):
    depth, d, n = w_ada.shape
    rows = c.shape[0]
    tn = 512
    return pl.pallas_call(
        _ada_kernel,
        grid=(depth, n // tn),
        in_specs=[pl.BlockSpec((rows, d), lambda l, j: (0, 0)),
                  pl.BlockSpec((None, d, tn), lambda l, j: (l, 0, j)),
                  pl.BlockSpec((None, 1, tn), lambda l, j: (l, 0, j))],
        out_specs=pl.BlockSpec((None, rows, tn), lambda l, j: (l, 0, j)),
        out_shape=jax.ShapeDtypeStruct((depth, rows, n), F32),
        compiler_params=_params(("parallel", "parallel"), 40),
    )(c, w_ada, b_ada.reshape(depth, 1, n))


ROW_TILE = 1024
SAMPLE_PAD = 16
COMBINED_TILE = ROW_TILE + SAMPLE_PAD
COL_TILE = 256
NORM_CHUNK = 128


def _single_buffered(shape, index_map):
    return pl.BlockSpec(shape, index_map, pipeline_mode=pl.Buffered(1))


def _sample_out_spec(tn):
    return pl.BlockSpec((None, SAMPLE_PAD, tn), lambda i, j: (i, 0, j))


def _stage_modulated(lhs_ref, xp_ref, xs_ref, g_ref, scp_ref, shp_ref, scs_ref, shs_ref):
    g = g_ref[...]
    one_sc, sh = 1.0 + scp_ref[...], shp_ref[...]

    def body(c, carry):
        r0 = pl.multiple_of(c * NORM_CHUNK, NORM_CHUNK)
        lhs_ref[pl.ds(r0, NORM_CHUNK), :] = (_rms(xp_ref[pl.ds(r0, NORM_CHUNK), :], g) * one_sc + sh).astype(BF16)
        return carry

    lax.fori_loop(0, ROW_TILE // NORM_CHUNK, body, 0)
    lhs_ref[ROW_TILE:, :] = (_rms(xs_ref[...], g) * (1.0 + scs_ref[...]) + shs_ref[...]).astype(BF16)


def _mod_specs(d, rows_per_group):
    per = rows_per_group // ROW_TILE
    vec_p = pl.BlockSpec((None, 1, d), lambda i, j: (i // per, 0, 0))
    vec_s = pl.BlockSpec((SAMPLE_PAD, d), lambda i, j: (0, 0))
    return [_single_buffered((ROW_TILE, d), lambda i, j: (i, 0)), vec_s,
            pl.BlockSpec((1, d), lambda i, j: (0, 0)), vec_p, vec_p, vec_s, vec_s]


def _mod_args(xp, xs, g, mod_p, mod_s, idx):
    d = xp.shape[1]
    return (xp, xs, g.reshape(1, d), mod_p[:, idx, 1].reshape(-1, 1, d), mod_p[:, idx, 0].reshape(-1, 1, d),
            mod_s[:, idx, 1], mod_s[:, idx, 0])


def _norm_glu_kernel(xp_ref, xs_ref, g_ref, scp_ref, shp_ref, scs_ref, shs_ref, wg_ref, wu_ref, o_ref, lhs_ref):
    @pl.when(pl.program_id(1) == 0)
    def _():
        _stage_modulated(lhs_ref, xp_ref, xs_ref, g_ref, scp_ref, shp_ref, scs_ref, shs_ref)

    h = lhs_ref[...]
    a = _dot(h, wg_ref[...].astype(BF16))
    b = _dot(h, wu_ref[...].astype(BF16))
    o_ref[...] = (_silu(a) * b).astype(o_ref.dtype)


def _norm_glu(xp, xs, g, mod_p, mod_s, idx, wg, wu, layer, sub, rows_per_group):
    m, d = xp.shape
    f = wg.shape[-1]
    nt = m // ROW_TILE
    wspec = pl.BlockSpec((None, None, d, COL_TILE), lambda i, j: (layer, sub, 0, j))
    return pl.pallas_call(
        _norm_glu_kernel,
        grid=(nt, f // COL_TILE),
        in_specs=_mod_specs(d, rows_per_group) + [wspec, wspec],
        out_specs=pl.BlockSpec((COMBINED_TILE, COL_TILE), lambda i, j: (i, j)),
        out_shape=jax.ShapeDtypeStruct((nt * COMBINED_TILE, f), BF16),
        scratch_shapes=[pltpu.VMEM((COMBINED_TILE, d), BF16)],
        compiler_params=_params(("parallel", "arbitrary"), 52),
    )(*_mod_args(xp, xs, g, mod_p, mod_s, idx), wg, wu)


DOWN_TILES = 2


def _down_kernel(x_ref, w_ref, rp_ref, rs_ref, gp_ref, gs_ref, op_ref, os_ref, *, coef):
    acc = _dot(x_ref[...], w_ref[...].astype(BF16))
    gp = coef * gp_ref[...]
    for r in range(DOWN_TILES):
        rows = slice(r * ROW_TILE, (r + 1) * ROW_TILE)
        op_ref[rows, :] = rp_ref[rows, :] + gp * acc[r * COMBINED_TILE:r * COMBINED_TILE + ROW_TILE]
    os_ref[...] = rs_ref[...] + (coef * gs_ref[...]) * acc[ROW_TILE:COMBINED_TILE]


def _down(act, w, layer, sub, xp, xs, gate_p, gate_s, coef):
    m, n = xp.shape
    kdim = act.shape[1]
    tk = kdim // 2
    rows = DOWN_TILES * ROW_TILE
    assert rows == m // gate_p.shape[0]
    for ks in range(2):
        xp, xs = pl.pallas_call(
            functools.partial(_down_kernel, coef=coef),
            grid=(m // rows, n // COL_TILE),
            in_specs=[_single_buffered((DOWN_TILES * COMBINED_TILE, tk), lambda i, j, ks=ks: (i, ks)),
                      pl.BlockSpec((None, None, tk, COL_TILE), lambda i, j, ks=ks: (layer, sub, ks, j)),
                      pl.BlockSpec((rows, COL_TILE), lambda i, j: (i, j)),
                      pl.BlockSpec((SAMPLE_PAD, COL_TILE), lambda i, j: (0, j)),
                      pl.BlockSpec((None, 1, COL_TILE), lambda i, j: (i, 0, j)),
                      pl.BlockSpec((SAMPLE_PAD, COL_TILE), lambda i, j: (0, j))],
            out_specs=[pl.BlockSpec((rows, COL_TILE), lambda i, j: (i, j)), _sample_out_spec(COL_TILE)],
            out_shape=[jax.ShapeDtypeStruct((m, n), F32), jax.ShapeDtypeStruct((m // rows, SAMPLE_PAD, n), F32)],
            compiler_params=_params(("parallel", "arbitrary"), 56),
        )(act, w, xp, xs, gate_p.reshape(-1, 1, n), gate_s)
        xs = xs[0]
    return xp, xs


def _out_proj_kernel(*refs, n_parts):
    x_refs = refs[:n_parts]
    xs_ref, w_ref, rp_ref, rs_ref, gp_ref, gs_ref, op_ref, os_ref, lhs_ref = refs[n_parts:]

    @pl.when(pl.program_id(1) == 0)
    def _():
        off = 0
        for x_ref in x_refs:
            lhs_ref[:ROW_TILE, off:off + x_ref.shape[1]] = x_ref[...]
            off += x_ref.shape[1]
        lhs_ref[ROW_TILE:, :] = xs_ref[...]

    acc = _dot(lhs_ref[...], w_ref[...].astype(BF16))
    op_ref[...] = rp_ref[...] + gp_ref[...] * acc[:ROW_TILE]
    os_ref[...] = rs_ref[...] + gs_ref[...] * acc[ROW_TILE:]


def _out_proj(parts, mixed_s, w, layer, xp, xs, gate_p, gate_s, rows_per_group):
    m, n = xp.shape
    kdim = w.shape[1]
    per = rows_per_group // ROW_TILE
    in_specs = [pl.BlockSpec((ROW_TILE, x.shape[1]), lambda i, j: (i, 0)) for x in parts]
    in_specs += [pl.BlockSpec((SAMPLE_PAD, kdim), lambda i, j: (0, 0)),
                 pl.BlockSpec((None, kdim, COL_TILE), lambda i, j: (layer, 0, j)),
                 pl.BlockSpec((ROW_TILE, COL_TILE), lambda i, j: (i, j)),
                 pl.BlockSpec((SAMPLE_PAD, COL_TILE), lambda i, j: (0, j)),
                 pl.BlockSpec((None, 1, COL_TILE), lambda i, j: (i // per, 0, j)),
                 pl.BlockSpec((SAMPLE_PAD, COL_TILE), lambda i, j: (0, j))]
    xp, xs = pl.pallas_call(
        functools.partial(_out_proj_kernel, n_parts=len(parts)),
        grid=(m // ROW_TILE, n // COL_TILE),
        in_specs=in_specs,
        out_specs=[pl.BlockSpec((ROW_TILE, COL_TILE), lambda i, j: (i, j)), _sample_out_spec(COL_TILE)],
        out_shape=[jax.ShapeDtypeStruct((m, n), F32), jax.ShapeDtypeStruct((m // ROW_TILE, SAMPLE_PAD, n), F32)],
        scratch_shapes=[pltpu.VMEM((COMBINED_TILE, kdim), BF16)],
        compiler_params=_params(("parallel", "arbitrary"), 48),
    )(*parts, mixed_s, w, xp, xs, gate_p.reshape(-1, 1, n), gate_s)
    return xp, xs[0]


def _tiles(name, width):
    a = SEG[name] // PROJ_TN
    return a, a + width // PROJ_TN


_NORM_TILES = (_tiles("qa", 1024), _tiles("qc", 1024), _tiles("kvs", 256), _tiles("kvw", 256), _tiles("kvm", 256))
_GELU_TILES = (_tiles("u", 2048),)
_SILU_TILES = (_tiles("qd", 1024),)
_SIGM_TILES = (_tiles("ga", 256),)


def _in_ranges(j, ranges):
    return functools.reduce(jnp.logical_or, [(j >= a) & (j < b) for a, b in ranges])


def _proj_kernel(xp_ref, xs_ref, g_ref, scp_ref, shp_ref, scs_ref, shs_ref, w_ref, gain_ref, zp_ref, zs_ref, lhs_ref):
    j = pl.program_id(1)

    @pl.when(j == 0)
    def _():
        _stage_modulated(lhs_ref, xp_ref, xs_ref, g_ref, scp_ref, shp_ref, scs_ref, shs_ref)

    acc = _dot(lhs_ref[...], w_ref[...])
    is_norm = _in_ranges(j, _NORM_TILES)
    is_gelu = _in_ranges(j, _GELU_TILES)
    is_silu = _in_ranges(j, _SILU_TILES)
    is_sigm = _in_ranges(j, _SIGM_TILES)

    def store(val):
        zp_ref[...] = val[:ROW_TILE]
        zs_ref[...] = val[ROW_TILE:]

    @pl.when(is_norm)
    def _():
        gain = gain_ref[...]
        heads = [slice(hh * HEAD_DIM, (hh + 1) * HEAD_DIM) for hh in range(PROJ_TN // HEAD_DIM)]
        store(jnp.concatenate([_rms(acc[:, sl], gain[:, sl]) for sl in heads], axis=1))

    @pl.when(is_gelu)
    def _():
        store(_gelu(acc))

    @pl.when(is_silu)
    def _():
        store(_silu(acc) * ATT_SCALE)

    @pl.when(is_sigm)
    def _():
        store(_sigmoid(acc))

    @pl.when(jnp.logical_not(is_norm | is_gelu | is_silu | is_sigm))
    def _():
        store(acc)


def _proj(xp, xs, g, mod_p, mod_s, idx, w_in_p, gains, rows_per_group):
    m, d = xp.shape
    zp, zs = pl.pallas_call(
        _proj_kernel,
        grid=(m // ROW_TILE, N_PROJ // PROJ_TN),
        in_specs=_mod_specs(d, rows_per_group) + [pl.BlockSpec((d, PROJ_TN), lambda i, j: (0, j)),
                                                   pl.BlockSpec((None, 1, PROJ_TN), lambda i, j: (j, 0, 0))],
        out_specs=[pl.BlockSpec((ROW_TILE, PROJ_TN), lambda i, j: (i, j)), _sample_out_spec(PROJ_TN)],
        out_shape=[jax.ShapeDtypeStruct((m, N_PROJ), F32),
                   jax.ShapeDtypeStruct((m // ROW_TILE, SAMPLE_PAD, N_PROJ), F32)],
        scratch_shapes=[pltpu.VMEM((COMBINED_TILE, d), BF16)],
        compiler_params=_params(("parallel", "arbitrary"), 48),
    )(*_mod_args(xp, xs, g, mod_p, mod_s, idx), w_in_p, gains)
    return zp, zs[0]


def _permute_w_in(w_in_l):
    d = w_in_l.shape[0]

    def cols(name):
        a, w = _ORIG[name]
        return w_in_l[:, a:a + w]

    ga = cols("ga")
    pad = jnp.zeros((d, LANE - 12), w_in_l.dtype)
    parts = [cols(n) for n in ("qa", "u", "v", "qc", "qd", "fd", "vd", "gd", "kvc", "kvs", "kvw", "kvm")]
    parts += [ga[:, :12], pad, ga[:, 12:], pad]
    return jnp.concatenate(parts, axis=1).astype(BF16)


def _proj_gains(qkg):
    g = jnp.ones((N_PROJ // PROJ_TN, PROJ_TN), F32)
    two = lambda v: jnp.tile(v, PROJ_TN // HEAD_DIM)
    for name, idx, ntile in (("qa", 0, 4), ("qc", 4, 4), ("kvs", 2, 1), ("kvw", 3, 1), ("kvm", 5, 1)):
        t0 = SEG[name] // PROJ_TN
        g = g.at[t0:t0 + ntile].set(two(qkg[idx])[None, :])
    return g.reshape(-1, 1, PROJ_TN)


CMP_ROWS = 128


def _compress_body(x_of, w1_ref, b1_ref, w2_ref, pe_ref, gk_ref, o_ref, n_cmp):
    i = pl.program_id(1)
    row = lax.broadcasted_iota(jnp.int32, (CMP_ROWS, 1), 0)
    valid = (i * CMP_ROWS + row) < n_cmp
    half = NSA_CMP_STRIDE * HEAD_DIM
    for s in range(2):
        w1a = w1_ref[s, 0].astype(BF16)
        w1b = w1_ref[s, 1].astype(BF16)
        pe = pe_ref[s].astype(BF16)
        bias = (_dot(pe[:, :half], w1a) + _dot(pe[:, half:], w1b))[0:1] + b1_ref[s]
        w2 = w2_ref[s].astype(BF16)
        for k in range(KV_HEADS):
            x, xn = x_of(s * KV_HEADS + k)
            x = x.astype(BF16)
            h1 = _dot(x, w1a)
            h2 = _dot(x, w1b)
            h2n = _dot(xn.astype(BF16), w1b)
            h2s = jnp.where(row == CMP_ROWS - 1, h2n[0:1], pltpu.roll(h2, CMP_ROWS - 1, 0))
            hid = _gelu(h1 + h2s + bias)
            out = _dot(hid.astype(BF16), w2)
            if s == 0:
                out = _rms(out, gk_ref[...])
            o_ref[s, k] = jnp.where(valid, out, 0.0)


def _compress_kernel(r_ref, rn_ref, w1_ref, b1_ref, w2_ref, pe_ref, gk_ref, o_ref, *, n_cmp):
    row_w = 2 * KV_HEADS * HEAD_DIM

    def x_of(c):
        cols = [slice(p * row_w + c * HEAD_DIM, p * row_w + (c + 1) * HEAD_DIM) for p in range(NSA_CMP_STRIDE)]
        return (jnp.concatenate([r_ref[:, sl] for sl in cols], axis=1),
                jnp.concatenate([rn_ref[:, sl] for sl in cols], axis=1))

    _compress_body(x_of, w1_ref, b1_ref, w2_ref, pe_ref, gk_ref, o_ref, n_cmp)


PAGE_HALVES = PAGE_SIZE // NSA_CMP_STRIDE
PAGE_STREAMS = 2 * KV_HEADS


def _compress_paged_kernel(pt_ref, *refs, n_cmp):
    pages = refs[:PAGES_PER_STEP]
    nxt = refs[PAGES_PER_STEP]
    w1_ref, b1_ref, w2_ref, pe_ref, gk_ref, o_ref = refs[PAGES_PER_STEP + 1:]

    def flat(pg, c):
        return jnp.concatenate([pg[pl.ds(p * PAGE_STREAMS + c, PAGE_HALVES, stride=NSA_CMP_STRIDE * PAGE_STREAMS), :]
                                for p in range(NSA_CMP_STRIDE)], axis=1)

    def x_of(c):
        return jnp.concatenate([flat(pg, c) for pg in pages], axis=0), flat(nxt, c)

    _compress_body(x_of, w1_ref, b1_ref, w2_ref, pe_ref, gk_ref, o_ref, n_cmp)


def _compress_weights(w1, b1, w2, pe, gk):
    half = NSA_CMP_STRIDE * HEAD_DIM
    w1r = w1.reshape(2, 2, half, NSA_CMP_HIDDEN)
    pe8 = jnp.broadcast_to(pe.reshape(2, 1, 2 * half), (2, 8, 2 * half))
    args = (w1r, b1.reshape(2, 1, NSA_CMP_HIDDEN), w2, pe8, gk.reshape(1, HEAD_DIM))
    return args, [a.shape for a in args]


def _compress(kv_flat, batch, w1, b1, w2, pe, gk):
    n_half = kv_flat.shape[0] // batch // NSA_CMP_STRIDE
    nblk = n_half // CMP_ROWS
    width = NSA_CMP_STRIDE * kv_flat.shape[1]
    r = kv_flat.reshape(batch * n_half, width)
    last8 = batch * n_half // 8 - 1
    wargs, wshapes = _compress_weights(w1, b1, w2, pe, gk)
    full = lambda shape: pl.BlockSpec(shape, lambda b, i: (0,) * len(shape))
    return pl.pallas_call(
        functools.partial(_compress_kernel, n_cmp=n_half - 1),
        grid=(batch, nblk),
        in_specs=[pl.BlockSpec((CMP_ROWS, width), lambda b, i: (b * nblk + i, 0)),
                  pl.BlockSpec((8, width), lambda b, i: (jnp.minimum((b * nblk + i + 1) * (CMP_ROWS // 8), last8), 0))]
                 + [full(s) for s in wshapes],
        out_specs=pl.BlockSpec((None, 2, KV_HEADS, CMP_ROWS, HEAD_DIM), lambda b, i: (b, 0, 0, i, 0)),
        out_shape=jax.ShapeDtypeStruct((batch, 2, KV_HEADS, n_half, HEAD_DIM), F32),
        compiler_params=_params(("parallel", "arbitrary"), 48),
    )(r, r, *wargs)


def _page_view(pool):
    depth, n_phys = pool.shape[:2]
    return pool.reshape(depth, n_phys, PAGE_SIZE * PAGE_STREAMS, HEAD_DIM)


def _compress_paged(pool, layer, page_table, w1, b1, w2, pe, gk):
    view = _page_view(pool)
    batch, n_pages = page_table.shape
    groups = n_pages // PAGES_PER_STEP
    n_half = n_pages * PAGE_HALVES
    wargs, wshapes = _compress_weights(w1, b1, w2, pe, gk)
    page = lambda fn: pl.BlockSpec((None, None, PAGE_SIZE * PAGE_STREAMS, HEAD_DIM), fn)
    in_specs = [page(lambda b, g, pt, r=r: (layer, pt[b, g * PAGES_PER_STEP + r], 0, 0)) for r in range(PAGES_PER_STEP)]
    in_specs.append(page(lambda b, g, pt: (layer, pt[b, jnp.minimum((g + 1) * PAGES_PER_STEP, n_pages - 1)], 0, 0)))
    in_specs += [pl.BlockSpec(s, lambda b, g, pt, n=len(s): (0,) * n) for s in wshapes]
    return pl.pallas_call(
        functools.partial(_compress_paged_kernel, n_cmp=n_half - 1),
        grid_spec=pltpu.PrefetchScalarGridSpec(
            num_scalar_prefetch=1, grid=(batch, groups), in_specs=in_specs,
            out_specs=pl.BlockSpec((None, 2, KV_HEADS, CMP_ROWS, HEAD_DIM), lambda b, g, pt: (b, 0, 0, g, 0))),
        out_shape=jax.ShapeDtypeStruct((batch, 2, KV_HEADS, n_half, HEAD_DIM), F32),
        compiler_params=_params(("parallel", "arbitrary"), 48),
    )(page_table, *([view] * (PAGES_PER_STEP + 1)), *wargs)


def _stack_heads(q):
    return jnp.concatenate([q[:, g * HEAD_DIM:(g + 1) * HEAD_DIM] for g in range(GROUP)], axis=0)


def _block_onehot(n_blocks, kpos, shift):
    blk = lax.broadcasted_iota(jnp.int32, (n_blocks, 1), 0)
    return jnp.where((kpos >> shift) == blk, 1.0, 0.0).astype(BF16)


def _nsa_kernel(*refs, tq, qpos0, n_sel, extents, has_tail, win_pos0, win_len):
    if len(extents) > 1:
        refs, osel_ref = refs[:-1], refs[-1]
    if has_tail:
        q_ref, ga_ref, kc_ref, vc_ref, ks_ref, vs_ref, kt_ref, vt_ref, kw_ref, vw_ref, o_ref = refs
    else:
        q_ref, ga_ref, kc_ref, vc_ref, ks_ref, vs_ref, kw_ref, vw_ref, o_ref = refs
    q0 = qpos0 + pl.program_id(2) * tq
    rows = GROUP * tq
    q4 = _stack_heads(q_ref[...]).astype(BF16)
    qpos = q0 + (lax.broadcasted_iota(jnp.int32, (rows, 1), 0) & (tq - 1))

    kc = kc_ref[...].astype(BF16)
    vc = vc_ref[...].astype(BF16)
    n_cp = kc.shape[0]
    cidx = lax.broadcasted_iota(jnp.int32, (1, n_cp), 1)
    s_cmp = _dot_nt(q4, kc) * ATT_SCALE
    (e_cmp,), den = _softmax_parts([s_cmp], [(cidx * NSA_CMP_STRIDE + (NSA_CMP_BLOCK - 1)) <= qpos])
    p_cmp = e_cmp / den
    o_cmp = _dot(p_cmp.astype(BF16), vc)

    n_sp = -(-n_sel // LANE) * LANE
    psum = functools.reduce(lambda a, b: a + b, [p_cmp[g * tq:(g + 1) * tq] for g in range(GROUP)])
    ci = lax.broadcasted_iota(jnp.int32, (n_cp, 1), 0) * NSA_CMP_STRIDE
    sj = lax.broadcasted_iota(jnp.int32, (1, n_sp), 1) * NSA_SEL_BLOCK
    cover = jnp.where((ci <= sj + (NSA_SEL_BLOCK - 1)) & (ci + (NSA_CMP_BLOCK - 1) >= sj), 1.0, 0.0).astype(BF16)
    imp = _dot_f32_lhs(psum, cover)
    col = lax.broadcasted_iota(jnp.int32, (1, n_sp), 1)
    qpos_t = q0 + lax.broadcasted_iota(jnp.int32, (tq, 1), 0)
    back = (qpos_t >> 6) - col
    valid = back >= 0
    forced = (col == 0) | (valid & (back < NSA_N_LOCAL))
    score = jnp.where(valid, jnp.where(forced, FORCE_SCORE, imp), NEG_INF)
    rank = _rank(score, n_sel, col)
    sel = jnp.where((rank < NSA_SEL_TOP) & (score > 0.5 * NEG_INF), 1.0, 0.0).astype(BF16)
    sel4 = jnp.concatenate([sel] * GROUP, axis=0)

    def sel_attend(n_keys):
        s_list, m_list, v_list = [], [], []

        def add_tile(k_bf, v_bf, kpos):
            picked = _dot(sel4, _block_onehot(n_sp, kpos, 6))
            s_list.append(_dot_nt(q4, k_bf) * ATT_SCALE)
            m_list.append(jnp.where(kpos <= qpos, picked, 0.0) > 0.5)
            v_list.append(v_bf)

        kt = min(n_keys, KEY_TILE)
        for t0 in range(0, n_keys, kt):
            add_tile(ks_ref[t0:t0 + kt, :].astype(BF16), vs_ref[t0:t0 + kt, :].astype(BF16),
                     t0 + lax.broadcasted_iota(jnp.int32, (1, kt), 1))
        if has_tail:
            add_tile(kt_ref[...].astype(BF16), vt_ref[...].astype(BF16),
                     n_keys + lax.broadcasted_iota(jnp.int32, (1, TAIL_ROWS), 1))
        e_list, den = _softmax_parts(s_list, m_list, never_empty=True)
        return functools.reduce(lambda a, b: a + b, [_dot(e.astype(BF16), v) for e, v in zip(e_list, v_list)]) / den

    if len(extents) == 1:
        o_sel = sel_attend(extents[0])
    else:
        need = (q0 - qpos0 + tq - 1) // extents[0]
        for idx, ext in enumerate(extents):
            @pl.when(need == idx)
            def _(ext=ext):
                osel_ref[...] = sel_attend(ext)
        o_sel = osel_ref[...]

    start = jnp.clip(q0 - NSA_WINDOW - win_pos0, 0, win_len - WIN_BAND)
    start = pl.multiple_of(start, LANE)
    kw = kw_ref[pl.ds(start, WIN_BAND), :].astype(BF16)
    vw = vw_ref[pl.ds(start, WIN_BAND), :].astype(BF16)
    kposw = win_pos0 + start + lax.broadcasted_iota(jnp.int32, (1, WIN_BAND), 1)
    s_win = _dot_nt(q4, kw) * ATT_SCALE
    m_win = jnp.where(kposw <= qpos, qpos - kposw, NSA_WINDOW) < NSA_WINDOW
    (e_win,), den = _softmax_parts([s_win], [m_win], never_empty=True)
    o_win = _dot(e_win.astype(BF16), vw) / den

    ga = ga_ref[...]
    outs = []
    for g in range(GROUP):
        r = slice(g * tq, (g + 1) * tq)
        outs.append(ga[:, 3 * g:3 * g + 1] * o_cmp[r] + ga[:, 3 * g + 1:3 * g + 2] * o_sel[r]
                    + ga[:, 3 * g + 2:3 * g + 3] * o_win[r])
    o_ref[...] = jnp.concatenate(outs, axis=1).astype(o_ref.dtype)


CAUSAL_STEP = 512


def _causal_extents(tk, tq):
    if tk % CAUSAL_STEP or CAUSAL_STEP % tq:
        return (tk,)
    return tuple(range(CAUSAL_STEP, tk + 1, CAUSAL_STEP))


def _nsa(zq, tq_total, tq, qpos0, cmp_kv, sel_main, sel_cols, sel_tail, win, win_cols, win_pos0, n_total,
         causal_skip):
    batch = cmp_kv.shape[0]
    nq = tq_total // tq
    n_cp = cmp_kv.shape[3]
    tk = sel_main.shape[0] // batch
    lw = win.shape[0] // batch
    n_sel = -(-n_total // NSA_SEL_BLOCK)
    extents = _causal_extents(tk, tq) if causal_skip else (tk,)
    scratch = [pltpu.VMEM((GROUP * tq, HEAD_DIM), F32)] if len(extents) > 1 else []
    qa_blk = SEG["qa"] // (GROUP * HEAD_DIM)
    ga_blk = SEG["ga"] // LANE
    in_specs = [
        pl.BlockSpec((tq, GROUP * HEAD_DIM), lambda b, k, i: (b * nq + i, qa_blk + k)),
        pl.BlockSpec((tq, LANE), lambda b, k, i: (b * nq + i, ga_blk + k)),
        pl.BlockSpec((None, None, None, n_cp, HEAD_DIM), lambda b, k, i: (b, 0, k, 0, 0)),
        pl.BlockSpec((None, None, None, n_cp, HEAD_DIM), lambda b, k, i: (b, 1, k, 0, 0)),
        pl.BlockSpec((tk, HEAD_DIM), lambda b, k, i: (b, sel_cols[0] + k)),
        pl.BlockSpec((tk, HEAD_DIM), lambda b, k, i: (b, sel_cols[1] + k)),
    ]
    args = [zq, zq, cmp_kv, cmp_kv, sel_main, sel_main]
    if sel_tail is not None:
        in_specs += [pl.BlockSpec((TAIL_ROWS, HEAD_DIM), lambda b, k, i: (b, k)),
                     pl.BlockSpec((TAIL_ROWS, HEAD_DIM), lambda b, k, i: (b, KV_HEADS + k))]
        args += [sel_tail, sel_tail]
    in_specs += [pl.BlockSpec((lw, HEAD_DIM), lambda b, k, i: (b, win_cols[0] + k)),
                 pl.BlockSpec((lw, HEAD_DIM), lambda b, k, i: (b, win_cols[1] + k))]
    args += [win, win]
    return pl.pallas_call(
        functools.partial(_nsa_kernel, tq=tq, qpos0=qpos0, n_sel=n_sel, extents=extents,
                          has_tail=sel_tail is not None, win_pos0=win_pos0, win_len=lw),
        grid=(batch, KV_HEADS, nq),
        in_specs=in_specs,
        out_specs=pl.BlockSpec((tq, GROUP * HEAD_DIM), lambda b, k, i: (b * nq + i, k)),
        out_shape=jax.ShapeDtypeStruct((batch * tq_total, WIDTH), BF16),
        scratch_shapes=scratch,
        compiler_params=_params(("parallel", "parallel", "arbitrary"), 56),
    )(*args)


def _moba_kernel(*refs, tq, qpos0, extents, has_tail):
    if has_tail:
        q_ref, k_ref, v_ref, kt_ref, vt_ref, o_ref = refs
    else:
        q_ref, k_ref, v_ref, o_ref = refs
    q0 = qpos0 + pl.program_id(2) * tq
    rows = GROUP * tq
    q4f = _stack_heads(q_ref[...])
    q4 = q4f.astype(BF16)
    qpos = q0 + (lax.broadcasted_iota(jnp.int32, (rows, 1), 0) & (tq - 1))
    cur = qpos >> 8
    col = lax.broadcasted_iota(jnp.int32, (1, LANE), 1)

    def attend(n_keys):
        kt = min(n_keys, KEY_TILE)
        n_blk = n_keys // MOBA_BLOCK
        means = [jnp.sum(k_ref[t0:t0 + kt, :].reshape(kt // MOBA_BLOCK, MOBA_BLOCK, HEAD_DIM), axis=1)
                 * (1.0 / MOBA_BLOCK) for t0 in range(0, n_keys, kt)]
        if n_blk < LANE:
            means.append(jnp.zeros((LANE - n_blk, HEAD_DIM), F32))
        kmean = jnp.concatenate(means, axis=0)
        gate = jnp.where(col < cur, _dot_nt_f32(q4f, kmean), NEG_INF)
        rank = _rank(gate, n_blk, col)
        sel = jnp.where((rank < MOBA_TOP) & (gate > 0.5 * NEG_INF), 1.0, 0.0).astype(BF16)

        s_list, m_list, v_list = [], [], []

        def add_tile(k_bf, v_bf, kpos):
            picked = _dot(sel, _block_onehot(LANE, kpos, 8))
            own = jnp.where((kpos >> 8) == cur, jnp.where(kpos <= qpos, 1.0, 0.0), 0.0)
            s_list.append(_dot_nt(q4, k_bf) * ATT_SCALE)
            m_list.append((picked + own) > 0.5)
            v_list.append(v_bf)

        for t0 in range(0, n_keys, kt):
            add_tile(k_ref[t0:t0 + kt, :].astype(BF16), v_ref[t0:t0 + kt, :].astype(BF16),
                     t0 + lax.broadcasted_iota(jnp.int32, (1, kt), 1))
        if has_tail:
            add_tile(kt_ref[...].astype(BF16), vt_ref[...].astype(BF16),
                     n_keys + lax.broadcasted_iota(jnp.int32, (1, TAIL_ROWS), 1))
        e_list, den = _softmax_parts(s_list, m_list, never_empty=True)
        o = functools.reduce(lambda a, b: a + b, [_dot(e.astype(BF16), v) for e, v in zip(e_list, v_list)]) / den
        o_ref[...] = jnp.concatenate([o[g * tq:(g + 1) * tq] for g in range(GROUP)], axis=1).astype(o_ref.dtype)

    if len(extents) == 1:
        attend(extents[0])
    else:
        need = (q0 - qpos0 + tq - 1) // extents[0]
        for idx, ext in enumerate(extents):
            @pl.when(need == idx)
            def _(ext=ext):
                attend(ext)


def _moba(zq, tq_total, tq, qpos0, batch, main, cols, tail, causal_skip):
    nq = tq_total // tq
    tk = main.shape[0] // batch
    extents = _causal_extents(tk, tq) if causal_skip else (tk,)
    qc_blk = SEG["qc"] // (GROUP * HEAD_DIM)
    in_specs = [pl.BlockSpec((tq, GROUP * HEAD_DIM), lambda b, k, i: (b * nq + i, qc_blk + k)),
                pl.BlockSpec((tk, HEAD_DIM), lambda b, k, i: (b, cols[0] + k)),
                pl.BlockSpec((tk, HEAD_DIM), lambda b, k, i: (b, cols[1] + k))]
    args = [zq, main, main]
    if tail is not None:
        in_specs += [pl.BlockSpec((TAIL_ROWS, HEAD_DIM), lambda b, k, i: (b, k)),
                     pl.BlockSpec((TAIL_ROWS, HEAD_DIM), lambda b, k, i: (b, KV_HEADS + k))]
        args += [tail, tail]
    return pl.pallas_call(
        functools.partial(_moba_kernel, tq=tq, qpos0=qpos0, extents=extents, has_tail=tail is not None),
        grid=(batch, KV_HEADS, nq),
        in_specs=in_specs,
        out_specs=pl.BlockSpec((tq, GROUP * HEAD_DIM), lambda b, k, i: (b * nq + i, k)),
        out_shape=jax.ShapeDtypeStruct((batch * tq_total, WIDTH), BF16),
        compiler_params=_params(("parallel", "parallel", "arbitrary"), 56),
    )(*args)


def _layer_norm(v, g, b, eps=1e-5):
    mu = jnp.mean(v, axis=-1, keepdims=True)
    var = jnp.mean(jnp.square(v - mu), axis=-1, keepdims=True)
    return (v - mu) * lax.rsqrt(var + eps) * g + b


def _gmlp_kernel(u_ref, v_ref, ws_ref, bst_ref, g_ref, b_ref, o_ref):
    vn = _layer_norm(v_ref[...], g_ref[...], b_ref[...]).astype(BF16)
    r = lax.broadcasted_iota(jnp.int32, (GMLP_CHUNK, GMLP_CHUNK), 0)
    c = lax.broadcasted_iota(jnp.int32, (GMLP_CHUNK, GMLP_CHUNK), 1)
    bst = bst_ref[...]
    for g in range(N_HEADS):
        sl = slice(g * HEAD_DIM, (g + 1) * HEAD_DIM)
        wm = jnp.where(c <= r, ws_ref[g], 0.0).astype(BF16)
        sv = _dot(wm, vn[:, sl]) + bst[:, g:g + 1]
        o_ref[:, sl] = (u_ref[:, sl] * sv).astype(o_ref.dtype)


def _gmlp(z, ws, bs, ln_g, ln_b):
    m = z.shape[0]
    ub, vb = SEG["u"] // WIDTH, SEG["v"] // WIDTH
    full = lambda shape: pl.BlockSpec(shape, lambda i: (0,) * len(shape))
    return pl.pallas_call(
        _gmlp_kernel,
        grid=(m // GMLP_CHUNK,),
        in_specs=[pl.BlockSpec((GMLP_CHUNK, WIDTH), lambda i: (i, ub)),
                  pl.BlockSpec((GMLP_CHUNK, WIDTH), lambda i: (i, vb)),
                  full(ws.shape), full((GMLP_CHUNK, N_HEADS)), full((1, WIDTH)), full((1, WIDTH))],
        out_specs=pl.BlockSpec((GMLP_CHUNK, WIDTH), lambda i: (i, 0)),
        out_shape=jax.ShapeDtypeStruct((m, WIDTH), BF16),
        compiler_params=_params(("parallel",), 32),
    )(z, z, ws, bs.T, ln_g.reshape(1, WIDTH), ln_b.reshape(1, WIDTH))


def _hgrn_gates(fd, lb):
    f_gate = lb + (1.0 - lb) * _sigmoid(fd)
    return jnp.log(jnp.maximum(f_gate, TINY)), (1.0 - lb) * _sigmoid(-fd)


def _hgrn_readout(o, gd, g_out):
    return _rms(o, g_out) * _silu(gd)


HGRN_HEADS_PER_STEP = 2


def _hgrn_kernel(q_ref, f_ref, v_ref, gd_ref, lb_ref, go_ref, o_ref, s_ref):
    c = HGRN_CHUNK
    t_total = q_ref.shape[0]
    g_out = go_ref[...]
    ri = lax.broadcasted_iota(jnp.int32, (c, c), 0)
    ci = lax.broadcasted_iota(jnp.int32, (c, c), 1)
    tri = jnp.where(ci <= ri, 1.0, 0.0).astype(BF16)
    lane = lax.broadcasted_iota(jnp.int32, (1, c), 1)
    diag_mask = (ci <= ri) & ((ci >> 3) == (ri >> 3))

    def head_chunk(r0, hh, st):
        hs = slice(hh * HEAD_DIM, (hh + 1) * HEAD_DIM)
        q = q_ref[pl.ds(r0, c), hs]
        v = v_ref[pl.ds(r0, c), hs]
        lf, k = _hgrn_gates(f_ref[pl.ds(r0, c), hs], lb_ref[:, hs])
        cum = _dot_f32_rhs(tri, lf)
        a = jnp.zeros((c, c), F32)
        for bs in (32, 16, 8):
            nb = c // bs
            refq = jnp.concatenate(
                [jnp.zeros((bs, HEAD_DIM), F32)]
                + [jnp.broadcast_to(cum[b * bs - 1:b * bs], (bs, HEAD_DIM)) for b in range(1, nb)], axis=0)
            refk = jnp.concatenate(
                [jnp.broadcast_to(cum[(b + 1) * bs - 1:(b + 1) * bs], (bs, HEAD_DIM)) for b in range(nb)], axis=0)
            qt = (q * jnp.exp(jnp.minimum(cum - refq, 0.0))).astype(BF16)
            kt = (k * jnp.exp(jnp.minimum(refk - cum, 0.0))).astype(BF16)
            sh = bs.bit_length() - 1
            lvl = (((ri >> sh) & 1) == 1) & ((ci >> sh) == (ri >> sh) - 1)
            a = a + jnp.where(lvl, _dot_nt(qt, kt), 0.0)
        rows = []
        for blk in range(c // 8):
            b0 = blk * 8
            qb, kb, cb = q[b0:b0 + 8], k[b0:b0 + 8], cum[b0:b0 + 8]
            acc = jnp.zeros((8, c), F32)
            for s in range(8):
                w = qb * kb[s:s + 1] * jnp.exp(jnp.minimum(cb - cb[s:s + 1], 0.0))
                acc = jnp.where(lane == b0 + s, jnp.sum(w, axis=1, keepdims=True), acc)
            rows.append(acc)
        a = a + jnp.where(diag_mask, jnp.concatenate(rows, axis=0), 0.0)
        o = _dot(a.astype(BF16), v.astype(BF16)) + _dot_nt((q * jnp.exp(cum)).astype(BF16), st.astype(BF16))
        o_ref[pl.ds(r0, c), hs] = _hgrn_readout(o, gd_ref[pl.ds(r0, c), hs], g_out).astype(o_ref.dtype)
        last = cum[c - 1:c]
        kk = (k * jnp.exp(last - cum)).astype(BF16)
        return st * jnp.exp(last) + _dot_tn(v.astype(BF16), kk)

    def chunk(n, sts):
        r0 = pl.multiple_of(n * c, c)
        return tuple(head_chunk(r0, hh, st) for hh, st in enumerate(sts))

    zero = jnp.zeros((HEAD_DIM, HEAD_DIM), F32)
    sts = lax.fori_loop(0, t_total // c, chunk, (zero,) * HGRN_HEADS_PER_STEP, unroll=2)
    for hh, st in enumerate(sts):
        s_ref[hh] = st.T


def _hgrn(z, batch, lb, g_out):
    t = z.shape[0] // batch
    hw = HGRN_HEADS_PER_STEP * HEAD_DIM
    blk = lambda name: SEG[name] // hw
    col = lambda name: pl.BlockSpec((t, hw), lambda b, h: (b, blk(name) + h))
    return pl.pallas_call(
        _hgrn_kernel,
        grid=(batch, N_HEADS // HGRN_HEADS_PER_STEP),
        in_specs=[col("qd"), col("fd"), col("vd"), col("gd"),
                  pl.BlockSpec((1, hw), lambda b, h: (0, h)),
                  pl.BlockSpec((1, HEAD_DIM), lambda b, h: (0, 0))],
        out_specs=[pl.BlockSpec((t, hw), lambda b, h: (b, h)),
                   pl.BlockSpec((None, HGRN_HEADS_PER_STEP, HEAD_DIM, HEAD_DIM), lambda b, h: (b, h, 0, 0))],
        out_shape=[jax.ShapeDtypeStruct((batch * t, WIDTH), BF16),
                   jax.ShapeDtypeStruct((batch, N_HEADS, HEAD_DIM, HEAD_DIM), F32)],
        compiler_params=_params(("parallel", "parallel"), 32),
    )(z, z, z, z, lb.reshape(1, WIDTH), g_out.reshape(1, HEAD_DIM))


def _gmlp_step_kernel(u_ref, v_ref, w_ref, b_ref, g_ref, bb_ref, o_ref, vn_ref):
    vn = _layer_norm(v_ref[...], g_ref[...], bb_ref[...])
    vn_ref[...] = vn
    o_ref[...] = (u_ref[...] * (w_ref[...] * vn + b_ref[...])).astype(o_ref.dtype)


def _gmlp_step(u, v, ws, bs, ln_g, ln_b):
    rows = u.shape[0]
    w_row = jnp.repeat(ws[:, 0, 0], HEAD_DIM).reshape(1, WIDTH)
    b_row = jnp.repeat(bs[:, 0], HEAD_DIM).reshape(1, WIDTH)
    return pl.pallas_call(
        _gmlp_step_kernel,
        out_shape=[jax.ShapeDtypeStruct((rows, WIDTH), BF16), jax.ShapeDtypeStruct((rows, WIDTH), F32)],
    )(u, v, w_row, b_row, ln_g.reshape(1, WIDTH), ln_b.reshape(1, WIDTH))


def _hgrn_step_kernel(qc_ref, fc_ref, lbc_ref, v_ref, gd_ref, go_ref, s0_ref, o_ref, s_ref):
    lf, k = _hgrn_gates(fc_ref[...], lbc_ref[...])
    s_new = jnp.exp(lf) * s0_ref[...] + k * v_ref[...]
    s_ref[...] = s_new
    o = jnp.sum(qc_ref[...] * s_new, axis=0, keepdims=True)
    o_ref[...] = _hgrn_readout(o, gd_ref[...], go_ref[...]).astype(o_ref.dtype)


def _hgrn_step(qd, fd, vd, gd, lb, g_out, state, layer):
    batch = qd.shape[0]
    colv = lambda a: a.reshape(batch, N_HEADS, HEAD_DIM, 1)
    rowv = lambda a: a.reshape(batch, N_HEADS, 1, HEAD_DIM)
    cspec = pl.BlockSpec((None, None, HEAD_DIM, 1), lambda b, h: (b, h, 0, 0))
    rspec = pl.BlockSpec((None, None, 1, HEAD_DIM), lambda b, h: (b, h, 0, 0))
    return pl.pallas_call(
        _hgrn_step_kernel,
        grid=(batch, N_HEADS),
        in_specs=[cspec, cspec, pl.BlockSpec((None, HEAD_DIM, 1), lambda b, h: (h, 0, 0)), rspec, rspec,
                  pl.BlockSpec((1, HEAD_DIM), lambda b, h: (0, 0)),
                  pl.BlockSpec((None, None, None, HEAD_DIM, HEAD_DIM), lambda b, h: (layer, b, h, 0, 0))],
        out_specs=[rspec, pl.BlockSpec((None, None, HEAD_DIM, HEAD_DIM), lambda b, h: (b, h, 0, 0))],
        out_shape=[jax.ShapeDtypeStruct((batch, N_HEADS, 1, HEAD_DIM), BF16),
                   jax.ShapeDtypeStruct((batch, N_HEADS, HEAD_DIM, HEAD_DIM), F32)],
        compiler_params=_params(("parallel", "parallel"), 32),
    )(colv(qd), colv(fd), lb.reshape(N_HEADS, HEAD_DIM, 1), rowv(vd), rowv(gd), g_out.reshape(1, HEAD_DIM), state)


def _gather_kernel(pt_ref, *refs):
    o_ref = refs[-1]
    for r in range(PAGES_PER_STEP):
        for c in range(PAGE_STREAMS):
            o_ref[r * PAGE_SIZE:(r + 1) * PAGE_SIZE, c * HEAD_DIM:(c + 1) * HEAD_DIM] = (
                refs[r][pl.ds(c, PAGE_SIZE, stride=PAGE_STREAMS), :])


def _gather_pages(pool, layer, page_table):
    view = _page_view(pool)
    width = PAGE_STREAMS * HEAD_DIM
    batch, n_pages = page_table.shape
    groups = n_pages // PAGES_PER_STEP
    rows = PAGES_PER_STEP * PAGE_SIZE
    in_specs = [pl.BlockSpec((None, None, PAGE_SIZE * PAGE_STREAMS, HEAD_DIM),
                             lambda b, g, pt, r=r: (layer, pt[b, g * PAGES_PER_STEP + r], 0, 0))
                for r in range(PAGES_PER_STEP)]
    return pl.pallas_call(
        _gather_kernel,
        grid_spec=pltpu.PrefetchScalarGridSpec(
            num_scalar_prefetch=1, grid=(batch, groups), in_specs=in_specs,
            out_specs=pl.BlockSpec((rows, width), lambda b, g, pt: (b * groups + g, 0))),
        out_shape=jax.ShapeDtypeStruct((batch * n_pages * PAGE_SIZE, width), pool.dtype),
        compiler_params=_params(("parallel", "arbitrary"), 40),
    )(page_table, *([view] * PAGES_PER_STEP))


def _ffn(xp, xs, mod_p, mod_s, sub, mod_idx, g, wg, wu, wd, layer, rows_per_group):
    act = _norm_glu(xp, xs, g, mod_p, mod_s, mod_idx, wg, wu, layer, sub, rows_per_group)
    return _down(act, wd, layer, sub, xp, xs, mod_p[:, mod_idx, 2], mod_s[:, mod_idx, 2], 0.5)


def _kv5(a, batch):
    return a.reshape(batch, -1, 2, KV_HEADS, HEAD_DIM)


def _prompt_mix(z, batch, mixw):
    t = z.shape[0] // batch
    seg = lambda name, w: z[:, SEG[name]:SEG[name] + w]
    kvc, kvs, kvw, kvm = seg("kvc", 512), seg("kvs", 512), seg("kvw", 512), seg("kvm", 512)
    cmp_kv = _compress(kvc, batch, mixw["cw1"], mixw["cb1"], mixw["cw2"], mixw["cpe"], mixw["qkg"][1])
    kb = lambda name: SEG[name] // HEAD_DIM
    o_a = _nsa(z, t, 128, 0, cmp_kv, z, (kb("kvs"), kb("kvs") + KV_HEADS), None,
               z, (kb("kvw"), kb("kvw") + KV_HEADS), 0, t, True)
    o_b = _gmlp(z, mixw["ws"], mixw["bs"], mixw["ln_g"], mixw["ln_b"])
    o_c = _moba(z, t, 128, 0, batch, z, (kb("kvm"), kb("kvm") + KV_HEADS), None, True)
    o_d, s_d = _hgrn(z, batch, mixw["lb"], mixw["out_g"])
    mixed = (o_a, o_b, o_c, o_d)
    w = min(NSA_WINDOW, t)
    state = (_kv5(kvc, batch), _kv5(kvs, batch), _kv5(kvm, batch), _kv5(kvw, batch)[:, t - w:], s_d)
    return mixed, state


def _pad_rows(a, batch, rows):
    out = jnp.zeros((batch, rows, a.shape[1]), a.dtype).at[:, 0].set(a)
    return out.reshape(batch * rows, a.shape[1])


def _sample_mix(z_pad, batch, mixw, pool_cmp, pool_sel, pool_moba, win_buf, state, page_table, layer):
    past = page_table.shape[1] * PAGE_SIZE
    z = z_pad[:batch]
    seg = lambda name, w: z[:, SEG[name]:SEG[name] + w]
    kvc, kvs, kvw, kvm = seg("kvc", 512), seg("kvs", 512), seg("kvw", 512), seg("kvm", 512)
    zq = _pad_rows(z, batch, SAMPLE_ROWS)
    dense_sel = _gather_pages(pool_sel, layer, page_table)
    dense_moba = _gather_pages(pool_moba, layer, page_table)
    cmp_kv = _compress_paged(pool_cmp, layer, page_table, mixw["cw1"], mixw["cb1"], mixw["cw2"], mixw["cpe"],
                             mixw["qkg"][1])
    wlen = win_buf.shape[2]
    band = jnp.concatenate([win_buf[layer].reshape(batch, wlen, 512), kvw[:, None, :]], axis=1)
    win = jnp.concatenate([band, jnp.zeros((batch, WIN_BAND - wlen - 1, 512), F32)], axis=1)
    o_a = _nsa(zq, SAMPLE_ROWS, SAMPLE_ROWS, past, cmp_kv, dense_sel, (0, KV_HEADS),
               _pad_rows(kvs, batch, TAIL_ROWS), win.reshape(batch * WIN_BAND, 512), (0, KV_HEADS),
               past - wlen, past + 1, False)
    o_c = _moba(zq, SAMPLE_ROWS, SAMPLE_ROWS, past, batch, dense_moba, (0, KV_HEADS),
                _pad_rows(kvm, batch, TAIL_ROWS), False)
    o_b, v_n = _gmlp_step(seg("u", WIDTH), seg("v", WIDTH), mixw["ws"], mixw["bs"], mixw["ln_g"], mixw["ln_b"])
    o_d, s_d = _hgrn_step(seg("qd", WIDTH), seg("fd", WIDTH), seg("vd", WIDTH), seg("gd", WIDTH),
                          mixw["lb"], mixw["out_g"], state, layer)
    mixed = jnp.concatenate([o_a[::SAMPLE_ROWS], o_b, o_c[::SAMPLE_ROWS], o_d.reshape(batch, WIDTH)], axis=1)
    mixed = jnp.concatenate([mixed, jnp.zeros((z_pad.shape[0] - batch, mixed.shape[1]), mixed.dtype)], axis=0)
    new_state = (_kv5(kvc, batch), _kv5(kvs, batch), _kv5(kvm, batch),
                 band[:, 1:].reshape(batch, wlen, 2, KV_HEADS, HEAD_DIM), s_d, v_n.reshape(batch, 1, WIDTH))
    return mixed, new_state


def kernel(x_prompt, x_sample, c_prompt, c_sample, cache_nsa_cmp_kv, cache_nsa_sel_kv, cache_moba_kv, cache_nsa_win_kv, state_hgrn, page_table, w_ada, b_ada, norm_g, w_ffn_gate, w_ffn_up, w_ffn_down, w_in, w_out, qk_norm_g, nsa_cmp_w1, nsa_cmp_b1, nsa_cmp_w2, nsa_cmp_pos, gmlp_ln_g, gmlp_ln_b, gmlp_ws, gmlp_bs, hgrn_lb_logits, hgrn_out_g):
    bp, t, d = x_prompt.shape
    bs = x_sample.shape[0]
    depth = w_in.shape[0]
    srows = SAMPLE_PAD
    lb_p = jax.nn.softmax(hgrn_lb_logits.astype(F32), axis=0)
    lb_all = jnp.clip(jnp.cumsum(lb_p, axis=0) - lb_p[0:1], 0.0, 1.0)

    c_all = jnp.concatenate([c_prompt, c_sample, jnp.zeros((16 - bp - bs, d), F32)], axis=0)
    mod = _ada_mod(c_all, w_ada, b_ada).reshape(depth, 16, 3, 3, d)

    xp = x_prompt.reshape(bp * t, d)
    xs = jnp.concatenate([x_sample.reshape(bs, d), jnp.zeros((srows - bs, d), F32)], axis=0)
    st_p, st_s = [], []
    for l in range(depth):
        mod_p = mod[l, :bp]
        mod_s = jnp.concatenate([mod[l, bp:bp + bs], jnp.zeros((srows - bs, 3, 3, d), F32)], axis=0)
        w_in_p = _permute_w_in(w_in[l])
        gains = _proj_gains(qk_norm_g[l])
        mixw = dict(qkg=qk_norm_g[l], cw1=nsa_cmp_w1[l], cb1=nsa_cmp_b1[l], cw2=nsa_cmp_w2[l], cpe=nsa_cmp_pos[l],
                    ln_g=gmlp_ln_g[l], ln_b=gmlp_ln_b[l], ws=gmlp_ws[l], bs=gmlp_bs[l], lb=lb_all[l],
                    out_g=hgrn_out_g[l])
        g = norm_g[l]

        xp, xs = _ffn(xp, xs, mod_p, mod_s, 0, 0, g[0], w_ffn_gate, w_ffn_up, w_ffn_down, l, t)
        zp, zs = _proj(xp, xs, g[1], mod_p, mod_s, 1, w_in_p, gains, t)
        mixed_p, sp = _prompt_mix(zp, bp, mixw)
        mixed_s, ss = _sample_mix(zs, bs, mixw, cache_nsa_cmp_kv, cache_nsa_sel_kv, cache_moba_kv,
                                  cache_nsa_win_kv, state_hgrn, page_table, l)
        xp, xs = _out_proj(mixed_p, mixed_s, w_out, l, xp, xs, mod_p[:, 1, 2], mod_s[:, 1, 2], t)
        xp, xs = _ffn(xp, xs, mod_p, mod_s, 1, 2, g[2], w_ffn_gate, w_ffn_up, w_ffn_down, l, t)
        st_p.append(sp)
        st_s.append(ss)

    stack = lambda sts, i: jnp.stack([s[i] for s in sts])
    return (xp.reshape(bp, t, d), xs[:bs].reshape(bs, 1, d),
            stack(st_p, 0), stack(st_p, 1), stack(st_p, 2), stack(st_p, 3), stack(st_p, 4),
            stack(st_s, 0), stack(st_s, 1), stack(st_s, 2), stack(st_s, 3), stack(st_s, 4), stack(st_s, 5))
```

```python
import functools

import jax
import jax.numpy as jnp
from jax import lax
from jax.experimental import pallas as pl
from jax.experimental.pallas import tpu as pltpu

F32 = jnp.float32
BF16 = jnp.bfloat16

HEAD_DIM = 128
N_HEADS = 8
KV_HEADS = 2
GROUP = N_HEADS // KV_HEADS
WIDTH = N_HEADS * HEAD_DIM
PAGE_SIZE = 128
PAGES_PER_STEP = 16

NSA_CMP_STRIDE = 16
NSA_CMP_BLOCK = 32
NSA_CMP_HIDDEN = 256
NSA_SEL_BLOCK = 64
NSA_SEL_TOP = 16
NSA_N_LOCAL = 2
NSA_WINDOW = 512
GMLP_CHUNK = 128
MOBA_BLOCK = 256
MOBA_TOP = 3
HGRN_CHUNK = 64
NEG_INF = -1e30
FORCE_SCORE = 1e4
TINY = 1e-30
ATT_SCALE = HEAD_DIM ** -0.5

LANE = 128
KEY_TILE = 4096
WIN_BAND = NSA_WINDOW + 128
SAMPLE_ROWS = 8
TAIL_ROWS = 256

PROJ_TN = 256
PROJ_TILES_PER_STEP = 2
SEG = {}
_off = 0
for _name, _w in (("qa", 1024), ("u", 1024), ("v", 1024), ("qc", 1024), ("qd", 1024), ("fd", 1024),
                  ("vd", 1024), ("gd", 1024), ("kvc", 512), ("kvs", 512), ("kvw", 512), ("kvm", 512),
                  ("ga", 256), ("pad", 256)):
    SEG[_name] = _off
    _off += _w
N_PROJ = _off

_ORIG = {}
_o = 0
for _name, _w in (("qa", 1024), ("kvc", 512), ("kvs", 512), ("kvw", 512), ("ga", 24), ("u", 1024), ("v", 1024),
                  ("qc", 1024), ("kvm", 512), ("qd", 1024), ("fd", 1024), ("vd", 1024), ("gd", 1024)):
    _ORIG[_name] = (_o, _w)
    _o += _w


def _params(sem, vmem_mb):
    return pltpu.CompilerParams(dimension_semantics=sem, vmem_limit_bytes=vmem_mb << 20)


def _sigmoid(x):
    return 1.0 / (1.0 + jnp.exp(-x))


def _silu(x):
    return x * _sigmoid(x)


def _gelu(x):
    return 0.5 * x * (1.0 + jnp.tanh(0.7978845608028654 * (x + 0.044715 * (x * x * x))))


def _dot(a, b):
    return jnp.dot(a, b, preferred_element_type=F32)


def _dot_nt(a, b):
    return lax.dot_general(a, b, (((1,), (1,)), ((), ())), preferred_element_type=F32)


def _dot_tn(a, b):
    return lax.dot_general(a, b, (((0,), (0,)), ((), ())), preferred_element_type=F32)


def _split3(a):
    hi = a.astype(BF16)
    r1 = a - hi.astype(F32)
    mid = r1.astype(BF16)
    lo = (r1 - mid.astype(F32)).astype(BF16)
    return hi, mid, lo


def _dot_f32_lhs(a, b_exact):
    hi, mid, lo = _split3(a)
    return _dot(hi, b_exact) + _dot(mid, b_exact) + _dot(lo, b_exact)


def _dot_f32_rhs(a_exact, b):
    hi, mid, lo = _split3(b)
    return _dot(a_exact, hi) + _dot(a_exact, mid) + _dot(a_exact, lo)


def _dot_nt_f32(a, b):
    ah, am, _ = _split3(a)
    bh, bm, _ = _split3(b)
    return _dot_nt(ah, bh) + _dot_nt(ah, bm) + _dot_nt(am, bh)


def _rms(x, g, eps=1e-6):
    return x * lax.rsqrt(jnp.mean(x * x, axis=-1, keepdims=True) + eps) * g


def _softmax_parts(s_list, m_list, never_empty=False):
    sm = [jnp.where(m, s, NEG_INF) for s, m in zip(s_list, m_list)]
    mx = functools.reduce(jnp.maximum, [jnp.max(s, axis=-1, keepdims=True) for s in sm])
    if never_empty:
        e = [jnp.exp(s - mx) for s in sm]
    else:
        e = [jnp.where(m, jnp.exp(s - mx), 0.0) for s, m in zip(sm, m_list)]
    den = functools.reduce(lambda a, b: a + b, [jnp.sum(x, axis=-1, keepdims=True) for x in e])
    return e, jnp.maximum(den, 1e-30)


def _rank(score, n, col):
    rank = jnp.zeros(score.shape, jnp.int32)
    for i in range(n):
        si = score[:, i:i + 1]
        before = jnp.where(si > score, 1, jnp.where((si == score) & (col > i), 1, 0))
        rank = rank + before
    return rank


def _ada_kernel(c_ref, w_ref, b_ref, o_ref):
    a = _silu(c_ref[...]).astype(BF16)
    o_ref[...] = _dot(a, w_ref[...].astype(BF16)) + b_ref[...]


def _ada_mod(c, w_ada, b_ada):
    depth, d, n = w_ada.shape
    rows = c.shape[0]
    tn = 512
    return pl.pallas_call(
        _ada_kernel,
        grid=(depth, n // tn),
        in_specs=[pl.BlockSpec((rows, d), lambda l, j: (0, 0)),
                  pl.BlockSpec((None, d, tn), lambda l, j: (l, 0, j)),
                  pl.BlockSpec((None, 1, tn), lambda l, j: (l, 0, j))],
        out_specs=pl.BlockSpec((None, rows, tn), lambda l, j: (l, 0, j)),
        out_shape=jax.ShapeDtypeStruct((depth, rows, n), F32),
        compiler_params=_params(("parallel", "parallel"), 40),
    )(c, w_ada, b_ada.reshape(depth, 1, n))


ROW_TILE = 1024
SAMPLE_PAD = 16
COMBINED_TILE = ROW_TILE + SAMPLE_PAD
COL_TILE = 256
NORM_CHUNK = 128


def _single_buffered(shape, index_map):
    return pl.BlockSpec(shape, index_map, pipeline_mode=pl.Buffered(1))


def _sample_out_spec(tn):
    return pl.BlockSpec((None, SAMPLE_PAD, tn), lambda i, j: (i, 0, j))


def _stage_modulated(lhs_ref, xp_ref, xs_ref, g_ref, scp_ref, shp_ref, scs_ref, shs_ref):
    g = g_ref[...]
    one_sc, sh = 1.0 + scp_ref[...], shp_ref[...]

    def body(c, carry):
        r0 = pl.multiple_of(c * NORM_CHUNK, NORM_CHUNK)
        lhs_ref[pl.ds(r0, NORM_CHUNK), :] = (_rms(xp_ref[pl.ds(r0, NORM_CHUNK), :], g) * one_sc + sh).astype(BF16)
        return carry

    lax.fori_loop(0, ROW_TILE // NORM_CHUNK, body, 0)
    lhs_ref[ROW_TILE:, :] = (_rms(xs_ref[...], g) * (1.0 + scs_ref[...]) + shs_ref[...]).astype(BF16)


def _mod_specs(d, rows_per_group):
    per = rows_per_group // ROW_TILE
    vec_p = pl.BlockSpec((None, 1, d), lambda i, j: (i // per, 0, 0))
    vec_s = pl.BlockSpec((SAMPLE_PAD, d), lambda i, j: (0, 0))
    return [_single_buffered((ROW_TILE, d), lambda i, j: (i, 0)), vec_s,
            pl.BlockSpec((1, d), lambda i, j: (0, 0)), vec_p, vec_p, vec_s, vec_s]


def _mod_args(xp, xs, g, mod_p, mod_s, idx):
    d = xp.shape[1]
    return (xp, xs, g.reshape(1, d), mod_p[:, idx, 1].reshape(-1, 1, d), mod_p[:, idx, 0].reshape(-1, 1, d),
            mod_s[:, idx, 1], mod_s[:, idx, 0])


def _norm_glu_kernel(xp_ref, xs_ref, g_ref, scp_ref, shp_ref, scs_ref, shs_ref, wg_ref, wu_ref, o_ref, lhs_ref):
    @pl.when(pl.program_id(1) == 0)
    def _():
        _stage_modulated(lhs_ref, xp_ref, xs_ref, g_ref, scp_ref, shp_ref, scs_ref, shs_ref)

    h = lhs_ref[...]
    a = _dot(h, wg_ref[...].astype(BF16))
    b = _dot(h, wu_ref[...].astype(BF16))
    o_ref[...] = (_silu(a) * b).astype(o_ref.dtype)


def _norm_glu(xp, xs, g, mod_p, mod_s, idx, wg, wu, layer, sub, rows_per_group):
    m, d = xp.shape
    f = wg.shape[-1]
    nt = m // ROW_TILE
    wspec = pl.BlockSpec((None, None, d, COL_TILE), lambda i, j: (layer, sub, 0, j))
    return pl.pallas_call(
        _norm_glu_kernel,
        grid=(nt, f // COL_TILE),
        in_specs=_mod_specs(d, rows_per_group) + [wspec, wspec],
        out_specs=pl.BlockSpec((COMBINED_TILE, COL_TILE), lambda i, j: (i, j)),
        out_shape=jax.ShapeDtypeStruct((nt * COMBINED_TILE, f), BF16),
        scratch_shapes=[pltpu.VMEM((COMBINED_TILE, d), BF16)],
        compiler_params=_params(("parallel", "arbitrary"), 52),
    )(*_mod_args(xp, xs, g, mod_p, mod_s, idx), wg, wu)


DOWN_TILES = 2


def _down_kernel(x_ref, w_ref, rp_ref, rs_ref, gp_ref, gs_ref, op_ref, os_ref, *, coef):
    acc = _dot(x_ref[...], w_ref[...].astype(BF16))
    gp = coef * gp_ref[...]
    for r in range(DOWN_TILES):
        rows = slice(r * ROW_TILE, (r + 1) * ROW_TILE)
        op_ref[rows, :] = rp_ref[rows, :] + gp * acc[r * COMBINED_TILE:r * COMBINED_TILE + ROW_TILE]
    os_ref[...] = rs_ref[...] + (coef * gs_ref[...]) * acc[ROW_TILE:COMBINED_TILE]


def _down(act, w, layer, sub, xp, xs, gate_p, gate_s, coef):
    m, n = xp.shape
    kdim = act.shape[1]
    tk = kdim // 2
    rows = DOWN_TILES * ROW_TILE
    assert rows == m // gate_p.shape[0]
    for ks in range(2):
        xp, xs = pl.pallas_call(
            functools.partial(_down_kernel, coef=coef),
            grid=(m // rows, n // COL_TILE),
            in_specs=[_single_buffered((DOWN_TILES * COMBINED_TILE, tk), lambda i, j, ks=ks: (i, ks)),
                      pl.BlockSpec((None, None, tk, COL_TILE), lambda i, j, ks=ks: (layer, sub, ks, j)),
                      pl.BlockSpec((rows, COL_TILE), lambda i, j: (i, j)),
                      pl.BlockSpec((SAMPLE_PAD, COL_TILE), lambda i, j: (0, j)),
                      pl.BlockSpec((None, 1, COL_TILE), lambda i, j: (i, 0, j)),
                      pl.BlockSpec((SAMPLE_PAD, COL_TILE), lambda i, j: (0, j))],
            out_specs=[pl.BlockSpec((rows, COL_TILE), lambda i, j: (i, j)), _sample_out_spec(COL_TILE)],
            out_shape=[jax.ShapeDtypeStruct((m, n), F32), jax.ShapeDtypeStruct((m // rows, SAMPLE_PAD, n), F32)],
            compiler_params=_params(("parallel", "arbitrary"), 56),
        )(act, w, xp, xs, gate_p.reshape(-1, 1, n), gate_s)
        xs = xs[0]
    return xp, xs


def _out_proj_kernel(*refs, n_parts):
    x_refs = refs[:n_parts]
    xs_ref, w_ref, rp_ref, rs_ref, gp_ref, gs_ref, op_ref, os_ref, lhs_ref = refs[n_parts:]

    @pl.when(pl.program_id(1) == 0)
    def _():
        off = 0
        for x_ref in x_refs:
            lhs_ref[:ROW_TILE, off:off + x_ref.shape[1]] = x_ref[...]
            off += x_ref.shape[1]
        lhs_ref[ROW_TILE:, :] = xs_ref[...]

    acc = _dot(lhs_ref[...], w_ref[...].astype(BF16))
    op_ref[...] = rp_ref[...] + gp_ref[...] * acc[:ROW_TILE]
    os_ref[...] = rs_ref[...] + gs_ref[...] * acc[ROW_TILE:]


def _out_proj(parts, mixed_s, w, layer, xp, xs, gate_p, gate_s, rows_per_group):
    m, n = xp.shape
    kdim = w.shape[1]
    per = rows_per_group // ROW_TILE
    in_specs = [pl.BlockSpec((ROW_TILE, x.shape[1]), lambda i, j: (i, 0)) for x in parts]
    in_specs += [pl.BlockSpec((SAMPLE_PAD, kdim), lambda i, j: (0, 0)),
                 pl.BlockSpec((None, kdim, COL_TILE), lambda i, j: (layer, 0, j)),
                 pl.BlockSpec((ROW_TILE, COL_TILE), lambda i, j: (i, j)),
                 pl.BlockSpec((SAMPLE_PAD, COL_TILE), lambda i, j: (0, j)),
                 pl.BlockSpec((None, 1, COL_TILE), lambda i, j: (i // per, 0, j)),
                 pl.BlockSpec((SAMPLE_PAD, COL_TILE), lambda i, j: (0, j))]
    xp, xs = pl.pallas_call(
        functools.partial(_out_proj_kernel, n_parts=len(parts)),
        grid=(m // ROW_TILE, n // COL_TILE),
        in_specs=in_specs,
        out_specs=[pl.BlockSpec((ROW_TILE, COL_TILE), lambda i, j: (i, j)), _sample_out_spec(COL_TILE)],
        out_shape=[jax.ShapeDtypeStruct((m, n), F32), jax.ShapeDtypeStruct((m // ROW_TILE, SAMPLE_PAD, n), F32)],
        scratch_shapes=[pltpu.VMEM((COMBINED_TILE, kdim), BF16)],
        compiler_params=_params(("parallel", "arbitrary"), 48),
    )(*parts, mixed_s, w, xp, xs, gate_p.reshape(-1, 1, n), gate_s)
    return xp, xs[0]


def _tiles(name, width):
    a = SEG[name] // PROJ_TN
    return a, a + width // PROJ_TN


_NORM_TILES = (_tiles("qa", 1024), _tiles("qc", 1024), _tiles("kvs", 256), _tiles("kvw", 256), _tiles("kvm", 256))
_GELU_TILES = (_tiles("u", 2048),)
_SILU_TILES = (_tiles("qd", 1024),)
_SIGM_TILES = (_tiles("ga", 256),)


def _in_ranges(j, ranges):
    return functools.reduce(jnp.logical_or, [(j >= a) & (j < b) for a, b in ranges])


def _proj_kernel(xp_ref, xs_ref, g_ref, scp_ref, shp_ref, scs_ref, shs_ref, w_ref, gain_ref, zp_ref, zs_ref, lhs_ref):
    j = pl.program_id(1)

    @pl.when(j == 0)
    def _():
        _stage_modulated(lhs_ref, xp_ref, xs_ref, g_ref, scp_ref, shp_ref, scs_ref, shs_ref)

    lhs = lhs_ref[...]
    accs = [_dot(lhs, w_ref[:, c * PROJ_TN:(c + 1) * PROJ_TN]) for c in range(PROJ_TILES_PER_STEP)]
    for c, acc in enumerate(accs):
        _proj_epilogue(j * PROJ_TILES_PER_STEP + c, acc, gain_ref, zp_ref, zs_ref,
                       slice(c * PROJ_TN, (c + 1) * PROJ_TN))


def _proj_epilogue(tile, acc, gain_ref, zp_ref, zs_ref, cols):
    is_norm = _in_ranges(tile, _NORM_TILES)
    is_gelu = _in_ranges(tile, _GELU_TILES)
    is_silu = _in_ranges(tile, _SILU_TILES)
    is_sigm = _in_ranges(tile, _SIGM_TILES)

    def store(val):
        zp_ref[:, cols] = val[:ROW_TILE]
        zs_ref[:, cols] = val[ROW_TILE:]

    @pl.when(is_norm)
    def _():
        gain = gain_ref[:, cols]
        heads = [slice(hh * HEAD_DIM, (hh + 1) * HEAD_DIM) for hh in range(PROJ_TN // HEAD_DIM)]
        store(jnp.concatenate([_rms(acc[:, sl], gain[:, sl]) for sl in heads], axis=1))

    @pl.when(is_gelu)
    def _():
        store(_gelu(acc))

    @pl.when(is_silu)
    def _():
        store(_silu(acc) * ATT_SCALE)

    @pl.when(is_sigm)
    def _():
        store(_sigmoid(acc))

    @pl.when(jnp.logical_not(is_norm | is_gelu | is_silu | is_sigm))
    def _():
        store(acc)


def _proj(xp, xs, g, mod_p, mod_s, idx, w_in_p, gains, rows_per_group):
    m, d = xp.shape
    tn = PROJ_TILES_PER_STEP * PROJ_TN
    zp, zs = pl.pallas_call(
        _proj_kernel,
        grid=(m // ROW_TILE, N_PROJ // tn),
        in_specs=_mod_specs(d, rows_per_group) + [pl.BlockSpec((d, tn), lambda i, j: (0, j)),
                                                   pl.BlockSpec((None, 1, tn), lambda i, j: (j, 0, 0))],
        out_specs=[pl.BlockSpec((ROW_TILE, tn), lambda i, j: (i, j)), _sample_out_spec(tn)],
        out_shape=[jax.ShapeDtypeStruct((m, N_PROJ), F32),
                   jax.ShapeDtypeStruct((m // ROW_TILE, SAMPLE_PAD, N_PROJ), F32)],
        scratch_shapes=[pltpu.VMEM((COMBINED_TILE, d), BF16)],
        compiler_params=_params(("parallel", "arbitrary"), 48),
    )(*_mod_args(xp, xs, g, mod_p, mod_s, idx), w_in_p, gains)
    return zp, zs[0]


W_IN_ROWS = 256
GATES_PER_KV = 3 * GROUP


def _permute_w_in_kernel(w_ref, o_ref):
    for name in ("qa", "u", "v", "qc", "qd", "fd", "vd", "gd", "kvc", "kvs", "kvw", "kvm"):
        src, width = _ORIG[name]
        o_ref[:, SEG[name]:SEG[name] + width] = w_ref[:, src:src + width].astype(o_ref.dtype)
    lane = lax.broadcasted_iota(jnp.int32, (1, LANE), 1)
    for k in range(KV_HEADS):
        src = _ORIG["ga"][0] + k * GATES_PER_KV
        blk = jnp.where(lane < GATES_PER_KV, w_ref[:, src:src + LANE], 0.0)
        o_ref[:, SEG["ga"] + k * LANE:SEG["ga"] + (k + 1) * LANE] = blk.astype(o_ref.dtype)
    o_ref[:, SEG["pad"]:] = jnp.zeros((W_IN_ROWS, N_PROJ - SEG["pad"]), o_ref.dtype)


def _permute_w_in(w_in, layer):
    _, d, n_in = w_in.shape
    return pl.pallas_call(
        _permute_w_in_kernel,
        grid=(d // W_IN_ROWS,),
        in_specs=[pl.BlockSpec((None, W_IN_ROWS, n_in), lambda i: (layer, i, 0))],
        out_specs=pl.BlockSpec((W_IN_ROWS, N_PROJ), lambda i: (i, 0)),
        out_shape=jax.ShapeDtypeStruct((d, N_PROJ), BF16),
        compiler_params=_params(("parallel",), 48),
    )(w_in)


def _proj_gains(qkg):
    g = jnp.ones((N_PROJ // PROJ_TN, PROJ_TN), F32)
    two = lambda v: jnp.tile(v, PROJ_TN // HEAD_DIM)
    for name, idx, ntile in (("qa", 0, 4), ("qc", 4, 4), ("kvs", 2, 1), ("kvw", 3, 1), ("kvm", 5, 1)):
        t0 = SEG[name] // PROJ_TN
        g = g.at[t0:t0 + ntile].set(two(qkg[idx])[None, :])
    return g.reshape(-1, 1, PROJ_TILES_PER_STEP * PROJ_TN)


CMP_ROWS = 128


def _compress_body(x_of, w1_ref, b1_ref, w2_ref, pe_ref, gk_ref, o_ref, n_cmp):
    i = pl.program_id(1)
    row = lax.broadcasted_iota(jnp.int32, (CMP_ROWS, 1), 0)
    valid = (i * CMP_ROWS + row) < n_cmp
    half = NSA_CMP_STRIDE * HEAD_DIM
    for s in range(2):
        w1a = w1_ref[s, 0]
        w1b = w1_ref[s, 1]
        pe = pe_ref[s]
        bias = (_dot(pe[:, :half], w1a) + _dot(pe[:, half:], w1b))[0:1] + b1_ref[s]
        w2 = w2_ref[s]
        for k in range(KV_HEADS):
            x, xn = x_of(s * KV_HEADS + k)
            x = x.astype(BF16)
            h1 = _dot(x, w1a)
            h2 = _dot(x, w1b)
            h2n = _dot(xn.astype(BF16), w1b)
            h2s = jnp.where(row == CMP_ROWS - 1, h2n[0:1], pltpu.roll(h2, CMP_ROWS - 1, 0))
            hid = _gelu(h1 + h2s + bias)
            out = _dot(hid.astype(BF16), w2)
            if s == 0:
                out = _rms(out, gk_ref[...])
            o_ref[s, k] = jnp.where(valid, out, 0.0)


def _compress_kernel(r_ref, rn_ref, w1_ref, b1_ref, w2_ref, pe_ref, gk_ref, o_ref, *, n_cmp):
    row_w = 2 * KV_HEADS * HEAD_DIM

    def x_of(c):
        cols = [slice(p * row_w + c * HEAD_DIM, p * row_w + (c + 1) * HEAD_DIM) for p in range(NSA_CMP_STRIDE)]
        return (jnp.concatenate([r_ref[:, sl] for sl in cols], axis=1),
                jnp.concatenate([rn_ref[:, sl] for sl in cols], axis=1))

    _compress_body(x_of, w1_ref, b1_ref, w2_ref, pe_ref, gk_ref, o_ref, n_cmp)


PAGE_HALVES = PAGE_SIZE // NSA_CMP_STRIDE
PAGE_STREAMS = 2 * KV_HEADS


def _compress_paged_kernel(pt_ref, *refs, n_cmp):
    pages = refs[:PAGES_PER_STEP]
    nxt = refs[PAGES_PER_STEP]
    w1_ref, b1_ref, w2_ref, pe_ref, gk_ref, o_ref = refs[PAGES_PER_STEP + 1:]

    def flat(pg, c):
        return jnp.concatenate([pg[pl.ds(p * PAGE_STREAMS + c, PAGE_HALVES, stride=NSA_CMP_STRIDE * PAGE_STREAMS), :]
                                for p in range(NSA_CMP_STRIDE)], axis=1)

    def x_of(c):
        return jnp.concatenate([flat(pg, c) for pg in pages], axis=0), flat(nxt, c)

    _compress_body(x_of, w1_ref, b1_ref, w2_ref, pe_ref, gk_ref, o_ref, n_cmp)


def _compress_weights(w1, b1, w2, pe, gk):
    half = NSA_CMP_STRIDE * HEAD_DIM
    w1r = w1.reshape(2, 2, half, NSA_CMP_HIDDEN)
    pe8 = jnp.broadcast_to(pe.reshape(2, 1, 2 * half), (2, 8, 2 * half))
    args = (w1r.astype(BF16), b1.reshape(2, 1, NSA_CMP_HIDDEN), w2.astype(BF16), pe8.astype(BF16),
            gk.reshape(1, HEAD_DIM))
    return args, [a.shape for a in args]


def _compress(kv_flat, batch, w1, b1, w2, pe, gk):
    n_half = kv_flat.shape[0] // batch // NSA_CMP_STRIDE
    nblk = n_half // CMP_ROWS
    width = NSA_CMP_STRIDE * kv_flat.shape[1]
    r = kv_flat.reshape(batch * n_half, width)
    last8 = batch * n_half // 8 - 1
    wargs, wshapes = _compress_weights(w1, b1, w2, pe, gk)
    full = lambda shape: pl.BlockSpec(shape, lambda b, i: (0,) * len(shape))
    return pl.pallas_call(
        functools.partial(_compress_kernel, n_cmp=n_half - 1),
        grid=(batch, nblk),
        in_specs=[pl.BlockSpec((CMP_ROWS, width), lambda b, i: (b * nblk + i, 0)),
                  pl.BlockSpec((8, width), lambda b, i: (jnp.minimum((b * nblk + i + 1) * (CMP_ROWS // 8), last8), 0))]
                 + [full(s) for s in wshapes],
        out_specs=pl.BlockSpec((None, 2, KV_HEADS, CMP_ROWS, HEAD_DIM), lambda b, i: (b, 0, 0, i, 0)),
        out_shape=jax.ShapeDtypeStruct((batch, 2, KV_HEADS, n_half, HEAD_DIM), F32),
        compiler_params=_params(("parallel", "arbitrary"), 48),
    )(r, r, *wargs)


def _page_view(pool):
    depth, n_phys = pool.shape[:2]
    return pool.reshape(depth, n_phys, PAGE_SIZE * PAGE_STREAMS, HEAD_DIM)


def _compress_paged(pool, layer, page_table, w1, b1, w2, pe, gk):
    view = _page_view(pool)
    batch, n_pages = page_table.shape
    groups = n_pages // PAGES_PER_STEP
    n_half = n_pages * PAGE_HALVES
    wargs, wshapes = _compress_weights(w1, b1, w2, pe, gk)
    page = lambda fn: pl.BlockSpec((None, None, PAGE_SIZE * PAGE_STREAMS, HEAD_DIM), fn)
    in_specs = [page(lambda b, g, pt, r=r: (layer, pt[b, g * PAGES_PER_STEP + r], 0, 0)) for r in range(PAGES_PER_STEP)]
    in_specs.append(page(lambda b, g, pt: (layer, pt[b, jnp.minimum((g + 1) * PAGES_PER_STEP, n_pages - 1)], 0, 0)))
    in_specs += [pl.BlockSpec(s, lambda b, g, pt, n=len(s): (0,) * n) for s in wshapes]
    return pl.pallas_call(
        functools.partial(_compress_paged_kernel, n_cmp=n_half - 1),
        grid_spec=pltpu.PrefetchScalarGridSpec(
            num_scalar_prefetch=1, grid=(batch, groups), in_specs=in_specs,
            out_specs=pl.BlockSpec((None, 2, KV_HEADS, CMP_ROWS, HEAD_DIM), lambda b, g, pt: (b, 0, 0, g, 0))),
        out_shape=jax.ShapeDtypeStruct((batch, 2, KV_HEADS, n_half, HEAD_DIM), F32),
        compiler_params=_params(("parallel", "arbitrary"), 48),
    )(page_table, *([view] * (PAGES_PER_STEP + 1)), *wargs)


def _stack_heads(q):
    return jnp.concatenate([q[:, g * HEAD_DIM:(g + 1) * HEAD_DIM] for g in range(GROUP)], axis=0)


def _block_onehot(n_blocks, kpos, shift):
    blk = lax.broadcasted_iota(jnp.int32, (n_blocks, 1), 0)
    return jnp.where((kpos >> shift) == blk, 1.0, 0.0).astype(BF16)


def _nsa_kernel(*refs, tq, qpos0, n_sel, extents, has_tail, win_pos0, win_len):
    if len(extents) > 1:
        refs, osel_ref = refs[:-1], refs[-1]
    if has_tail:
        q_ref, ga_ref, kc_ref, vc_ref, ks_ref, vs_ref, kt_ref, vt_ref, kw_ref, vw_ref, o_ref = refs
    else:
        q_ref, ga_ref, kc_ref, vc_ref, ks_ref, vs_ref, kw_ref, vw_ref, o_ref = refs
    q0 = qpos0 + pl.program_id(2) * tq
    rows = GROUP * tq
    q4 = _stack_heads(q_ref[...]).astype(BF16)
    qpos = q0 + (lax.broadcasted_iota(jnp.int32, (rows, 1), 0) & (tq - 1))

    kc = kc_ref[...].astype(BF16)
    vc = vc_ref[...].astype(BF16)
    n_cp = kc.shape[0]
    cidx = lax.broadcasted_iota(jnp.int32, (1, n_cp), 1)
    s_cmp = _dot_nt(q4, kc) * ATT_SCALE
    (e_cmp,), den = _softmax_parts([s_cmp], [(cidx * NSA_CMP_STRIDE + (NSA_CMP_BLOCK - 1)) <= qpos])
    p_cmp = e_cmp / den
    o_cmp = _dot(p_cmp.astype(BF16), vc)

    n_sp = -(-n_sel // LANE) * LANE
    psum = functools.reduce(lambda a, b: a + b, [p_cmp[g * tq:(g + 1) * tq] for g in range(GROUP)])
    ci = lax.broadcasted_iota(jnp.int32, (n_cp, 1), 0) * NSA_CMP_STRIDE
    sj = lax.broadcasted_iota(jnp.int32, (1, n_sp), 1) * NSA_SEL_BLOCK
    cover = jnp.where((ci <= sj + (NSA_SEL_BLOCK - 1)) & (ci + (NSA_CMP_BLOCK - 1) >= sj), 1.0, 0.0).astype(BF16)
    imp = _dot_f32_lhs(psum, cover)
    col = lax.broadcasted_iota(jnp.int32, (1, n_sp), 1)
    qpos_t = q0 + lax.broadcasted_iota(jnp.int32, (tq, 1), 0)
    back = (qpos_t >> 6) - col
    valid = back >= 0
    forced = (col == 0) | (valid & (back < NSA_N_LOCAL))
    score = jnp.where(valid, jnp.where(forced, FORCE_SCORE, imp), NEG_INF)
    rank = _rank(score, n_sel, col)
    sel = jnp.where((rank < NSA_SEL_TOP) & (score > 0.5 * NEG_INF), 1.0, 0.0).astype(BF16)
    sel4 = jnp.concatenate([sel] * GROUP, axis=0)

    def sel_attend(n_keys):
        s_list, m_list, v_list = [], [], []

        def add_tile(k_bf, v_bf, kpos):
            picked = _dot(sel4, _block_onehot(n_sp, kpos, 6))
            s_list.append(_dot_nt(q4, k_bf) * ATT_SCALE)
            m_list.append(jnp.where(kpos <= qpos, picked, 0.0) > 0.5)
            v_list.append(v_bf)

        kt = min(n_keys, KEY_TILE)
        for t0 in range(0, n_keys, kt):
            add_tile(ks_ref[t0:t0 + kt, :].astype(BF16), vs_ref[t0:t0 + kt, :].astype(BF16),
                     t0 + lax.broadcasted_iota(jnp.int32, (1, kt), 1))
        if has_tail:
            add_tile(kt_ref[...].astype(BF16), vt_ref[...].astype(BF16),
                     n_keys + lax.broadcasted_iota(jnp.int32, (1, TAIL_ROWS), 1))
        e_list, den = _softmax_parts(s_list, m_list, never_empty=True)
        return functools.reduce(lambda a, b: a + b, [_dot(e.astype(BF16), v) for e, v in zip(e_list, v_list)]) / den

    if len(extents) == 1:
        o_sel = sel_attend(extents[0])
    else:
        need = (q0 - qpos0 + tq - 1) // extents[0]
        for idx, ext in enumerate(extents):
            @pl.when(need == idx)
            def _(ext=ext):
                osel_ref[...] = sel_attend(ext)
        o_sel = osel_ref[...]

    start = jnp.clip(q0 - NSA_WINDOW - win_pos0, 0, win_len - WIN_BAND)
    start = pl.multiple_of(start, LANE)
    kw = kw_ref[pl.ds(start, WIN_BAND), :].astype(BF16)
    vw = vw_ref[pl.ds(start, WIN_BAND), :].astype(BF16)
    kposw = win_pos0 + start + lax.broadcasted_iota(jnp.int32, (1, WIN_BAND), 1)
    s_win = _dot_nt(q4, kw) * ATT_SCALE
    m_win = jnp.where(kposw <= qpos, qpos - kposw, NSA_WINDOW) < NSA_WINDOW
    (e_win,), den = _softmax_parts([s_win], [m_win], never_empty=True)
    o_win = _dot(e_win.astype(BF16), vw) / den

    ga = ga_ref[...]
    outs = []
    for g in range(GROUP):
        r = slice(g * tq, (g + 1) * tq)
        outs.append(ga[:, 3 * g:3 * g + 1] * o_cmp[r] + ga[:, 3 * g + 1:3 * g + 2] * o_sel[r]
                    + ga[:, 3 * g + 2:3 * g + 3] * o_win[r])
    o_ref[...] = jnp.concatenate(outs, axis=1).astype(o_ref.dtype)


CAUSAL_STEP = 512


def _causal_extents(tk, tq):
    if tk % CAUSAL_STEP or CAUSAL_STEP % tq:
        return (tk,)
    return tuple(range(CAUSAL_STEP, tk + 1, CAUSAL_STEP))


def _nsa(zq, tq_total, tq, qpos0, cmp_kv, sel_main, sel_cols, sel_tail, win, win_cols, win_pos0, n_total,
         causal_skip):
    batch = cmp_kv.shape[0]
    nq = tq_total // tq
    n_cp = cmp_kv.shape[3]
    tk = sel_main.shape[0] // batch
    lw = win.shape[0] // batch
    n_sel = -(-n_total // NSA_SEL_BLOCK)
    extents = _causal_extents(tk, tq) if causal_skip else (tk,)
    scratch = [pltpu.VMEM((GROUP * tq, HEAD_DIM), F32)] if len(extents) > 1 else []
    qa_blk = SEG["qa"] // (GROUP * HEAD_DIM)
    ga_blk = SEG["ga"] // LANE
    in_specs = [
        pl.BlockSpec((tq, GROUP * HEAD_DIM), lambda b, k, i: (b * nq + i, qa_blk + k)),
        pl.BlockSpec((tq, LANE), lambda b, k, i: (b * nq + i, ga_blk + k)),
        pl.BlockSpec((None, None, None, n_cp, HEAD_DIM), lambda b, k, i: (b, 0, k, 0, 0)),
        pl.BlockSpec((None, None, None, n_cp, HEAD_DIM), lambda b, k, i: (b, 1, k, 0, 0)),
        pl.BlockSpec((tk, HEAD_DIM), lambda b, k, i: (b, sel_cols[0] + k)),
        pl.BlockSpec((tk, HEAD_DIM), lambda b, k, i: (b, sel_cols[1] + k)),
    ]
    args = [zq, zq, cmp_kv, cmp_kv, sel_main, sel_main]
    if sel_tail is not None:
        in_specs += [pl.BlockSpec((TAIL_ROWS, HEAD_DIM), lambda b, k, i: (b, k)),
                     pl.BlockSpec((TAIL_ROWS, HEAD_DIM), lambda b, k, i: (b, KV_HEADS + k))]
        args += [sel_tail, sel_tail]
    in_specs += [pl.BlockSpec((lw, HEAD_DIM), lambda b, k, i: (b, win_cols[0] + k)),
                 pl.BlockSpec((lw, HEAD_DIM), lambda b, k, i: (b, win_cols[1] + k))]
    args += [win, win]
    return pl.pallas_call(
        functools.partial(_nsa_kernel, tq=tq, qpos0=qpos0, n_sel=n_sel, extents=extents,
                          has_tail=sel_tail is not None, win_pos0=win_pos0, win_len=lw),
        grid=(batch, KV_HEADS, nq),
        in_specs=in_specs,
        out_specs=pl.BlockSpec((tq, GROUP * HEAD_DIM), lambda b, k, i: (b * nq + i, k)),
        out_shape=jax.ShapeDtypeStruct((batch * tq_total, WIDTH), BF16),
        scratch_shapes=scratch,
        compiler_params=_params(("parallel", "parallel", "arbitrary"), 56),
    )(*args)


def _moba_kernel(*refs, tq, qpos0, extents, has_tail):
    if has_tail:
        q_ref, k_ref, v_ref, kt_ref, vt_ref, o_ref = refs
    else:
        q_ref, k_ref, v_ref, o_ref = refs
    q0 = qpos0 + pl.program_id(2) * tq
    rows = GROUP * tq
    q4f = _stack_heads(q_ref[...])
    q4 = q4f.astype(BF16)
    qpos = q0 + (lax.broadcasted_iota(jnp.int32, (rows, 1), 0) & (tq - 1))
    cur = qpos >> 8
    col = lax.broadcasted_iota(jnp.int32, (1, LANE), 1)

    def attend(n_keys):
        kt = min(n_keys, KEY_TILE)
        n_blk = n_keys // MOBA_BLOCK
        means = [jnp.sum(k_ref[t0:t0 + kt, :].reshape(kt // MOBA_BLOCK, MOBA_BLOCK, HEAD_DIM), axis=1)
                 * (1.0 / MOBA_BLOCK) for t0 in range(0, n_keys, kt)]
        if n_blk < LANE:
            means.append(jnp.zeros((LANE - n_blk, HEAD_DIM), F32))
        kmean = jnp.concatenate(means, axis=0)
        gate = jnp.where(col < cur, _dot_nt_f32(q4f, kmean), NEG_INF)
        rank = _rank(gate, n_blk, col)
        sel = jnp.where((rank < MOBA_TOP) & (gate > 0.5 * NEG_INF), 1.0, 0.0).astype(BF16)

        s_list, m_list, v_list = [], [], []

        def add_tile(k_bf, v_bf, kpos):
            picked = _dot(sel, _block_onehot(LANE, kpos, 8))
            own = jnp.where((kpos >> 8) == cur, jnp.where(kpos <= qpos, 1.0, 0.0), 0.0)
            s_list.append(_dot_nt(q4, k_bf) * ATT_SCALE)
            m_list.append((picked + own) > 0.5)
            v_list.append(v_bf)

        for t0 in range(0, n_keys, kt):
            add_tile(k_ref[t0:t0 + kt, :].astype(BF16), v_ref[t0:t0 + kt, :].astype(BF16),
                     t0 + lax.broadcasted_iota(jnp.int32, (1, kt), 1))
        if has_tail:
            add_tile(kt_ref[...].astype(BF16), vt_ref[...].astype(BF16),
                     n_keys + lax.broadcasted_iota(jnp.int32, (1, TAIL_ROWS), 1))
        e_list, den = _softmax_parts(s_list, m_list, never_empty=True)
        o = functools.reduce(lambda a, b: a + b, [_dot(e.astype(BF16), v) for e, v in zip(e_list, v_list)]) / den
        o_ref[...] = jnp.concatenate([o[g * tq:(g + 1) * tq] for g in range(GROUP)], axis=1).astype(o_ref.dtype)

    if len(extents) == 1:
        attend(extents[0])
    else:
        need = (q0 - qpos0 + tq - 1) // extents[0]
        for idx, ext in enumerate(extents):
            @pl.when(need == idx)
            def _(ext=ext):
                attend(ext)


def _moba(zq, tq_total, tq, qpos0, batch, main, cols, tail, causal_skip):
    nq = tq_total // tq
    tk = main.shape[0] // batch
    extents = _causal_extents(tk, tq) if causal_skip else (tk,)
    qc_blk = SEG["qc"] // (GROUP * HEAD_DIM)
    in_specs = [pl.BlockSpec((tq, GROUP * HEAD_DIM), lambda b, k, i: (b * nq + i, qc_blk + k)),
                pl.BlockSpec((tk, HEAD_DIM), lambda b, k, i: (b, cols[0] + k)),
                pl.BlockSpec((tk, HEAD_DIM), lambda b, k, i: (b, cols[1] + k))]
    args = [zq, main, main]
    if tail is not None:
        in_specs += [pl.BlockSpec((TAIL_ROWS, HEAD_DIM), lambda b, k, i: (b, k)),
                     pl.BlockSpec((TAIL_ROWS, HEAD_DIM), lambda b, k, i: (b, KV_HEADS + k))]
        args += [tail, tail]
    return pl.pallas_call(
        functools.partial(_moba_kernel, tq=tq, qpos0=qpos0, extents=extents, has_tail=tail is not None),
        grid=(batch, KV_HEADS, nq),
        in_specs=in_specs,
        out_specs=pl.BlockSpec((tq, GROUP * HEAD_DIM), lambda b, k, i: (b * nq + i, k)),
        out_shape=jax.ShapeDtypeStruct((batch * tq_total, WIDTH), BF16),
        compiler_params=_params(("parallel", "parallel", "arbitrary"), 56),
    )(*args)


def _layer_norm(v, g, b, eps=1e-5):
    mu = jnp.mean(v, axis=-1, keepdims=True)
    var = jnp.mean(jnp.square(v - mu), axis=-1, keepdims=True)
    return (v - mu) * lax.rsqrt(var + eps) * g + b


def _gmlp_kernel(u_ref, v_ref, ws_ref, bst_ref, g_ref, b_ref, o_ref):
    vn = _layer_norm(v_ref[...], g_ref[...], b_ref[...]).astype(BF16)
    r = lax.broadcasted_iota(jnp.int32, (GMLP_CHUNK, GMLP_CHUNK), 0)
    c = lax.broadcasted_iota(jnp.int32, (GMLP_CHUNK, GMLP_CHUNK), 1)
    bst = bst_ref[...]
    for g in range(N_HEADS):
        sl = slice(g * HEAD_DIM, (g + 1) * HEAD_DIM)
        wm = jnp.where(c <= r, ws_ref[g], 0.0).astype(BF16)
        sv = _dot(wm, vn[:, sl]) + bst[:, g:g + 1]
        o_ref[:, sl] = (u_ref[:, sl] * sv).astype(o_ref.dtype)


def _gmlp(z, ws, bs, ln_g, ln_b):
    m = z.shape[0]
    ub, vb = SEG["u"] // WIDTH, SEG["v"] // WIDTH
    full = lambda shape: pl.BlockSpec(shape, lambda i: (0,) * len(shape))
    return pl.pallas_call(
        _gmlp_kernel,
        grid=(m // GMLP_CHUNK,),
        in_specs=[pl.BlockSpec((GMLP_CHUNK, WIDTH), lambda i: (i, ub)),
                  pl.BlockSpec((GMLP_CHUNK, WIDTH), lambda i: (i, vb)),
                  full(ws.shape), full((GMLP_CHUNK, N_HEADS)), full((1, WIDTH)), full((1, WIDTH))],
        out_specs=pl.BlockSpec((GMLP_CHUNK, WIDTH), lambda i: (i, 0)),
        out_shape=jax.ShapeDtypeStruct((m, WIDTH), BF16),
        compiler_params=_params(("parallel",), 32),
    )(z, z, ws, bs.T, ln_g.reshape(1, WIDTH), ln_b.reshape(1, WIDTH))


def _hgrn_gates(fd, lb):
    f_gate = lb + (1.0 - lb) * _sigmoid(fd)
    return jnp.log(jnp.maximum(f_gate, TINY)), (1.0 - lb) * _sigmoid(-fd)


def _hgrn_readout(o, gd, g_out):
    return _rms(o, g_out) * _silu(gd)


HGRN_HEADS_PER_STEP = 4


def _hgrn_kernel(q_ref, f_ref, v_ref, gd_ref, lb_ref, go_ref, o_ref, s_ref):
    c = HGRN_CHUNK
    t_total = q_ref.shape[0]
    g_out = go_ref[...]
    ri = lax.broadcasted_iota(jnp.int32, (c, c), 0)
    ci = lax.broadcasted_iota(jnp.int32, (c, c), 1)
    tri = jnp.where(ci <= ri, 1.0, 0.0).astype(BF16)
    lane = lax.broadcasted_iota(jnp.int32, (1, c), 1)
    diag_mask = (ci <= ri) & ((ci >> 3) == (ri >> 3))

    def head_chunk(r0, hh, st):
        hs = slice(hh * HEAD_DIM, (hh + 1) * HEAD_DIM)
        q = q_ref[pl.ds(r0, c), hs]
        v = v_ref[pl.ds(r0, c), hs]
        lf, k = _hgrn_gates(f_ref[pl.ds(r0, c), hs], lb_ref[:, hs])
        cum = _dot_f32_rhs(tri, lf)
        a = jnp.zeros((c, c), F32)
        for bs in (32, 16, 8):
            nb = c // bs
            refq = jnp.concatenate(
                [jnp.zeros((bs, HEAD_DIM), F32)]
                + [jnp.broadcast_to(cum[b * bs - 1:b * bs], (bs, HEAD_DIM)) for b in range(1, nb)], axis=0)
            refk = jnp.concatenate(
                [jnp.broadcast_to(cum[(b + 1) * bs - 1:(b + 1) * bs], (bs, HEAD_DIM)) for b in range(nb)], axis=0)
            qt = (q * jnp.exp(jnp.minimum(cum - refq, 0.0))).astype(BF16)
            kt = (k * jnp.exp(jnp.minimum(refk - cum, 0.0))).astype(BF16)
            sh = bs.bit_length() - 1
            lvl = (((ri >> sh) & 1) == 1) & ((ci >> sh) == (ri >> sh) - 1)
            a = a + jnp.where(lvl, _dot_nt(qt, kt), 0.0)
        rows = []
        for blk in range(c // 8):
            b0 = blk * 8
            qb, kb, cb = q[b0:b0 + 8], k[b0:b0 + 8], cum[b0:b0 + 8]
            acc = jnp.zeros((8, c), F32)
            for s in range(8):
                w = qb * kb[s:s + 1] * jnp.exp(jnp.minimum(cb - cb[s:s + 1], 0.0))
                acc = jnp.where(lane == b0 + s, jnp.sum(w, axis=1, keepdims=True), acc)
            rows.append(acc)
        a = a + jnp.where(diag_mask, jnp.concatenate(rows, axis=0), 0.0)
        o = _dot(a.astype(BF16), v.astype(BF16)) + _dot_nt((q * jnp.exp(cum)).astype(BF16), st.astype(BF16))
        o_ref[pl.ds(r0, c), hs] = _hgrn_readout(o, gd_ref[pl.ds(r0, c), hs], g_out).astype(o_ref.dtype)
        last = cum[c - 1:c]
        kk = (k * jnp.exp(last - cum)).astype(BF16)
        return st * jnp.exp(last) + _dot_tn(v.astype(BF16), kk)

    def chunk(n, sts):
        r0 = pl.multiple_of(n * c, c)
        return tuple(head_chunk(r0, hh, st) for hh, st in enumerate(sts))

    zero = jnp.zeros((HEAD_DIM, HEAD_DIM), F32)
    sts = lax.fori_loop(0, t_total // c, chunk, (zero,) * HGRN_HEADS_PER_STEP, unroll=2)
    for hh, st in enumerate(sts):
        s_ref[hh] = st.T


def _hgrn(z, batch, lb, g_out):
    t = z.shape[0] // batch
    hw = HGRN_HEADS_PER_STEP * HEAD_DIM
    blk = lambda name: SEG[name] // hw
    col = lambda name: pl.BlockSpec((t, hw), lambda b, h: (b, blk(name) + h))
    return pl.pallas_call(
        _hgrn_kernel,
        grid=(batch, N_HEADS // HGRN_HEADS_PER_STEP),
        in_specs=[col("qd"), col("fd"), col("vd"), col("gd"),
                  pl.BlockSpec((1, hw), lambda b, h: (0, h)),
                  pl.BlockSpec((1, HEAD_DIM), lambda b, h: (0, 0))],
        out_specs=[pl.BlockSpec((t, hw), lambda b, h: (b, h)),
                   pl.BlockSpec((None, HGRN_HEADS_PER_STEP, HEAD_DIM, HEAD_DIM), lambda b, h: (b, h, 0, 0))],
        out_shape=[jax.ShapeDtypeStruct((batch * t, WIDTH), BF16),
                   jax.ShapeDtypeStruct((batch, N_HEADS, HEAD_DIM, HEAD_DIM), F32)],
        compiler_params=_params(("parallel", "parallel"), 48),
    )(z, z, z, z, lb.reshape(1, WIDTH), g_out.reshape(1, HEAD_DIM))


def _gmlp_step_kernel(u_ref, v_ref, w_ref, b_ref, g_ref, bb_ref, o_ref, vn_ref):
    vn = _layer_norm(v_ref[...], g_ref[...], bb_ref[...])
    vn_ref[...] = vn
    o_ref[...] = (u_ref[...] * (w_ref[...] * vn + b_ref[...])).astype(o_ref.dtype)


def _gmlp_step(u, v, ws, bs, ln_g, ln_b):
    rows = u.shape[0]
    w_row = jnp.repeat(ws[:, 0, 0], HEAD_DIM).reshape(1, WIDTH)
    b_row = jnp.repeat(bs[:, 0], HEAD_DIM).reshape(1, WIDTH)
    return pl.pallas_call(
        _gmlp_step_kernel,
        out_shape=[jax.ShapeDtypeStruct((rows, WIDTH), BF16), jax.ShapeDtypeStruct((rows, WIDTH), F32)],
    )(u, v, w_row, b_row, ln_g.reshape(1, WIDTH), ln_b.reshape(1, WIDTH))


def _hgrn_step_kernel(qc_ref, fc_ref, lbc_ref, v_ref, gd_ref, go_ref, s0_ref, o_ref, s_ref):
    lf, k = _hgrn_gates(fc_ref[...], lbc_ref[...])
    s_new = jnp.exp(lf) * s0_ref[...] + k * v_ref[...]
    s_ref[...] = s_new
    o = jnp.sum(qc_ref[...] * s_new, axis=0, keepdims=True)
    o_ref[...] = _hgrn_readout(o, gd_ref[...], go_ref[...]).astype(o_ref.dtype)


def _hgrn_step(qd, fd, vd, gd, lb, g_out, state, layer):
    batch = qd.shape[0]
    colv = lambda a: a.reshape(batch, N_HEADS, HEAD_DIM, 1)
    rowv = lambda a: a.reshape(batch, N_HEADS, 1, HEAD_DIM)
    cspec = pl.BlockSpec((None, None, HEAD_DIM, 1), lambda b, h: (b, h, 0, 0))
    rspec = pl.BlockSpec((None, None, 1, HEAD_DIM), lambda b, h: (b, h, 0, 0))
    return pl.pallas_call(
        _hgrn_step_kernel,
        grid=(batch, N_HEADS),
        in_specs=[cspec, cspec, pl.BlockSpec((None, HEAD_DIM, 1), lambda b, h: (h, 0, 0)), rspec, rspec,
                  pl.BlockSpec((1, HEAD_DIM), lambda b, h: (0, 0)),
                  pl.BlockSpec((None, None, None, HEAD_DIM, HEAD_DIM), lambda b, h: (layer, b, h, 0, 0))],
        out_specs=[rspec, pl.BlockSpec((None, None, HEAD_DIM, HEAD_DIM), lambda b, h: (b, h, 0, 0))],
        out_shape=[jax.ShapeDtypeStruct((batch, N_HEADS, 1, HEAD_DIM), BF16),
                   jax.ShapeDtypeStruct((batch, N_HEADS, HEAD_DIM, HEAD_DIM), F32)],
        compiler_params=_params(("parallel", "parallel"), 32),
    )(colv(qd), colv(fd), lb.reshape(N_HEADS, HEAD_DIM, 1), rowv(vd), rowv(gd), g_out.reshape(1, HEAD_DIM), state)


def _gather_kernel(pt_ref, *refs):
    o_ref = refs[-1]
    for r in range(PAGES_PER_STEP):
        for c in range(PAGE_STREAMS):
            o_ref[r * PAGE_SIZE:(r + 1) * PAGE_SIZE, c * HEAD_DIM:(c + 1) * HEAD_DIM] = (
                refs[r][pl.ds(c, PAGE_SIZE, stride=PAGE_STREAMS), :])


def _gather_pages(pool, layer, page_table):
    view = _page_view(pool)
    width = PAGE_STREAMS * HEAD_DIM
    batch, n_pages = page_table.shape
    groups = n_pages // PAGES_PER_STEP
    rows = PAGES_PER_STEP * PAGE_SIZE
    in_specs = [pl.BlockSpec((None, None, PAGE_SIZE * PAGE_STREAMS, HEAD_DIM),
                             lambda b, g, pt, r=r: (layer, pt[b, g * PAGES_PER_STEP + r], 0, 0))
                for r in range(PAGES_PER_STEP)]
    return pl.pallas_call(
        _gather_kernel,
        grid_spec=pltpu.PrefetchScalarGridSpec(
            num_scalar_prefetch=1, grid=(batch, groups), in_specs=in_specs,
            out_specs=pl.BlockSpec((rows, width), lambda b, g, pt: (b * groups + g, 0))),
        out_shape=jax.ShapeDtypeStruct((batch * n_pages * PAGE_SIZE, width), pool.dtype),
        compiler_params=_params(("parallel", "arbitrary"), 40),
    )(page_table, *([view] * PAGES_PER_STEP))


def _ffn(xp, xs, mod_p, mod_s, sub, mod_idx, g, wg, wu, wd, layer, rows_per_group):
    act = _norm_glu(xp, xs, g, mod_p, mod_s, mod_idx, wg, wu, layer, sub, rows_per_group)
    return _down(act, wd, layer, sub, xp, xs, mod_p[:, mod_idx, 2], mod_s[:, mod_idx, 2], 0.5)


def _kv5(a, batch):
    return a.reshape(batch, -1, 2, KV_HEADS, HEAD_DIM)


def _prompt_mix(z, batch, mixw):
    t = z.shape[0] // batch
    seg = lambda name, w: z[:, SEG[name]:SEG[name] + w]
    kvc, kvs, kvw, kvm = seg("kvc", 512), seg("kvs", 512), seg("kvw", 512), seg("kvm", 512)
    cmp_kv = _compress(kvc, batch, mixw["cw1"], mixw["cb1"], mixw["cw2"], mixw["cpe"], mixw["qkg"][1])
    kb = lambda name: SEG[name] // HEAD_DIM
    o_a = _nsa(z, t, 128, 0, cmp_kv, z, (kb("kvs"), kb("kvs") + KV_HEADS), None,
               z, (kb("kvw"), kb("kvw") + KV_HEADS), 0, t, True)
    o_b = _gmlp(z, mixw["ws"], mixw["bs"], mixw["ln_g"], mixw["ln_b"])
    o_c = _moba(z, t, 128, 0, batch, z, (kb("kvm"), kb("kvm") + KV_HEADS), None, True)
    o_d, s_d = _hgrn(z, batch, mixw["lb"], mixw["out_g"])
    mixed = (o_a, o_b, o_c, o_d)
    w = min(NSA_WINDOW, t)
    state = (_kv5(kvc, batch), _kv5(kvs, batch), _kv5(kvm, batch), _kv5(kvw, batch)[:, t - w:], s_d)
    return mixed, state


def _pad_rows(a, batch, rows):
    out = jnp.zeros((batch, rows, a.shape[1]), a.dtype).at[:, 0].set(a)
    return out.reshape(batch * rows, a.shape[1])


def _sample_mix(z_pad, batch, mixw, pool_cmp, pool_sel, pool_moba, win_buf, state, page_table, layer):
    past = page_table.shape[1] * PAGE_SIZE
    z = z_pad[:batch]
    seg = lambda name, w: z[:, SEG[name]:SEG[name] + w]
    kvc, kvs, kvw, kvm = seg("kvc", 512), seg("kvs", 512), seg("kvw", 512), seg("kvm", 512)
    zq = _pad_rows(z, batch, SAMPLE_ROWS)
    dense_sel = _gather_pages(pool_sel, layer, page_table)
    dense_moba = _gather_pages(pool_moba, layer, page_table)
    cmp_kv = _compress_paged(pool_cmp, layer, page_table, mixw["cw1"], mixw["cb1"], mixw["cw2"], mixw["cpe"],
                             mixw["qkg"][1])
    wlen = win_buf.shape[2]
    band = jnp.concatenate([win_buf[layer].reshape(batch, wlen, 512), kvw[:, None, :]], axis=1)
    win = jnp.concatenate([band, jnp.zeros((batch, WIN_BAND - wlen - 1, 512), F32)], axis=1)
    o_a = _nsa(zq, SAMPLE_ROWS, SAMPLE_ROWS, past, cmp_kv, dense_sel, (0, KV_HEADS),
               _pad_rows(kvs, batch, TAIL_ROWS), win.reshape(batch * WIN_BAND, 512), (0, KV_HEADS),
               past - wlen, past + 1, False)
    o_c = _moba(zq, SAMPLE_ROWS, SAMPLE_ROWS, past, batch, dense_moba, (0, KV_HEADS),
                _pad_rows(kvm, batch, TAIL_ROWS), False)
    o_b, v_n = _gmlp_step(seg("u", WIDTH), seg("v", WIDTH), mixw["ws"], mixw["bs"], mixw["ln_g"], mixw["ln_b"])
    o_d, s_d = _hgrn_step(seg("qd", WIDTH), seg("fd", WIDTH), seg("vd", WIDTH), seg("gd", WIDTH),
                          mixw["lb"], mixw["out_g"], state, layer)
    mixed = jnp.concatenate([o_a[::SAMPLE_ROWS], o_b, o_c[::SAMPLE_ROWS], o_d.reshape(batch, WIDTH)], axis=1)
    mixed = jnp.concatenate([mixed, jnp.zeros((z_pad.shape[0] - batch, mixed.shape[1]), mixed.dtype)], axis=0)
    new_state = (_kv5(kvc, batch), _kv5(kvs, batch), _kv5(kvm, batch),
                 band[:, 1:].reshape(batch, wlen, 2, KV_HEADS, HEAD_DIM), s_d, v_n.reshape(batch, 1, WIDTH))
    return mixed, new_state


def kernel(x_prompt, x_sample, c_prompt, c_sample, cache_nsa_cmp_kv, cache_nsa_sel_kv, cache_moba_kv, cache_nsa_win_kv, state_hgrn, page_table, w_ada, b_ada, norm_g, w_ffn_gate, w_ffn_up, w_ffn_down, w_in, w_out, qk_norm_g, nsa_cmp_w1, nsa_cmp_b1, nsa_cmp_w2, nsa_cmp_pos, gmlp_ln_g, gmlp_ln_b, gmlp_ws, gmlp_bs, hgrn_lb_logits, hgrn_out_g):
    bp, t, d = x_prompt.shape
    bs = x_sample.shape[0]
    depth = w_in.shape[0]
    srows = SAMPLE_PAD
    lb_p = jax.nn.softmax(hgrn_lb_logits.astype(F32), axis=0)
    lb_all = jnp.clip(jnp.cumsum(lb_p, axis=0) - lb_p[0:1], 0.0, 1.0)

    c_all = jnp.concatenate([c_prompt, c_sample, jnp.zeros((16 - bp - bs, d), F32)], axis=0)
    mod = _ada_mod(c_all, w_ada, b_ada).reshape(depth, 16, 3, 3, d)

    xp = x_prompt.reshape(bp * t, d)
    xs = jnp.concatenate([x_sample.reshape(bs, d), jnp.zeros((srows - bs, d), F32)], axis=0)
    st_p, st_s = [], []
    for l in range(depth):
        mod_p = mod[l, :bp]
        mod_s = jnp.concatenate([mod[l, bp:bp + bs], jnp.zeros((srows - bs, 3, 3, d), F32)], axis=0)
        w_in_p = _permute_w_in(w_in, l)
        gains = _proj_gains(qk_norm_g[l])
        mixw = dict(qkg=qk_norm_g[l], cw1=nsa_cmp_w1[l], cb1=nsa_cmp_b1[l], cw2=nsa_cmp_w2[l], cpe=nsa_cmp_pos[l],
                    ln_g=gmlp_ln_g[l], ln_b=gmlp_ln_b[l], ws=gmlp_ws[l], bs=gmlp_bs[l], lb=lb_all[l],
                    out_g=hgrn_out_g[l])
        g = norm_g[l]

        xp, xs = _ffn(xp, xs, mod_p, mod_s, 0, 0, g[0], w_ffn_gate, w_ffn_up, w_ffn_down, l, t)
        zp, zs = _proj(xp, xs, g[1], mod_p, mod_s, 1, w_in_p, gains, t)
        mixed_p, sp = _prompt_mix(zp, bp, mixw)
        mixed_s, ss = _sample_mix(zs, bs, mixw, cache_nsa_cmp_kv, cache_nsa_sel_kv, cache_moba_kv,
                                  cache_nsa_win_kv, state_hgrn, page_table, l)
        xp, xs = _out_proj(mixed_p, mixed_s, w_out, l, xp, xs, mod_p[:, 1, 2], mod_s[:, 1, 2], t)
        xp, xs = _ffn(xp, xs, mod_p, mod_s, 1, 2, g[2], w_ffn_gate, w_ffn_up, w_ffn_down, l, t)
        st_p.append(sp)
        st_s.append(ss)

    stack = lambda sts, i: jnp.stack([s[i] for s in sts])
    return (xp.reshape(bp, t, d), xs[:bs].reshape(bs, 1, d),
            stack(st_p, 0), stack(st_p, 1), stack(st_p, 2), stack(st_p, 3), stack(st_p, 4),
            stack(st_s, 0), stack(st_s, 1), stack(st_s, 2), stack(st_s, 3), stack(st_s, 4), stack(st_s, 5))
```

```python
import functools

import jax
import jax.numpy as jnp
from jax import lax
from jax.experimental import pallas as pl
from jax.experimental.pallas import tpu as pltpu

F32 = jnp.float32
BF16 = jnp.bfloat16

HEAD_DIM = 128
N_HEADS = 8
KV_HEADS = 2
GROUP = N_HEADS // KV_HEADS
WIDTH = N_HEADS * HEAD_DIM
PAGE_SIZE = 128
PAGES_PER_STEP = 16

NSA_CMP_STRIDE = 16
NSA_CMP_BLOCK = 32
NSA_CMP_HIDDEN = 256
NSA_SEL_BLOCK = 64
NSA_SEL_TOP = 16
NSA_N_LOCAL = 2
NSA_WINDOW = 512
GMLP_CHUNK = 128
MOBA_BLOCK = 256
MOBA_TOP = 3
HGRN_CHUNK = 64
NEG_INF = -1e30
FORCE_SCORE = 1e4
TINY = 1e-30
ATT_SCALE = HEAD_DIM ** -0.5

LANE = 128
KEY_TILE = 4096
WIN_BAND = NSA_WINDOW + 128
SAMPLE_ROWS = 8
TAIL_ROWS = 256

PROJ_TN = 256
PROJ_TILES_PER_STEP = 2
SEG = {}
_off = 0
for _name, _w in (("qa", 1024), ("u", 1024), ("v", 1024), ("qc", 1024), ("qd", 1024), ("fd", 1024),
                  ("vd", 1024), ("gd", 1024), ("kvc", 512), ("kvs", 512), ("kvw", 512), ("kvm", 512),
                  ("ga", 256), ("pad", 256)):
    SEG[_name] = _off
    _off += _w
N_PROJ = _off

_ORIG = {}
_o = 0
for _name, _w in (("qa", 1024), ("kvc", 512), ("kvs", 512), ("kvw", 512), ("ga", 24), ("u", 1024), ("v", 1024),
                  ("qc", 1024), ("kvm", 512), ("qd", 1024), ("fd", 1024), ("vd", 1024), ("gd", 1024)):
    _ORIG[_name] = (_o, _w)
    _o += _w


def _params(sem, vmem_mb):
    return pltpu.CompilerParams(dimension_semantics=sem, vmem_limit_bytes=vmem_mb << 20)


def _sigmoid(x):
    return 1.0 / (1.0 + jnp.exp(-x))


def _silu(x):
    return x * _sigmoid(x)


def _gelu(x):
    return 0.5 * x * (1.0 + jnp.tanh(0.7978845608028654 * (x + 0.044715 * (x * x * x))))


def _dot(a, b):
    return jnp.dot(a, b, preferred_element_type=F32)


def _dot_nt(a, b):
    return lax.dot_general(a, b, (((1,), (1,)), ((), ())), preferred_element_type=F32)


def _dot_tn(a, b):
    return lax.dot_general(a, b, (((0,), (0,)), ((), ())), preferred_element_type=F32)


def _split3(a):
    hi = a.astype(BF16)
    r1 = a - hi.astype(F32)
    mid = r1.astype(BF16)
    lo = (r1 - mid.astype(F32)).astype(BF16)
    return hi, mid, lo


def _dot_f32_lhs(a, b_exact):
    hi, mid, lo = _split3(a)
    return _dot(hi, b_exact) + _dot(mid, b_exact) + _dot(lo, b_exact)


def _dot_f32_rhs(a_exact, b):
    hi, mid, lo = _split3(b)
    return _dot(a_exact, hi) + _dot(a_exact, mid) + _dot(a_exact, lo)


def _dot_nt_f32(a, b):
    ah, am, _ = _split3(a)
    bh, bm, _ = _split3(b)
    return _dot_nt(ah, bh) + _dot_nt(ah, bm) + _dot_nt(am, bh)


def _rms(x, g, eps=1e-6):
    return x * lax.rsqrt(jnp.mean(x * x, axis=-1, keepdims=True) + eps) * g


def _softmax_parts(s_list, m_list, never_empty=False):
    sm = [jnp.where(m, s, NEG_INF) for s, m in zip(s_list, m_list)]
    mx = functools.reduce(jnp.maximum, [jnp.max(s, axis=-1, keepdims=True) for s in sm])
    if never_empty:
        e = [jnp.exp(s - mx) for s in sm]
    else:
        e = [jnp.where(m, jnp.exp(s - mx), 0.0) for s, m in zip(sm, m_list)]
    den = functools.reduce(lambda a, b: a + b, [jnp.sum(x, axis=-1, keepdims=True) for x in e])
    return e, jnp.maximum(den, 1e-30)


def _rank(score, n, col):
    rank = jnp.zeros(score.shape, jnp.int32)
    for i in range(n):
        si = score[:, i:i + 1]
        before = jnp.where(si > score, 1, jnp.where((si == score) & (col > i), 1, 0))
        rank = rank + before
    return rank


def _ada_kernel(c_ref, w_ref, b_ref, o_ref):
    a = _silu(c_ref[...]).astype(BF16)
    o_ref[...] = _dot(a, w_ref[...].astype(BF16)) + b_ref[...]


def _ada_mod(c, w_ada, b_ada):
    depth, d, n = w_ada.shape
    rows = c.shape[0]
    tn = 512
    return pl.pallas_call(
        _ada_kernel,
        grid=(depth, n // tn),
        in_specs=[pl.BlockSpec((rows, d), lambda l, j: (0, 0)),
                  pl.BlockSpec((None, d, tn), lambda l, j: (l, 0, j)),
                  pl.BlockSpec((None, 1, tn), lambda l, j: (l, 0, j))],
        out_specs=pl.BlockSpec((None, rows, tn), lambda l, j: (l, 0, j)),
        out_shape=jax.ShapeDtypeStruct((depth, rows, n), F32),
        compiler_params=_params(("parallel", "parallel"), 40),
    )(c, w_ada, b_ada.reshape(depth, 1, n))


ROW_TILE = 1024
SAMPLE_PAD = 16
COMBINED_TILE = ROW_TILE + SAMPLE_PAD
COL_TILE = 256
NORM_CHUNK = 128


def _single_buffered(shape, index_map):
    return pl.BlockSpec(shape, index_map, pipeline_mode=pl.Buffered(1))


def _sample_out_spec(tn):
    return pl.BlockSpec((None, SAMPLE_PAD, tn), lambda i, j: (i, 0, j))


def _stage_modulated(lhs_ref, xp_ref, xs_ref, g_ref, scp_ref, shp_ref, scs_ref, shs_ref):
    g = g_ref[...]
    one_sc, sh = 1.0 + scp_ref[...], shp_ref[...]

    def body(c, carry):
        r0 = pl.multiple_of(c * NORM_CHUNK, NORM_CHUNK)
        lhs_ref[pl.ds(r0, NORM_CHUNK), :] = (_rms(xp_ref[pl.ds(r0, NORM_CHUNK), :], g) * one_sc + sh).astype(BF16)
        return carry

    lax.fori_loop(0, ROW_TILE // NORM_CHUNK, body, 0)
    lhs_ref[ROW_TILE:, :] = (_rms(xs_ref[...], g) * (1.0 + scs_ref[...]) + shs_ref[...]).astype(BF16)


def _mod_specs(d, rows_per_group):
    per = rows_per_group // ROW_TILE
    vec_p = pl.BlockSpec((None, 1, d), lambda i, j: (i // per, 0, 0))
    vec_s = pl.BlockSpec((SAMPLE_PAD, d), lambda i, j: (0, 0))
    return [_single_buffered((ROW_TILE, d), lambda i, j: (i, 0)), vec_s,
            pl.BlockSpec((1, d), lambda i, j: (0, 0)), vec_p, vec_p, vec_s, vec_s]


def _mod_args(xp, xs, g, mod_p, mod_s, idx):
    d = xp.shape[1]
    return (xp, xs, g.reshape(1, d), mod_p[:, idx, 1].reshape(-1, 1, d), mod_p[:, idx, 0].reshape(-1, 1, d),
            mod_s[:, idx, 1], mod_s[:, idx, 0])


def _norm_glu_kernel(xp_ref, xs_ref, g_ref, scp_ref, shp_ref, scs_ref, shs_ref, wg_ref, wu_ref, o_ref, lhs_ref):
    @pl.when(pl.program_id(1) == 0)
    def _():
        _stage_modulated(lhs_ref, xp_ref, xs_ref, g_ref, scp_ref, shp_ref, scs_ref, shs_ref)

    h = lhs_ref[...]
    a = _dot(h, wg_ref[...].astype(BF16))
    b = _dot(h, wu_ref[...].astype(BF16))
    o_ref[...] = (_silu(a) * b).astype(o_ref.dtype)


def _norm_glu(xp, xs, g, mod_p, mod_s, idx, wg, wu, layer, sub, rows_per_group):
    m, d = xp.shape
    f = wg.shape[-1]
    nt = m // ROW_TILE
    wspec = pl.BlockSpec((None, None, d, COL_TILE), lambda i, j: (layer, sub, 0, j))
    return pl.pallas_call(
        _norm_glu_kernel,
        grid=(nt, f // COL_TILE),
        in_specs=_mod_specs(d, rows_per_group) + [wspec, wspec],
        out_specs=pl.BlockSpec((COMBINED_TILE, COL_TILE), lambda i, j: (i, j)),
        out_shape=jax.ShapeDtypeStruct((nt * COMBINED_TILE, f), BF16),
        scratch_shapes=[pltpu.VMEM((COMBINED_TILE, d), BF16)],
        compiler_params=_params(("parallel", "arbitrary"), 52),
    )(*_mod_args(xp, xs, g, mod_p, mod_s, idx), wg, wu)


DOWN_TILES = 2


def _down_kernel(x_ref, w_ref, rp_ref, rs_ref, gp_ref, gs_ref, op_ref, os_ref, *, coef):
    acc = _dot(x_ref[...], w_ref[...].astype(BF16))
    gp = coef * gp_ref[...]
    for r in range(DOWN_TILES):
        rows = slice(r * ROW_TILE, (r + 1) * ROW_TILE)
        op_ref[rows, :] = rp_ref[rows, :] + gp * acc[r * COMBINED_TILE:r * COMBINED_TILE + ROW_TILE]
    os_ref[...] = rs_ref[...] + (coef * gs_ref[...]) * acc[ROW_TILE:COMBINED_TILE]


def _down(act, w, layer, sub, xp, xs, gate_p, gate_s, coef):
    m, n = xp.shape
    kdim = act.shape[1]
    tk = kdim // 2
    rows = DOWN_TILES * ROW_TILE
    assert rows == m // gate_p.shape[0]
    for ks in range(2):
        xp, xs = pl.pallas_call(
            functools.partial(_down_kernel, coef=coef),
            grid=(m // rows, n // COL_TILE),
            in_specs=[_single_buffered((DOWN_TILES * COMBINED_TILE, tk), lambda i, j, ks=ks: (i, ks)),
                      pl.BlockSpec((None, None, tk, COL_TILE), lambda i, j, ks=ks: (layer, sub, ks, j)),
                      pl.BlockSpec((rows, COL_TILE), lambda i, j: (i, j)),
                      pl.BlockSpec((SAMPLE_PAD, COL_TILE), lambda i, j: (0, j)),
                      pl.BlockSpec((None, 1, COL_TILE), lambda i, j: (i, 0, j)),
                      pl.BlockSpec((SAMPLE_PAD, COL_TILE), lambda i, j: (0, j))],
            out_specs=[pl.BlockSpec((rows, COL_TILE), lambda i, j: (i, j)), _sample_out_spec(COL_TILE)],
            out_shape=[jax.ShapeDtypeStruct((m, n), F32), jax.ShapeDtypeStruct((m // rows, SAMPLE_PAD, n), F32)],
            compiler_params=_params(("parallel", "arbitrary"), 56),
        )(act, w, xp, xs, gate_p.reshape(-1, 1, n), gate_s)
        xs = xs[0]
    return xp, xs


def _out_proj_kernel(*refs, n_parts):
    x_refs = refs[:n_parts]
    xs_ref, w_ref, rp_ref, rs_ref, gp_ref, gs_ref, op_ref, os_ref, lhs_ref = refs[n_parts:]

    @pl.when(pl.program_id(1) == 0)
    def _():
        off = 0
        for x_ref in x_refs:
            lhs_ref[:ROW_TILE, off:off + x_ref.shape[1]] = x_ref[...]
            off += x_ref.shape[1]
        lhs_ref[ROW_TILE:, :] = xs_ref[...]

    acc = _dot(lhs_ref[...], w_ref[...].astype(BF16))
    op_ref[...] = rp_ref[...] + gp_ref[...] * acc[:ROW_TILE]
    os_ref[...] = rs_ref[...] + gs_ref[...] * acc[ROW_TILE:]


def _out_proj(parts, mixed_s, w, layer, xp, xs, gate_p, gate_s, rows_per_group):
    m, n = xp.shape
    kdim = w.shape[1]
    per = rows_per_group // ROW_TILE
    in_specs = [pl.BlockSpec((ROW_TILE, x.shape[1]), lambda i, j: (i, 0)) for x in parts]
    in_specs += [pl.BlockSpec((SAMPLE_PAD, kdim), lambda i, j: (0, 0)),
                 pl.BlockSpec((None, kdim, COL_TILE), lambda i, j: (layer, 0, j)),
                 pl.BlockSpec((ROW_TILE, COL_TILE), lambda i, j: (i, j)),
                 pl.BlockSpec((SAMPLE_PAD, COL_TILE), lambda i, j: (0, j)),
                 pl.BlockSpec((None, 1, COL_TILE), lambda i, j: (i // per, 0, j)),
                 pl.BlockSpec((SAMPLE_PAD, COL_TILE), lambda i, j: (0, j))]
    xp, xs = pl.pallas_call(
        functools.partial(_out_proj_kernel, n_parts=len(parts)),
        grid=(m // ROW_TILE, n // COL_TILE),
        in_specs=in_specs,
        out_specs=[pl.BlockSpec((ROW_TILE, COL_TILE), lambda i, j: (i, j)), _sample_out_spec(COL_TILE)],
        out_shape=[jax.ShapeDtypeStruct((m, n), F32), jax.ShapeDtypeStruct((m // ROW_TILE, SAMPLE_PAD, n), F32)],
        scratch_shapes=[pltpu.VMEM((COMBINED_TILE, kdim), BF16)],
        compiler_params=_params(("parallel", "arbitrary"), 48),
    )(*parts, mixed_s, w, xp, xs, gate_p.reshape(-1, 1, n), gate_s)
    return xp, xs[0]


def _tiles(name, width):
    a = SEG[name] // PROJ_TN
    return a, a + width // PROJ_TN


_NORM_TILES = (_tiles("qa", 1024), _tiles("qc", 1024), _tiles("kvs", 256), _tiles("kvw", 256), _tiles("kvm", 256))
_GELU_TILES = (_tiles("u", 2048),)
_SILU_TILES = (_tiles("qd", 1024),)
_SIGM_TILES = (_tiles("ga", 256),)


def _in_ranges(j, ranges):
    return functools.reduce(jnp.logical_or, [(j >= a) & (j < b) for a, b in ranges])


def _proj_kernel(xp_ref, xs_ref, g_ref, scp_ref, shp_ref, scs_ref, shs_ref, w_ref, gain_ref, zp_ref, zs_ref, lhs_ref):
    j = pl.program_id(1)

    @pl.when(j == 0)
    def _():
        _stage_modulated(lhs_ref, xp_ref, xs_ref, g_ref, scp_ref, shp_ref, scs_ref, shs_ref)

    lhs = lhs_ref[...]
    accs = [_dot_nt(lhs, w_ref[c * PROJ_TN:(c + 1) * PROJ_TN, :]) for c in range(PROJ_TILES_PER_STEP)]
    for c, acc in enumerate(accs):
        _proj_epilogue(j * PROJ_TILES_PER_STEP + c, acc, gain_ref, zp_ref, zs_ref,
                       slice(c * PROJ_TN, (c + 1) * PROJ_TN))


def _proj_epilogue(tile, acc, gain_ref, zp_ref, zs_ref, cols):
    is_norm = _in_ranges(tile, _NORM_TILES)
    is_gelu = _in_ranges(tile, _GELU_TILES)
    is_silu = _in_ranges(tile, _SILU_TILES)
    is_sigm = _in_ranges(tile, _SIGM_TILES)

    def store(val):
        zp_ref[:, cols] = val[:ROW_TILE]
        zs_ref[:, cols] = val[ROW_TILE:]

    @pl.when(is_norm)
    def _():
        gain = gain_ref[:, cols]
        heads = [slice(hh * HEAD_DIM, (hh + 1) * HEAD_DIM) for hh in range(PROJ_TN // HEAD_DIM)]
        store(jnp.concatenate([_rms(acc[:, sl], gain[:, sl]) for sl in heads], axis=1))

    @pl.when(is_gelu)
    def _():
        store(_gelu(acc))

    @pl.when(is_silu)
    def _():
        store(_silu(acc) * ATT_SCALE)

    @pl.when(is_sigm)
    def _():
        store(_sigmoid(acc))

    @pl.when(jnp.logical_not(is_norm | is_gelu | is_silu | is_sigm))
    def _():
        store(acc)


def _proj(xp, xs, g, mod_p, mod_s, idx, w_in_p, gains, rows_per_group):
    m, d = xp.shape
    tn = PROJ_TILES_PER_STEP * PROJ_TN
    zp, zs = pl.pallas_call(
        _proj_kernel,
        grid=(m // ROW_TILE, N_PROJ // tn),
        in_specs=_mod_specs(d, rows_per_group) + [pl.BlockSpec((tn, d), lambda i, j: (j, 0)),
                                                   pl.BlockSpec((None, 1, tn), lambda i, j: (j, 0, 0))],
        out_specs=[pl.BlockSpec((ROW_TILE, tn), lambda i, j: (i, j)), _sample_out_spec(tn)],
        out_shape=[jax.ShapeDtypeStruct((m, N_PROJ), F32),
                   jax.ShapeDtypeStruct((m // ROW_TILE, SAMPLE_PAD, N_PROJ), F32)],
        scratch_shapes=[pltpu.VMEM((COMBINED_TILE, d), BF16)],
        compiler_params=_params(("parallel", "arbitrary"), 48),
    )(*_mod_args(xp, xs, g, mod_p, mod_s, idx), w_in_p, gains)
    return zp, zs[0]


GATES_PER_KV = 3 * GROUP


def _permute_w_in(w_in, layer):
    w_t = jnp.swapaxes(w_in, 1, 2)[layer]
    d = w_t.shape[1]

    def rows(name):
        a, w = _ORIG[name]
        return w_t[a:a + w]

    ga = rows("ga")
    pad = jnp.zeros((LANE - GATES_PER_KV, d), w_t.dtype)
    parts = [rows(n) for n in ("qa", "u", "v", "qc", "qd", "fd", "vd", "gd", "kvc", "kvs", "kvw", "kvm")]
    parts += [ga[:GATES_PER_KV], pad, ga[GATES_PER_KV:], pad, jnp.zeros((N_PROJ - SEG["pad"], d), w_t.dtype)]
    return jnp.concatenate(parts, axis=0).astype(BF16)


def _proj_gains(qkg):
    g = jnp.ones((N_PROJ // PROJ_TN, PROJ_TN), F32)
    two = lambda v: jnp.tile(v, PROJ_TN // HEAD_DIM)
    for name, idx, ntile in (("qa", 0, 4), ("qc", 4, 4), ("kvs", 2, 1), ("kvw", 3, 1), ("kvm", 5, 1)):
        t0 = SEG[name] // PROJ_TN
        g = g.at[t0:t0 + ntile].set(two(qkg[idx])[None, :])
    return g.reshape(-1, 1, PROJ_TILES_PER_STEP * PROJ_TN)


CMP_ROWS = 128


def _compress_body(x_of, w1_ref, b1_ref, w2_ref, pe_ref, gk_ref, o_ref, n_cmp):
    i = pl.program_id(1)
    row = lax.broadcasted_iota(jnp.int32, (CMP_ROWS, 1), 0)
    valid = (i * CMP_ROWS + row) < n_cmp
    half = NSA_CMP_STRIDE * HEAD_DIM
    for s in range(2):
        w1a = w1_ref[s, 0]
        w1b = w1_ref[s, 1]
        pe = pe_ref[s]
        bias = (_dot(pe[:, :half], w1a) + _dot(pe[:, half:], w1b))[0:1] + b1_ref[s]
        w2 = w2_ref[s]
        for k in range(KV_HEADS):
            x, xn = x_of(s * KV_HEADS + k)
            x = x.astype(BF16)
            h1 = _dot(x, w1a)
            h2 = _dot(x, w1b)
            h2n = _dot(xn.astype(BF16), w1b)
            h2s = jnp.where(row == CMP_ROWS - 1, h2n[0:1], pltpu.roll(h2, CMP_ROWS - 1, 0))
            hid = _gelu(h1 + h2s + bias)
            out = _dot(hid.astype(BF16), w2)
            if s == 0:
                out = _rms(out, gk_ref[...])
            o_ref[s, k] = jnp.where(valid, out, 0.0)


def _compress_kernel(r_ref, rn_ref, w1_ref, b1_ref, w2_ref, pe_ref, gk_ref, o_ref, *, n_cmp):
    row_w = 2 * KV_HEADS * HEAD_DIM

    def x_of(c):
        cols = [slice(p * row_w + c * HEAD_DIM, p * row_w + (c + 1) * HEAD_DIM) for p in range(NSA_CMP_STRIDE)]
        return (jnp.concatenate([r_ref[:, sl] for sl in cols], axis=1),
                jnp.concatenate([rn_ref[:, sl] for sl in cols], axis=1))

    _compress_body(x_of, w1_ref, b1_ref, w2_ref, pe_ref, gk_ref, o_ref, n_cmp)


PAGE_HALVES = PAGE_SIZE // NSA_CMP_STRIDE
PAGE_STREAMS = 2 * KV_HEADS


def _compress_paged_kernel(pt_ref, *refs, n_cmp):
    pages = refs[:PAGES_PER_STEP]
    nxt = refs[PAGES_PER_STEP]
    w1_ref, b1_ref, w2_ref, pe_ref, gk_ref, o_ref = refs[PAGES_PER_STEP + 1:]

    def flat(pg, c):
        return jnp.concatenate([pg[pl.ds(p * PAGE_STREAMS + c, PAGE_HALVES, stride=NSA_CMP_STRIDE * PAGE_STREAMS), :]
                                for p in range(NSA_CMP_STRIDE)], axis=1)

    def x_of(c):
        return jnp.concatenate([flat(pg, c) for pg in pages], axis=0), flat(nxt, c)

    _compress_body(x_of, w1_ref, b1_ref, w2_ref, pe_ref, gk_ref, o_ref, n_cmp)


def _compress_weights(w1, b1, w2, pe, gk):
    half = NSA_CMP_STRIDE * HEAD_DIM
    w1r = w1.reshape(2, 2, half, NSA_CMP_HIDDEN)
    pe8 = jnp.broadcast_to(pe.reshape(2, 1, 2 * half), (2, 8, 2 * half))
    args = (w1r.astype(BF16), b1.reshape(2, 1, NSA_CMP_HIDDEN), w2.astype(BF16), pe8.astype(BF16),
            gk.reshape(1, HEAD_DIM))
    return args, [a.shape for a in args]


def _compress(kv_flat, batch, w1, b1, w2, pe, gk):
    n_half = kv_flat.shape[0] // batch // NSA_CMP_STRIDE
    nblk = n_half // CMP_ROWS
    width = NSA_CMP_STRIDE * kv_flat.shape[1]
    r = kv_flat.reshape(batch * n_half, width)
    last8 = batch * n_half // 8 - 1
    wargs, wshapes = _compress_weights(w1, b1, w2, pe, gk)
    full = lambda shape: pl.BlockSpec(shape, lambda b, i: (0,) * len(shape))
    return pl.pallas_call(
        functools.partial(_compress_kernel, n_cmp=n_half - 1),
        grid=(batch, nblk),
        in_specs=[pl.BlockSpec((CMP_ROWS, width), lambda b, i: (b * nblk + i, 0)),
                  pl.BlockSpec((8, width), lambda b, i: (jnp.minimum((b * nblk + i + 1) * (CMP_ROWS // 8), last8), 0))]
                 + [full(s) for s in wshapes],
        out_specs=pl.BlockSpec((None, 2, KV_HEADS, CMP_ROWS, HEAD_DIM), lambda b, i: (b, 0, 0, i, 0)),
        out_shape=jax.ShapeDtypeStruct((batch, 2, KV_HEADS, n_half, HEAD_DIM), F32),
        compiler_params=_params(("parallel", "arbitrary"), 48),
    )(r, r, *wargs)


def _page_view(pool):
    depth, n_phys = pool.shape[:2]
    return pool.reshape(depth, n_phys, PAGE_SIZE * PAGE_STREAMS, HEAD_DIM)


def _compress_paged(pool, layer, page_table, w1, b1, w2, pe, gk):
    view = _page_view(pool)
    batch, n_pages = page_table.shape
    groups = n_pages // PAGES_PER_STEP
    n_half = n_pages * PAGE_HALVES
    wargs, wshapes = _compress_weights(w1, b1, w2, pe, gk)
    page = lambda fn: pl.BlockSpec((None, None, PAGE_SIZE * PAGE_STREAMS, HEAD_DIM), fn)
    in_specs = [page(lambda b, g, pt, r=r: (layer, pt[b, g * PAGES_PER_STEP + r], 0, 0)) for r in range(PAGES_PER_STEP)]
    in_specs.append(page(lambda b, g, pt: (layer, pt[b, jnp.minimum((g + 1) * PAGES_PER_STEP, n_pages - 1)], 0, 0)))
    in_specs += [pl.BlockSpec(s, lambda b, g, pt, n=len(s): (0,) * n) for s in wshapes]
    return pl.pallas_call(
        functools.partial(_compress_paged_kernel, n_cmp=n_half - 1),
        grid_spec=pltpu.PrefetchScalarGridSpec(
            num_scalar_prefetch=1, grid=(batch, groups), in_specs=in_specs,
            out_specs=pl.BlockSpec((None, 2, KV_HEADS, CMP_ROWS, HEAD_DIM), lambda b, g, pt: (b, 0, 0, g, 0))),
        out_shape=jax.ShapeDtypeStruct((batch, 2, KV_HEADS, n_half, HEAD_DIM), F32),
        compiler_params=_params(("parallel", "arbitrary"), 48),
    )(page_table, *([view] * (PAGES_PER_STEP + 1)), *wargs)


def _stack_heads(q):
    return jnp.concatenate([q[:, g * HEAD_DIM:(g + 1) * HEAD_DIM] for g in range(GROUP)], axis=0)


def _block_onehot(n_blocks, kpos, shift):
    blk = lax.broadcasted_iota(jnp.int32, (n_blocks, 1), 0)
    return jnp.where((kpos >> shift) == blk, 1.0, 0.0).astype(BF16)


def _nsa_kernel(*refs, tq, qpos0, n_sel, extents, has_tail, win_pos0, win_len):
    if len(extents) > 1:
        refs, osel_ref = refs[:-1], refs[-1]
    if has_tail:
        q_ref, ga_ref, kc_ref, vc_ref, ks_ref, vs_ref, kt_ref, vt_ref, kw_ref, vw_ref, o_ref = refs
    else:
        q_ref, ga_ref, kc_ref, vc_ref, ks_ref, vs_ref, kw_ref, vw_ref, o_ref = refs
    q0 = qpos0 + pl.program_id(2) * tq
    rows = GROUP * tq
    q4 = _stack_heads(q_ref[...]).astype(BF16)
    qpos = q0 + (lax.broadcasted_iota(jnp.int32, (rows, 1), 0) & (tq - 1))

    kc = kc_ref[...].astype(BF16)
    vc = vc_ref[...].astype(BF16)
    n_cp = kc.shape[0]
    cidx = lax.broadcasted_iota(jnp.int32, (1, n_cp), 1)
    s_cmp = _dot_nt(q4, kc) * ATT_SCALE
    (e_cmp,), den = _softmax_parts([s_cmp], [(cidx * NSA_CMP_STRIDE + (NSA_CMP_BLOCK - 1)) <= qpos])
    p_cmp = e_cmp / den
    o_cmp = _dot(p_cmp.astype(BF16), vc)

    n_sp = -(-n_sel // LANE) * LANE
    psum = functools.reduce(lambda a, b: a + b, [p_cmp[g * tq:(g + 1) * tq] for g in range(GROUP)])
    ci = lax.broadcasted_iota(jnp.int32, (n_cp, 1), 0) * NSA_CMP_STRIDE
    sj = lax.broadcasted_iota(jnp.int32, (1, n_sp), 1) * NSA_SEL_BLOCK
    cover = jnp.where((ci <= sj + (NSA_SEL_BLOCK - 1)) & (ci + (NSA_CMP_BLOCK - 1) >= sj), 1.0, 0.0).astype(BF16)
    imp = _dot_f32_lhs(psum, cover)
    col = lax.broadcasted_iota(jnp.int32, (1, n_sp), 1)
    qpos_t = q0 + lax.broadcasted_iota(jnp.int32, (tq, 1), 0)
    back = (qpos_t >> 6) - col
    valid = back >= 0
    forced = (col == 0) | (valid & (back < NSA_N_LOCAL))
    score = jnp.where(valid, jnp.where(forced, FORCE_SCORE, imp), NEG_INF)
    rank = _rank(score, n_sel, col)
    sel = jnp.where((rank < NSA_SEL_TOP) & (score > 0.5 * NEG_INF), 1.0, 0.0).astype(BF16)
    sel4 = jnp.concatenate([sel] * GROUP, axis=0)

    def sel_attend(n_keys):
        s_list, m_list, v_list = [], [], []

        def add_tile(k_bf, v_bf, kpos):
            picked = _dot(sel4, _block_onehot(n_sp, kpos, 6))
            s_list.append(_dot_nt(q4, k_bf) * ATT_SCALE)
            m_list.append(jnp.where(kpos <= qpos, picked, 0.0) > 0.5)
            v_list.append(v_bf)

        kt = min(n_keys, KEY_TILE)
        for t0 in range(0, n_keys, kt):
            add_tile(ks_ref[t0:t0 + kt, :].astype(BF16), vs_ref[t0:t0 + kt, :].astype(BF16),
                     t0 + lax.broadcasted_iota(jnp.int32, (1, kt), 1))
        if has_tail:
            add_tile(kt_ref[...].astype(BF16), vt_ref[...].astype(BF16),
                     n_keys + lax.broadcasted_iota(jnp.int32, (1, TAIL_ROWS), 1))
        e_list, den = _softmax_parts(s_list, m_list, never_empty=True)
        return functools.reduce(lambda a, b: a + b, [_dot(e.astype(BF16), v) for e, v in zip(e_list, v_list)]) / den

    if len(extents) == 1:
        o_sel = sel_attend(extents[0])
    else:
        need = (q0 - qpos0 + tq - 1) // extents[0]
        for idx, ext in enumerate(extents):
            @pl.when(need == idx)
            def _(ext=ext):
                osel_ref[...] = sel_attend(ext)
        o_sel = osel_ref[...]

    start = jnp.clip(q0 - NSA_WINDOW - win_pos0, 0, win_len - WIN_BAND)
    start = pl.multiple_of(start, LANE)
    kw = kw_ref[pl.ds(start, WIN_BAND), :].astype(BF16)
    vw = vw_ref[pl.ds(start, WIN_BAND), :].astype(BF16)
    kposw = win_pos0 + start + lax.broadcasted_iota(jnp.int32, (1, WIN_BAND), 1)
    s_win = _dot_nt(q4, kw) * ATT_SCALE
    m_win = jnp.where(kposw <= qpos, qpos - kposw, NSA_WINDOW) < NSA_WINDOW
    (e_win,), den = _softmax_parts([s_win], [m_win], never_empty=True)
    o_win = _dot(e_win.astype(BF16), vw) / den

    ga = ga_ref[...]
    outs = []
    for g in range(GROUP):
        r = slice(g * tq, (g + 1) * tq)
        outs.append(ga[:, 3 * g:3 * g + 1] * o_cmp[r] + ga[:, 3 * g + 1:3 * g + 2] * o_sel[r]
                    + ga[:, 3 * g + 2:3 * g + 3] * o_win[r])
    o_ref[...] = jnp.concatenate(outs, axis=1).astype(o_ref.dtype)


CAUSAL_STEP = 512


def _causal_extents(tk, tq):
    if tk % CAUSAL_STEP or CAUSAL_STEP % tq:
        return (tk,)
    return tuple(range(CAUSAL_STEP, tk + 1, CAUSAL_STEP))


def _nsa(zq, tq_total, tq, qpos0, cmp_kv, sel_main, sel_cols, sel_tail, win, win_cols, win_pos0, n_total,
         causal_skip):
    batch = cmp_kv.shape[0]
    nq = tq_total // tq
    n_cp = cmp_kv.shape[3]
    tk = sel_main.shape[0] // batch
    lw = win.shape[0] // batch
    n_sel = -(-n_total // NSA_SEL_BLOCK)
    extents = _causal_extents(tk, tq) if causal_skip else (tk,)
    scratch = [pltpu.VMEM((GROUP * tq, HEAD_DIM), F32)] if len(extents) > 1 else []
    qa_blk = SEG["qa"] // (GROUP * HEAD_DIM)
    ga_blk = SEG["ga"] // LANE
    in_specs = [
        pl.BlockSpec((tq, GROUP * HEAD_DIM), lambda b, k, i: (b * nq + i, qa_blk + k)),
        pl.BlockSpec((tq, LANE), lambda b, k, i: (b * nq + i, ga_blk + k)),
        pl.BlockSpec((None, None, None, n_cp, HEAD_DIM), lambda b, k, i: (b, 0, k, 0, 0)),
        pl.BlockSpec((None, None, None, n_cp, HEAD_DIM), lambda b, k, i: (b, 1, k, 0, 0)),
        pl.BlockSpec((tk, HEAD_DIM), lambda b, k, i: (b, sel_cols[0] + k)),
        pl.BlockSpec((tk, HEAD_DIM), lambda b, k, i: (b, sel_cols[1] + k)),
    ]
    args = [zq, zq, cmp_kv, cmp_kv, sel_main, sel_main]
    if sel_tail is not None:
        in_specs += [pl.BlockSpec((TAIL_ROWS, HEAD_DIM), lambda b, k, i: (b, k)),
                     pl.BlockSpec((TAIL_ROWS, HEAD_DIM), lambda b, k, i: (b, KV_HEADS + k))]
        args += [sel_tail, sel_tail]
    in_specs += [pl.BlockSpec((lw, HEAD_DIM), lambda b, k, i: (b, win_cols[0] + k)),
                 pl.BlockSpec((lw, HEAD_DIM), lambda b, k, i: (b, win_cols[1] + k))]
    args += [win, win]
    return pl.pallas_call(
        functools.partial(_nsa_kernel, tq=tq, qpos0=qpos0, n_sel=n_sel, extents=extents,
                          has_tail=sel_tail is not None, win_pos0=win_pos0, win_len=lw),
        grid=(batch, KV_HEADS, nq),
        in_specs=in_specs,
        out_specs=pl.BlockSpec((tq, GROUP * HEAD_DIM), lambda b, k, i: (b * nq + i, k)),
        out_shape=jax.ShapeDtypeStruct((batch * tq_total, WIDTH), BF16),
        scratch_shapes=scratch,
        compiler_params=_params(("parallel", "parallel", "arbitrary"), 56),
    )(*args)


def _moba_kernel(*refs, tq, qpos0, extents, has_tail):
    if has_tail:
        q_ref, k_ref, v_ref, kt_ref, vt_ref, o_ref = refs
    else:
        q_ref, k_ref, v_ref, o_ref = refs
    q0 = qpos0 + pl.program_id(2) * tq
    rows = GROUP * tq
    q4f = _stack_heads(q_ref[...])
    q4 = q4f.astype(BF16)
    qpos = q0 + (lax.broadcasted_iota(jnp.int32, (rows, 1), 0) & (tq - 1))
    cur = qpos >> 8
    col = lax.broadcasted_iota(jnp.int32, (1, LANE), 1)

    def attend(n_keys):
        kt = min(n_keys, KEY_TILE)
        n_blk = n_keys // MOBA_BLOCK
        means = [jnp.sum(k_ref[t0:t0 + kt, :].reshape(kt // MOBA_BLOCK, MOBA_BLOCK, HEAD_DIM), axis=1)
                 * (1.0 / MOBA_BLOCK) for t0 in range(0, n_keys, kt)]
        if n_blk < LANE:
            means.append(jnp.zeros((LANE - n_blk, HEAD_DIM), F32))
        kmean = jnp.concatenate(means, axis=0)
        gate = jnp.where(col < cur, _dot_nt_f32(q4f, kmean), NEG_INF)
        rank = _rank(gate, n_blk, col)
        sel = jnp.where((rank < MOBA_TOP) & (gate > 0.5 * NEG_INF), 1.0, 0.0).astype(BF16)

        s_list, m_list, v_list = [], [], []

        def add_tile(k_bf, v_bf, kpos):
            picked = _dot(sel, _block_onehot(LANE, kpos, 8))
            own = jnp.where((kpos >> 8) == cur, jnp.where(kpos <= qpos, 1.0, 0.0), 0.0)
            s_list.append(_dot_nt(q4, k_bf) * ATT_SCALE)
            m_list.append((picked + own) > 0.5)
            v_list.append(v_bf)

        for t0 in range(0, n_keys, kt):
            add_tile(k_ref[t0:t0 + kt, :].astype(BF16), v_ref[t0:t0 + kt, :].astype(BF16),
                     t0 + lax.broadcasted_iota(jnp.int32, (1, kt), 1))
        if has_tail:
            add_tile(kt_ref[...].astype(BF16), vt_ref[...].astype(BF16),
                     n_keys + lax.broadcasted_iota(jnp.int32, (1, TAIL_ROWS), 1))
        e_list, den = _softmax_parts(s_list, m_list, never_empty=True)
        o = functools.reduce(lambda a, b: a + b, [_dot(e.astype(BF16), v) for e, v in zip(e_list, v_list)]) / den
        o_ref[...] = jnp.concatenate([o[g * tq:(g + 1) * tq] for g in range(GROUP)], axis=1).astype(o_ref.dtype)

    if len(extents) == 1:
        attend(extents[0])
    else:
        need = (q0 - qpos0 + tq - 1) // extents[0]
        for idx, ext in enumerate(extents):
            @pl.when(need == idx)
            def _(ext=ext):
                attend(ext)


def _moba(zq, tq_total, tq, qpos0, batch, main, cols, tail, causal_skip):
    nq = tq_total // tq
    tk = main.shape[0] // batch
    extents = _causal_extents(tk, tq) if causal_skip else (tk,)
    qc_blk = SEG["qc"] // (GROUP * HEAD_DIM)
    in_specs = [pl.BlockSpec((tq, GROUP * HEAD_DIM), lambda b, k, i: (b * nq + i, qc_blk + k)),
                pl.BlockSpec((tk, HEAD_DIM), lambda b, k, i: (b, cols[0] + k)),
                pl.BlockSpec((tk, HEAD_DIM), lambda b, k, i: (b, cols[1] + k))]
    args = [zq, main, main]
    if tail is not None:
        in_specs += [pl.BlockSpec((TAIL_ROWS, HEAD_DIM), lambda b, k, i: (b, k)),
                     pl.BlockSpec((TAIL_ROWS, HEAD_DIM), lambda b, k, i: (b, KV_HEADS + k))]
        args += [tail, tail]
    return pl.pallas_call(
        functools.partial(_moba_kernel, tq=tq, qpos0=qpos0, extents=extents, has_tail=tail is not None),
        grid=(batch, KV_HEADS, nq),
        in_specs=in_specs,
        out_specs=pl.BlockSpec((tq, GROUP * HEAD_DIM), lambda b, k, i: (b * nq + i, k)),
        out_shape=jax.ShapeDtypeStruct((batch * tq_total, WIDTH), BF16),
        compiler_params=_params(("parallel", "parallel", "arbitrary"), 56),
    )(*args)


def _layer_norm(v, g, b, eps=1e-5):
    mu = jnp.mean(v, axis=-1, keepdims=True)
    var = jnp.mean(jnp.square(v - mu), axis=-1, keepdims=True)
    return (v - mu) * lax.rsqrt(var + eps) * g + b


def _gmlp_kernel(u_ref, v_ref, ws_ref, bst_ref, g_ref, b_ref, o_ref):
    vn = _layer_norm(v_ref[...], g_ref[...], b_ref[...]).astype(BF16)
    r = lax.broadcasted_iota(jnp.int32, (GMLP_CHUNK, GMLP_CHUNK), 0)
    c = lax.broadcasted_iota(jnp.int32, (GMLP_CHUNK, GMLP_CHUNK), 1)
    bst = bst_ref[...]
    for g in range(N_HEADS):
        sl = slice(g * HEAD_DIM, (g + 1) * HEAD_DIM)
        wm = jnp.where(c <= r, ws_ref[g], 0.0).astype(BF16)
        sv = _dot(wm, vn[:, sl]) + bst[:, g:g + 1]
        o_ref[:, sl] = (u_ref[:, sl] * sv).astype(o_ref.dtype)


def _gmlp(z, ws, bs, ln_g, ln_b):
    m = z.shape[0]
    ub, vb = SEG["u"] // WIDTH, SEG["v"] // WIDTH
    full = lambda shape: pl.BlockSpec(shape, lambda i: (0,) * len(shape))
    return pl.pallas_call(
        _gmlp_kernel,
        grid=(m // GMLP_CHUNK,),
        in_specs=[pl.BlockSpec((GMLP_CHUNK, WIDTH), lambda i: (i, ub)),
                  pl.BlockSpec((GMLP_CHUNK, WIDTH), lambda i: (i, vb)),
                  full(ws.shape), full((GMLP_CHUNK, N_HEADS)), full((1, WIDTH)), full((1, WIDTH))],
        out_specs=pl.BlockSpec((GMLP_CHUNK, WIDTH), lambda i: (i, 0)),
        out_shape=jax.ShapeDtypeStruct((m, WIDTH), BF16),
        compiler_params=_params(("parallel",), 32),
    )(z, z, ws, bs.T, ln_g.reshape(1, WIDTH), ln_b.reshape(1, WIDTH))


def _hgrn_gates(fd, lb):
    f_gate = lb + (1.0 - lb) * _sigmoid(fd)
    return jnp.log(jnp.maximum(f_gate, TINY)), (1.0 - lb) * _sigmoid(-fd)


def _hgrn_readout(o, gd, g_out):
    return _rms(o, g_out) * _silu(gd)


HGRN_HEADS_PER_STEP = 4


def _hgrn_kernel(q_ref, f_ref, v_ref, gd_ref, lb_ref, go_ref, o_ref, s_ref):
    c = HGRN_CHUNK
    t_total = q_ref.shape[0]
    g_out = go_ref[...]
    ri = lax.broadcasted_iota(jnp.int32, (c, c), 0)
    ci = lax.broadcasted_iota(jnp.int32, (c, c), 1)
    tri = jnp.where(ci <= ri, 1.0, 0.0).astype(BF16)
    lane = lax.broadcasted_iota(jnp.int32, (1, c), 1)
    diag_mask = (ci <= ri) & ((ci >> 3) == (ri >> 3))

    def head_chunk(r0, hh, st):
        hs = slice(hh * HEAD_DIM, (hh + 1) * HEAD_DIM)
        q = q_ref[pl.ds(r0, c), hs]
        v = v_ref[pl.ds(r0, c), hs]
        lf, k = _hgrn_gates(f_ref[pl.ds(r0, c), hs], lb_ref[:, hs])
        cum = _dot_f32_rhs(tri, lf)
        a = jnp.zeros((c, c), F32)
        for bs in (32, 16, 8):
            nb = c // bs
            refq = jnp.concatenate(
                [jnp.zeros((bs, HEAD_DIM), F32)]
                + [jnp.broadcast_to(cum[b * bs - 1:b * bs], (bs, HEAD_DIM)) for b in range(1, nb)], axis=0)
            refk = jnp.concatenate(
                [jnp.broadcast_to(cum[(b + 1) * bs - 1:(b + 1) * bs], (bs, HEAD_DIM)) for b in range(nb)], axis=0)
            qt = (q * jnp.exp(jnp.minimum(cum - refq, 0.0))).astype(BF16)
            kt = (k * jnp.exp(jnp.minimum(refk - cum, 0.0))).astype(BF16)
            sh = bs.bit_length() - 1
            lvl = (((ri >> sh) & 1) == 1) & ((ci >> sh) == (ri >> sh) - 1)
            a = a + jnp.where(lvl, _dot_nt(qt, kt), 0.0)
        rows = []
        for blk in range(c // 8):
            b0 = blk * 8
            qb, kb, cb = q[b0:b0 + 8], k[b0:b0 + 8], cum[b0:b0 + 8]
            acc = jnp.zeros((8, c), F32)
            for s in range(8):
                w = qb * kb[s:s + 1] * jnp.exp(jnp.minimum(cb - cb[s:s + 1], 0.0))
                acc = jnp.where(lane == b0 + s, jnp.sum(w, axis=1, keepdims=True), acc)
            rows.append(acc)
        a = a + jnp.where(diag_mask, jnp.concatenate(rows, axis=0), 0.0)
        o = _dot(a.astype(BF16), v.astype(BF16)) + _dot_nt((q * jnp.exp(cum)).astype(BF16), st.astype(BF16))
        o_ref[pl.ds(r0, c), hs] = _hgrn_readout(o, gd_ref[pl.ds(r0, c), hs], g_out).astype(o_ref.dtype)
        last = cum[c - 1:c]
        kk = (k * jnp.exp(last - cum)).astype(BF16)
        return st * jnp.exp(last) + _dot_tn(v.astype(BF16), kk)

    def chunk(n, sts):
        r0 = pl.multiple_of(n * c, c)
        return tuple(head_chunk(r0, hh, st) for hh, st in enumerate(sts))

    zero = jnp.zeros((HEAD_DIM, HEAD_DIM), F32)
    sts = lax.fori_loop(0, t_total // c, chunk, (zero,) * HGRN_HEADS_PER_STEP, unroll=2)
    for hh, st in enumerate(sts):
        s_ref[hh] = st.T


def _hgrn(z, batch, lb, g_out):
    t = z.shape[0] // batch
    hw = HGRN_HEADS_PER_STEP * HEAD_DIM
    blk = lambda name: SEG[name] // hw
    col = lambda name: pl.BlockSpec((t, hw), lambda b, h: (b, blk(name) + h))
    return pl.pallas_call(
        _hgrn_kernel,
        grid=(batch, N_HEADS // HGRN_HEADS_PER_STEP),
        in_specs=[col("qd"), col("fd"), col("vd"), col("gd"),
                  pl.BlockSpec((1, hw), lambda b, h: (0, h)),
                  pl.BlockSpec((1, HEAD_DIM), lambda b, h: (0, 0))],
        out_specs=[pl.BlockSpec((t, hw), lambda b, h: (b, h)),
                   pl.BlockSpec((None, HGRN_HEADS_PER_STEP, HEAD_DIM, HEAD_DIM), lambda b, h: (b, h, 0, 0))],
        out_shape=[jax.ShapeDtypeStruct((batch * t, WIDTH), BF16),
                   jax.ShapeDtypeStruct((batch, N_HEADS, HEAD_DIM, HEAD_DIM), F32)],
        compiler_params=_params(("parallel", "parallel"), 48),
    )(z, z, z, z, lb.reshape(1, WIDTH), g_out.reshape(1, HEAD_DIM))


def _gmlp_step_kernel(u_ref, v_ref, w_ref, b_ref, g_ref, bb_ref, o_ref, vn_ref):
    vn = _layer_norm(v_ref[...], g_ref[...], bb_ref[...])
    vn_ref[...] = vn
    o_ref[...] = (u_ref[...] * (w_ref[...] * vn + b_ref[...])).astype(o_ref.dtype)


def _gmlp_step(u, v, ws, bs, ln_g, ln_b):
    rows = u.shape[0]
    w_row = jnp.repeat(ws[:, 0, 0], HEAD_DIM).reshape(1, WIDTH)
    b_row = jnp.repeat(bs[:, 0], HEAD_DIM).reshape(1, WIDTH)
    return pl.pallas_call(
        _gmlp_step_kernel,
        out_shape=[jax.ShapeDtypeStruct((rows, WIDTH), BF16), jax.ShapeDtypeStruct((rows, WIDTH), F32)],
    )(u, v, w_row, b_row, ln_g.reshape(1, WIDTH), ln_b.reshape(1, WIDTH))


def _hgrn_step_kernel(qc_ref, fc_ref, lbc_ref, v_ref, gd_ref, go_ref, s0_ref, o_ref, s_ref):
    lf, k = _hgrn_gates(fc_ref[...], lbc_ref[...])
    s_new = jnp.exp(lf) * s0_ref[...] + k * v_ref[...]
    s_ref[...] = s_new
    o = jnp.sum(qc_ref[...] * s_new, axis=0, keepdims=True)
    o_ref[...] = _hgrn_readout(o, gd_ref[...], go_ref[...]).astype(o_ref.dtype)


def _hgrn_step(qd, fd, vd, gd, lb, g_out, state, layer):
    batch = qd.shape[0]
    colv = lambda a: a.reshape(batch, N_HEADS, HEAD_DIM, 1)
    rowv = lambda a: a.reshape(batch, N_HEADS, 1, HEAD_DIM)
    cspec = pl.BlockSpec((None, None, HEAD_DIM, 1), lambda b, h: (b, h, 0, 0))
    rspec = pl.BlockSpec((None, None, 1, HEAD_DIM), lambda b, h: (b, h, 0, 0))
    return pl.pallas_call(
        _hgrn_step_kernel,
        grid=(batch, N_HEADS),
        in_specs=[cspec, cspec, pl.BlockSpec((None, HEAD_DIM, 1), lambda b, h: (h, 0, 0)), rspec, rspec,
                  pl.BlockSpec((1, HEAD_DIM), lambda b, h: (0, 0)),
                  pl.BlockSpec((None, None, None, HEAD_DIM, HEAD_DIM), lambda b, h: (layer, b, h, 0, 0))],
        out_specs=[rspec, pl.BlockSpec((None, None, HEAD_DIM, HEAD_DIM), lambda b, h: (b, h, 0, 0))],
        out_shape=[jax.ShapeDtypeStruct((batch, N_HEADS, 1, HEAD_DIM), BF16),
                   jax.ShapeDtypeStruct((batch, N_HEADS, HEAD_DIM, HEAD_DIM), F32)],
        compiler_params=_params(("parallel", "parallel"), 32),
    )(colv(qd), colv(fd), lb.reshape(N_HEADS, HEAD_DIM, 1), rowv(vd), rowv(gd), g_out.reshape(1, HEAD_DIM), state)


def _gather_kernel(pt_ref, *refs):
    o_ref = refs[-1]
    for r in range(PAGES_PER_STEP):
        for c in range(PAGE_STREAMS):
            o_ref[r * PAGE_SIZE:(r + 1) * PAGE_SIZE, c * HEAD_DIM:(c + 1) * HEAD_DIM] = (
                refs[r][pl.ds(c, PAGE_SIZE, stride=PAGE_STREAMS), :])


def _gather_pages(pool, layer, page_table):
    view = _page_view(pool)
    width = PAGE_STREAMS * HEAD_DIM
    batch, n_pages = page_table.shape
    groups = n_pages // PAGES_PER_STEP
    rows = PAGES_PER_STEP * PAGE_SIZE
    in_specs = [pl.BlockSpec((None, None, PAGE_SIZE * PAGE_STREAMS, HEAD_DIM),
                             lambda b, g, pt, r=r: (layer, pt[b, g * PAGES_PER_STEP + r], 0, 0))
                for r in range(PAGES_PER_STEP)]
    return pl.pallas_call(
        _gather_kernel,
        grid_spec=pltpu.PrefetchScalarGridSpec(
            num_scalar_prefetch=1, grid=(batch, groups), in_specs=in_specs,
            out_specs=pl.BlockSpec((rows, width), lambda b, g, pt: (b * groups + g, 0))),
        out_shape=jax.ShapeDtypeStruct((batch * n_pages * PAGE_SIZE, width), pool.dtype),
        compiler_params=_params(("parallel", "arbitrary"), 40),
    )(page_table, *([view] * PAGES_PER_STEP))


def _ffn(xp, xs, mod_p, mod_s, sub, mod_idx, g, wg, wu, wd, layer, rows_per_group):
    act = _norm_glu(xp, xs, g, mod_p, mod_s, mod_idx, wg, wu, layer, sub, rows_per_group)
    return _down(act, wd, layer, sub, xp, xs, mod_p[:, mod_idx, 2], mod_s[:, mod_idx, 2], 0.5)


def _kv5(a, batch):
    return a.reshape(batch, -1, 2, KV_HEADS, HEAD_DIM)


def _cache_rows_kernel(*refs):
    z_refs, o_ref = refs[:-1], refs[-1]
    rows = z_refs[0].shape[0]
    for li, z_ref in enumerate(z_refs):
        @pl.when(pl.program_id(0) == li)
        def _(z_ref=z_ref):
            for c in range(PAGE_STREAMS):
                o_ref[pl.ds(c, rows, stride=PAGE_STREAMS), :] = z_ref[:, c * HEAD_DIM:(c + 1) * HEAD_DIM]


def _cache_rows(z_layers, name, batch, keep):
    depth = len(z_layers)
    t = z_layers[0].shape[0] // batch
    rows = min(keep, ROW_TILE)
    per_batch, first = keep // rows, (t - keep) // rows
    n_steps = batch * per_batch
    width = PAGE_STREAMS * HEAD_DIM
    blk = SEG[name] // width

    def z_spec(li):
        def index(l, s):
            s_eff = jnp.where(l == li, s, jnp.where(l < li, 0, n_steps - 1))
            return ((s_eff // per_batch) * (t // rows) + first + s_eff % per_batch, blk)
        return pl.BlockSpec((rows, width), index)

    out = pl.pallas_call(
        _cache_rows_kernel,
        grid=(depth, n_steps),
        in_specs=[z_spec(li) for li in range(depth)],
        out_specs=pl.BlockSpec((None, rows * PAGE_STREAMS, HEAD_DIM), lambda l, s: (l, s, 0)),
        out_shape=jax.ShapeDtypeStruct((depth, batch * keep * PAGE_STREAMS, HEAD_DIM), F32),
        compiler_params=_params(("arbitrary", "arbitrary"), 32),
    )(*z_layers)
    return out.reshape(depth, batch, keep, 2, KV_HEADS, HEAD_DIM)


def _prompt_mix(z, batch, mixw):
    t = z.shape[0] // batch
    kvc = z[:, SEG["kvc"]:SEG["kvc"] + PAGE_STREAMS * HEAD_DIM]
    cmp_kv = _compress(kvc, batch, mixw["cw1"], mixw["cb1"], mixw["cw2"], mixw["cpe"], mixw["qkg"][1])
    kb = lambda name: SEG[name] // HEAD_DIM
    o_a = _nsa(z, t, 128, 0, cmp_kv, z, (kb("kvs"), kb("kvs") + KV_HEADS), None,
               z, (kb("kvw"), kb("kvw") + KV_HEADS), 0, t, True)
    o_b = _gmlp(z, mixw["ws"], mixw["bs"], mixw["ln_g"], mixw["ln_b"])
    o_c = _moba(z, t, 128, 0, batch, z, (kb("kvm"), kb("kvm") + KV_HEADS), None, True)
    o_d, s_d = _hgrn(z, batch, mixw["lb"], mixw["out_g"])
    return (o_a, o_b, o_c, o_d), s_d


def _pad_rows(a, batch, rows):
    out = jnp.zeros((batch, rows, a.shape[1]), a.dtype).at[:, 0].set(a)
    return out.reshape(batch * rows, a.shape[1])


def _sample_mix(z_pad, batch, mixw, pool_cmp, pool_sel, pool_moba, win_buf, state, page_table, layer):
    past = page_table.shape[1] * PAGE_SIZE
    z = z_pad[:batch]
    seg = lambda name, w: z[:, SEG[name]:SEG[name] + w]
    kvc, kvs, kvw, kvm = seg("kvc", 512), seg("kvs", 512), seg("kvw", 512), seg("kvm", 512)
    zq = _pad_rows(z, batch, SAMPLE_ROWS)
    dense_sel = _gather_pages(pool_sel, layer, page_table)
    dense_moba = _gather_pages(pool_moba, layer, page_table)
    cmp_kv = _compress_paged(pool_cmp, layer, page_table, mixw["cw1"], mixw["cb1"], mixw["cw2"], mixw["cpe"],
                             mixw["qkg"][1])
    wlen = win_buf.shape[2]
    band = jnp.concatenate([win_buf[layer].reshape(batch, wlen, 512), kvw[:, None, :]], axis=1)
    win = jnp.concatenate([band, jnp.zeros((batch, WIN_BAND - wlen - 1, 512), F32)], axis=1)
    o_a = _nsa(zq, SAMPLE_ROWS, SAMPLE_ROWS, past, cmp_kv, dense_sel, (0, KV_HEADS),
               _pad_rows(kvs, batch, TAIL_ROWS), win.reshape(batch * WIN_BAND, 512), (0, KV_HEADS),
               past - wlen, past + 1, False)
    o_c = _moba(zq, SAMPLE_ROWS, SAMPLE_ROWS, past, batch, dense_moba, (0, KV_HEADS),
                _pad_rows(kvm, batch, TAIL_ROWS), False)
    o_b, v_n = _gmlp_step(seg("u", WIDTH), seg("v", WIDTH), mixw["ws"], mixw["bs"], mixw["ln_g"], mixw["ln_b"])
    o_d, s_d = _hgrn_step(seg("qd", WIDTH), seg("fd", WIDTH), seg("vd", WIDTH), seg("gd", WIDTH),
                          mixw["lb"], mixw["out_g"], state, layer)
    mixed = jnp.concatenate([o_a[::SAMPLE_ROWS], o_b, o_c[::SAMPLE_ROWS], o_d.reshape(batch, WIDTH)], axis=1)
    mixed = jnp.concatenate([mixed, jnp.zeros((z_pad.shape[0] - batch, mixed.shape[1]), mixed.dtype)], axis=0)
    new_state = (_kv5(kvc, batch), _kv5(kvs, batch), _kv5(kvm, batch),
                 band[:, 1:].reshape(batch, wlen, 2, KV_HEADS, HEAD_DIM), s_d, v_n.reshape(batch, 1, WIDTH))
    return mixed, new_state


def kernel(x_prompt, x_sample, c_prompt, c_sample, cache_nsa_cmp_kv, cache_nsa_sel_kv, cache_moba_kv, cache_nsa_win_kv, state_hgrn, page_table, w_ada, b_ada, norm_g, w_ffn_gate, w_ffn_up, w_ffn_down, w_in, w_out, qk_norm_g, nsa_cmp_w1, nsa_cmp_b1, nsa_cmp_w2, nsa_cmp_pos, gmlp_ln_g, gmlp_ln_b, gmlp_ws, gmlp_bs, hgrn_lb_logits, hgrn_out_g):
    bp, t, d = x_prompt.shape
    bs = x_sample.shape[0]
    depth = w_in.shape[0]
    srows = SAMPLE_PAD
    lb_p = jax.nn.softmax(hgrn_lb_logits.astype(F32), axis=0)
    lb_all = jnp.clip(jnp.cumsum(lb_p, axis=0) - lb_p[0:1], 0.0, 1.0)

    c_all = jnp.concatenate([c_prompt, c_sample, jnp.zeros((16 - bp - bs, d), F32)], axis=0)
    mod = _ada_mod(c_all, w_ada, b_ada).reshape(depth, 16, 3, 3, d)

    xp = x_prompt.reshape(bp * t, d)
    xs = jnp.concatenate([x_sample.reshape(bs, d), jnp.zeros((srows - bs, d), F32)], axis=0)
    z_layers, hgrn_p, st_s = [], [], []
    for l in range(depth):
        mod_p = mod[l, :bp]
        mod_s = jnp.concatenate([mod[l, bp:bp + bs], jnp.zeros((srows - bs, 3, 3, d), F32)], axis=0)
        w_in_p = _permute_w_in(w_in, l)
        gains = _proj_gains(qk_norm_g[l])
        mixw = dict(qkg=qk_norm_g[l], cw1=nsa_cmp_w1[l], cb1=nsa_cmp_b1[l], cw2=nsa_cmp_w2[l], cpe=nsa_cmp_pos[l],
                    ln_g=gmlp_ln_g[l], ln_b=gmlp_ln_b[l], ws=gmlp_ws[l], bs=gmlp_bs[l], lb=lb_all[l],
                    out_g=hgrn_out_g[l])
        g = norm_g[l]

        xp, xs = _ffn(xp, xs, mod_p, mod_s, 0, 0, g[0], w_ffn_gate, w_ffn_up, w_ffn_down, l, t)
        zp, zs = _proj(xp, xs, g[1], mod_p, mod_s, 1, w_in_p, gains, t)
        mixed_p, sp = _prompt_mix(zp, bp, mixw)
        mixed_s, ss = _sample_mix(zs, bs, mixw, cache_nsa_cmp_kv, cache_nsa_sel_kv, cache_moba_kv,
                                  cache_nsa_win_kv, state_hgrn, page_table, l)
        xp, xs = _out_proj(mixed_p, mixed_s, w_out, l, xp, xs, mod_p[:, 1, 2], mod_s[:, 1, 2], t)
        xp, xs = _ffn(xp, xs, mod_p, mod_s, 1, 2, g[2], w_ffn_gate, w_ffn_up, w_ffn_down, l, t)
        z_layers.append(zp)
        hgrn_p.append(sp)
        st_s.append(ss)

    stack = lambda sts, i: jnp.stack([s[i] for s in sts])
    return (xp.reshape(bp, t, d), xs[:bs].reshape(bs, 1, d),
            _cache_rows(z_layers, "kvc", bp, t), _cache_rows(z_layers, "kvs", bp, t),
            _cache_rows(z_layers, "kvm", bp, t), _cache_rows(z_layers, "kvw", bp, min(NSA_WINDOW, t)),
            jnp.stack(hgrn_p),
            stack(st_s, 0), stack(st_s, 1), stack(st_s, 2), stack(st_s, 3), stack(st_s, 4), stack(st_s, 5))
```

```python
import functools

import jax
import jax.numpy as jnp
from jax import lax
from jax.experimental import pallas as pl
from jax.experimental.pallas import tpu as pltpu

F32 = jnp.float32
BF16 = jnp.bfloat16

HEAD_DIM = 128
N_HEADS = 8
KV_HEADS = 2
GROUP = N_HEADS // KV_HEADS
WIDTH = N_HEADS * HEAD_DIM
PAGE_SIZE = 128
PAGES_PER_STEP = 16

NSA_CMP_STRIDE = 16
NSA_CMP_BLOCK = 32
NSA_CMP_HIDDEN = 256
NSA_SEL_BLOCK = 64
NSA_SEL_TOP = 16
NSA_N_LOCAL = 2
NSA_WINDOW = 512
GMLP_CHUNK = 128
MOBA_BLOCK = 256
MOBA_TOP = 3
HGRN_CHUNK = 64
NEG_INF = -1e30
FORCE_SCORE = 1e4
TINY = 1e-30
ATT_SCALE = HEAD_DIM ** -0.5

LANE = 128
KEY_TILE = 4096
WIN_BAND = NSA_WINDOW + 128
SAMPLE_ROWS = 8
TAIL_ROWS = 256

PROJ_TN = 256
PROJ_TILES_PER_STEP = 2
SEG = {}
_off = 0
for _name, _w in (("qa", 1024), ("u", 1024), ("v", 1024), ("qc", 1024), ("qd", 1024), ("fd", 1024),
                  ("vd", 1024), ("gd", 1024), ("kvc", 512), ("kvs", 512), ("kvw", 512), ("kvm", 512),
                  ("ga", 256), ("pad", 256)):
    SEG[_name] = _off
    _off += _w
N_PROJ = _off

_ORIG = {}
_o = 0
for _name, _w in (("qa", 1024), ("kvc", 512), ("kvs", 512), ("kvw", 512), ("ga", 24), ("u", 1024), ("v", 1024),
                  ("qc", 1024), ("kvm", 512), ("qd", 1024), ("fd", 1024), ("vd", 1024), ("gd", 1024)):
    _ORIG[_name] = (_o, _w)
    _o += _w


def _params(sem, vmem_mb):
    return pltpu.CompilerParams(dimension_semantics=sem, vmem_limit_bytes=vmem_mb << 20)


def _sigmoid(x):
    return 1.0 / (1.0 + jnp.exp(-x))


def _silu(x):
    return x * _sigmoid(x)


def _gelu(x):
    return 0.5 * x * (1.0 + jnp.tanh(0.7978845608028654 * (x + 0.044715 * (x * x * x))))


def _dot(a, b):
    return jnp.dot(a, b, preferred_element_type=F32)


def _dot_nt(a, b):
    return lax.dot_general(a, b, (((1,), (1,)), ((), ())), preferred_element_type=F32)


def _dot_tn(a, b):
    return lax.dot_general(a, b, (((0,), (0,)), ((), ())), preferred_element_type=F32)


def _split3(a):
    hi = a.astype(BF16)
    r1 = a - hi.astype(F32)
    mid = r1.astype(BF16)
    lo = (r1 - mid.astype(F32)).astype(BF16)
    return hi, mid, lo


def _dot_f32_lhs(a, b_exact):
    hi, mid, lo = _split3(a)
    return _dot(hi, b_exact) + _dot(mid, b_exact) + _dot(lo, b_exact)


def _dot_f32_rhs(a_exact, b):
    hi, mid, lo = _split3(b)
    return _dot(a_exact, hi) + _dot(a_exact, mid) + _dot(a_exact, lo)


def _dot_nt_f32(a, b):
    ah, am, _ = _split3(a)
    bh, bm, _ = _split3(b)
    return _dot_nt(ah, bh) + _dot_nt(ah, bm) + _dot_nt(am, bh)


def _rms(x, g, eps=1e-6):
    return x * lax.rsqrt(jnp.mean(x * x, axis=-1, keepdims=True) + eps) * g


def _softmax_parts(s_list, m_list, never_empty=False):
    sm = [jnp.where(m, s, NEG_INF) for s, m in zip(s_list, m_list)]
    mx = functools.reduce(jnp.maximum, [jnp.max(s, axis=-1, keepdims=True) for s in sm])
    if never_empty:
        e = [jnp.exp(s - mx) for s in sm]
    else:
        e = [jnp.where(m, jnp.exp(s - mx), 0.0) for s, m in zip(sm, m_list)]
    den = functools.reduce(lambda a, b: a + b, [jnp.sum(x, axis=-1, keepdims=True) for x in e])
    return e, jnp.maximum(den, 1e-30)


def _rank(score, n, col):
    rank = jnp.zeros(score.shape, jnp.int32)
    for i in range(n):
        si = score[:, i:i + 1]
        before = jnp.where(si > score, 1, jnp.where((si == score) & (col > i), 1, 0))
        rank = rank + before
    return rank


def _ada_kernel(c_ref, w_ref, b_ref, o_ref):
    a = _silu(c_ref[...]).astype(BF16)
    o_ref[...] = _dot(a, w_ref[...].astype(BF16)) + b_ref[...]


def _ada_mod(c, w_ada, b_ada):
    depth, d, n = w_ada.shape
    rows = c.shape[0]
    tn = 512
    return pl.pallas_call(
        _ada_kernel,
        grid=(depth, n // tn),
        in_specs=[pl.BlockSpec((rows, d), lambda l, j: (0, 0)),
                  pl.BlockSpec((None, d, tn), lambda l, j: (l, 0, j)),
                  pl.BlockSpec((None, 1, tn), lambda l, j: (l, 0, j))],
        out_specs=pl.BlockSpec((None, rows, tn), lambda l, j: (l, 0, j)),
        out_shape=jax.ShapeDtypeStruct((depth, rows, n), F32),
        compiler_params=_params(("parallel", "parallel"), 40),
    )(c, w_ada, b_ada.reshape(depth, 1, n))


ROW_TILE = 1024
SAMPLE_PAD = 16
COMBINED_TILE = ROW_TILE + SAMPLE_PAD
COL_TILE = 256
NORM_CHUNK = 128


def _single_buffered(shape, index_map):
    return pl.BlockSpec(shape, index_map, pipeline_mode=pl.Buffered(1))


def _sample_out_spec(tn):
    return pl.BlockSpec((None, SAMPLE_PAD, tn), lambda i, j, *_: (i, 0, j))


def _stage_modulated(lhs_ref, xp_ref, xs_ref, g_ref, scp_ref, shp_ref, scs_ref, shs_ref):
    g = g_ref[...]
    one_sc, sh = 1.0 + scp_ref[...], shp_ref[...]

    def body(c, carry):
        r0 = pl.multiple_of(c * NORM_CHUNK, NORM_CHUNK)
        lhs_ref[pl.ds(r0, NORM_CHUNK), :] = (_rms(xp_ref[pl.ds(r0, NORM_CHUNK), :], g) * one_sc + sh).astype(BF16)
        return carry

    lax.fori_loop(0, ROW_TILE // NORM_CHUNK, body, 0)
    lhs_ref[ROW_TILE:, :] = (_rms(xs_ref[...], g) * (1.0 + scs_ref[...]) + shs_ref[...]).astype(BF16)


def _mod_specs(d, rows_per_group):
    per = rows_per_group // ROW_TILE
    vec_p = pl.BlockSpec((None, 1, d), lambda i, j, *_: (i // per, 0, 0))
    vec_s = pl.BlockSpec((SAMPLE_PAD, d), lambda i, j, *_: (0, 0))
    return [_single_buffered((ROW_TILE, d), lambda i, j, *_: (i, 0)), vec_s,
            pl.BlockSpec((1, d), lambda i, j, *_: (0, 0)), vec_p, vec_p, vec_s, vec_s]


def _mod_args(xp, xs, g, mod_p, mod_s, idx):
    d = xp.shape[1]
    return (xp, xs, g.reshape(1, d), mod_p[:, idx, 1].reshape(-1, 1, d), mod_p[:, idx, 0].reshape(-1, 1, d),
            mod_s[:, idx, 1], mod_s[:, idx, 0])


def _norm_glu_kernel(xp_ref, xs_ref, g_ref, scp_ref, shp_ref, scs_ref, shs_ref, wg_ref, wu_ref, o_ref, lhs_ref):
    @pl.when(pl.program_id(1) == 0)
    def _():
        _stage_modulated(lhs_ref, xp_ref, xs_ref, g_ref, scp_ref, shp_ref, scs_ref, shs_ref)

    h = lhs_ref[...]
    a = _dot(h, wg_ref[...].astype(BF16))
    b = _dot(h, wu_ref[...].astype(BF16))
    o_ref[...] = (_silu(a) * b).astype(o_ref.dtype)


def _norm_glu(xp, xs, g, mod_p, mod_s, idx, wg, wu, layer, sub, rows_per_group):
    m, d = xp.shape
    f = wg.shape[-1]
    nt = m // ROW_TILE
    wspec = pl.BlockSpec((None, None, d, COL_TILE), lambda i, j: (layer, sub, 0, j))
    return pl.pallas_call(
        _norm_glu_kernel,
        grid=(nt, f // COL_TILE),
        in_specs=_mod_specs(d, rows_per_group) + [wspec, wspec],
        out_specs=pl.BlockSpec((COMBINED_TILE, COL_TILE), lambda i, j: (i, j)),
        out_shape=jax.ShapeDtypeStruct((nt * COMBINED_TILE, f), BF16),
        scratch_shapes=[pltpu.VMEM((COMBINED_TILE, d), BF16)],
        compiler_params=_params(("parallel", "arbitrary"), 52),
    )(*_mod_args(xp, xs, g, mod_p, mod_s, idx), wg, wu)


DOWN_TILES = 2


def _down_kernel(x_ref, w_ref, rp_ref, rs_ref, gp_ref, gs_ref, op_ref, os_ref, *, coef):
    acc = _dot(x_ref[...], w_ref[...].astype(BF16))
    gp = coef * gp_ref[...]
    for r in range(DOWN_TILES):
        rows = slice(r * ROW_TILE, (r + 1) * ROW_TILE)
        op_ref[rows, :] = rp_ref[rows, :] + gp * acc[r * COMBINED_TILE:r * COMBINED_TILE + ROW_TILE]
    os_ref[...] = rs_ref[...] + (coef * gs_ref[...]) * acc[ROW_TILE:COMBINED_TILE]


def _down(act, w, layer, sub, xp, xs, gate_p, gate_s, coef):
    m, n = xp.shape
    kdim = act.shape[1]
    tk = kdim // 2
    rows = DOWN_TILES * ROW_TILE
    assert rows == m // gate_p.shape[0]
    for ks in range(2):
        xp, xs = pl.pallas_call(
            functools.partial(_down_kernel, coef=coef),
            grid=(m // rows, n // COL_TILE),
            in_specs=[_single_buffered((DOWN_TILES * COMBINED_TILE, tk), lambda i, j, ks=ks: (i, ks)),
                      pl.BlockSpec((None, None, tk, COL_TILE), lambda i, j, ks=ks: (layer, sub, ks, j)),
                      pl.BlockSpec((rows, COL_TILE), lambda i, j: (i, j)),
                      pl.BlockSpec((SAMPLE_PAD, COL_TILE), lambda i, j: (0, j)),
                      pl.BlockSpec((None, 1, COL_TILE), lambda i, j: (i, 0, j)),
                      pl.BlockSpec((SAMPLE_PAD, COL_TILE), lambda i, j: (0, j))],
            out_specs=[pl.BlockSpec((rows, COL_TILE), lambda i, j: (i, j)), _sample_out_spec(COL_TILE)],
            out_shape=[jax.ShapeDtypeStruct((m, n), F32), jax.ShapeDtypeStruct((m // rows, SAMPLE_PAD, n), F32)],
            compiler_params=_params(("parallel", "arbitrary"), 56),
        )(act, w, xp, xs, gate_p.reshape(-1, 1, n), gate_s)
        xs = xs[0]
    return xp, xs


def _out_proj_kernel(*refs, n_parts):
    x_refs = refs[:n_parts]
    xs_ref, w_ref, rp_ref, rs_ref, gp_ref, gs_ref, op_ref, os_ref, lhs_ref = refs[n_parts:]

    @pl.when(pl.program_id(1) == 0)
    def _():
        off = 0
        for x_ref in x_refs:
            lhs_ref[:ROW_TILE, off:off + x_ref.shape[1]] = x_ref[...]
            off += x_ref.shape[1]
        lhs_ref[ROW_TILE:, :] = xs_ref[...]

    acc = _dot(lhs_ref[...], w_ref[...].astype(BF16))
    op_ref[...] = rp_ref[...] + gp_ref[...] * acc[:ROW_TILE]
    os_ref[...] = rs_ref[...] + gs_ref[...] * acc[ROW_TILE:]


def _out_proj(parts, mixed_s, w, layer, xp, xs, gate_p, gate_s, rows_per_group):
    m, n = xp.shape
    kdim = w.shape[1]
    per = rows_per_group // ROW_TILE
    in_specs = [pl.BlockSpec((ROW_TILE, x.shape[1]), lambda i, j: (i, 0)) for x in parts]
    in_specs += [pl.BlockSpec((SAMPLE_PAD, kdim), lambda i, j: (0, 0)),
                 pl.BlockSpec((None, kdim, COL_TILE), lambda i, j: (layer, 0, j)),
                 pl.BlockSpec((ROW_TILE, COL_TILE), lambda i, j: (i, j)),
                 pl.BlockSpec((SAMPLE_PAD, COL_TILE), lambda i, j: (0, j)),
                 pl.BlockSpec((None, 1, COL_TILE), lambda i, j: (i // per, 0, j)),
                 pl.BlockSpec((SAMPLE_PAD, COL_TILE), lambda i, j: (0, j))]
    xp, xs = pl.pallas_call(
        functools.partial(_out_proj_kernel, n_parts=len(parts)),
        grid=(m // ROW_TILE, n // COL_TILE),
        in_specs=in_specs,
        out_specs=[pl.BlockSpec((ROW_TILE, COL_TILE), lambda i, j: (i, j)), _sample_out_spec(COL_TILE)],
        out_shape=[jax.ShapeDtypeStruct((m, n), F32), jax.ShapeDtypeStruct((m // ROW_TILE, SAMPLE_PAD, n), F32)],
        scratch_shapes=[pltpu.VMEM((COMBINED_TILE, kdim), BF16)],
        compiler_params=_params(("parallel", "arbitrary"), 48),
    )(*parts, mixed_s, w, xp, xs, gate_p.reshape(-1, 1, n), gate_s)
    return xp, xs[0]


def _tiles(name, width):
    a = SEG[name] // PROJ_TN
    return a, a + width // PROJ_TN


_NORM_TILES = (_tiles("qa", 1024), _tiles("qc", 1024), _tiles("kvs", 256), _tiles("kvw", 256), _tiles("kvm", 256))
_GELU_TILES = (_tiles("u", 2048),)
_SILU_TILES = (_tiles("qd", 1024),)
_SIGM_TILES = (_tiles("ga", 256),)


def _in_ranges(j, ranges):
    return functools.reduce(jnp.logical_or, [(j >= a) & (j < b) for a, b in ranges])


def _proj_kernel(off_ref, xp_ref, xs_ref, g_ref, scp_ref, shp_ref, scs_ref, shs_ref, w_ref, gain_ref, zp_ref, zs_ref,
                 lhs_ref):
    j = pl.program_id(1)

    @pl.when(j == 0)
    def _():
        _stage_modulated(lhs_ref, xp_ref, xs_ref, g_ref, scp_ref, shp_ref, scs_ref, shs_ref)

    lhs = lhs_ref[...]
    accs = [_dot_nt(lhs, w_ref[0, c * PROJ_TN:(c + 1) * PROJ_TN, :].astype(BF16)) for c in range(PROJ_TILES_PER_STEP)]
    for c, acc in enumerate(accs):
        _proj_epilogue(j * PROJ_TILES_PER_STEP + c, acc, gain_ref, zp_ref, zs_ref,
                       slice(c * PROJ_TN, (c + 1) * PROJ_TN))


def _proj_epilogue(tile, acc, gain_ref, zp_ref, zs_ref, cols):
    is_norm = _in_ranges(tile, _NORM_TILES)
    is_gelu = _in_ranges(tile, _GELU_TILES)
    is_silu = _in_ranges(tile, _SILU_TILES)
    is_sigm = _in_ranges(tile, _SIGM_TILES)

    def store(val):
        zp_ref[:, cols] = val[:ROW_TILE]
        zs_ref[:, cols] = val[ROW_TILE:]

    @pl.when(is_norm)
    def _():
        gain = gain_ref[:, cols]
        heads = [slice(hh * HEAD_DIM, (hh + 1) * HEAD_DIM) for hh in range(PROJ_TN // HEAD_DIM)]
        store(jnp.concatenate([_rms(acc[:, sl], gain[:, sl]) for sl in heads], axis=1))

    @pl.when(is_gelu)
    def _():
        store(_gelu(acc))

    @pl.when(is_silu)
    def _():
        store(_silu(acc) * ATT_SCALE)

    @pl.when(is_sigm)
    def _():
        store(_sigmoid(acc))

    @pl.when(jnp.logical_not(is_norm | is_gelu | is_silu | is_sigm))
    def _():
        store(acc)


def _proj_row_offsets():
    tn = PROJ_TILES_PER_STEP * PROJ_TN
    offs = []
    for name in ("qa", "u", "v", "qc", "qd", "fd", "vd", "gd", "kvc", "kvs", "kvw", "kvm"):
        a, w = _ORIG[name]
        offs += [a + k * tn for k in range(w // tn)]
    offs.append(_ORIG["ga"][0])
    assert len(offs) == N_PROJ // tn and all(o % 8 == 0 for o in offs)
    return offs


def _proj(xp, xs, g, mod_p, mod_s, idx, w_in, layer, gains, rows_per_group):
    m, d = xp.shape
    tn = PROJ_TILES_PER_STEP * PROJ_TN
    w_t = jnp.swapaxes(w_in, 1, 2)
    w_spec = pl.BlockSpec((pl.Element(1), pl.Element(tn), pl.Element(d)), lambda i, j, off: (layer, off[j] * 8, 0))
    zp, zs = pl.pallas_call(
        _proj_kernel,
        grid_spec=pltpu.PrefetchScalarGridSpec(
            num_scalar_prefetch=1, grid=(m // ROW_TILE, N_PROJ // tn),
            in_specs=_mod_specs(d, rows_per_group) + [w_spec,
                                                       pl.BlockSpec((None, 1, tn), lambda i, j, off: (j, 0, 0))],
            out_specs=[pl.BlockSpec((ROW_TILE, tn), lambda i, j, off: (i, j)), _sample_out_spec(tn)],
            scratch_shapes=[pltpu.VMEM((COMBINED_TILE, d), BF16)]),
        out_shape=[jax.ShapeDtypeStruct((m, N_PROJ), F32),
                   jax.ShapeDtypeStruct((m // ROW_TILE, SAMPLE_PAD, N_PROJ), F32)],
        compiler_params=_params(("parallel", "arbitrary"), 52),
    )(jnp.asarray(_proj_row_offsets(), jnp.int32) // 8, *_mod_args(xp, xs, g, mod_p, mod_s, idx), w_t, gains)
    return zp, zs[0]


GATES_PER_KV = 3 * GROUP


def _proj_gains(qkg):
    g = jnp.ones((N_PROJ // PROJ_TN, PROJ_TN), F32)
    two = lambda v: jnp.tile(v, PROJ_TN // HEAD_DIM)
    for name, idx, ntile in (("qa", 0, 4), ("qc", 4, 4), ("kvs", 2, 1), ("kvw", 3, 1), ("kvm", 5, 1)):
        t0 = SEG[name] // PROJ_TN
        g = g.at[t0:t0 + ntile].set(two(qkg[idx])[None, :])
    return g.reshape(-1, 1, PROJ_TILES_PER_STEP * PROJ_TN)


CMP_ROWS = 128


def _compress_body(x_of, w1_ref, b1_ref, w2_ref, pe_ref, gk_ref, o_ref, n_cmp):
    i = pl.program_id(1)
    row = lax.broadcasted_iota(jnp.int32, (CMP_ROWS, 1), 0)
    valid = (i * CMP_ROWS + row) < n_cmp
    half = NSA_CMP_STRIDE * HEAD_DIM
    for s in range(2):
        w1a = w1_ref[s, 0]
        w1b = w1_ref[s, 1]
        pe = pe_ref[s]
        bias = (_dot(pe[:, :half], w1a) + _dot(pe[:, half:], w1b))[0:1] + b1_ref[s]
        w2 = w2_ref[s]
        for k in range(KV_HEADS):
            x, xn = x_of(s * KV_HEADS + k)
            x = x.astype(BF16)
            h1 = _dot(x, w1a)
            h2 = _dot(x, w1b)
            h2n = _dot(xn.astype(BF16), w1b)
            h2s = jnp.where(row == CMP_ROWS - 1, h2n[0:1], pltpu.roll(h2, CMP_ROWS - 1, 0))
            hid = _gelu(h1 + h2s + bias)
            out = _dot(hid.astype(BF16), w2)
            if s == 0:
                out = _rms(out, gk_ref[...])
            o_ref[s, k] = jnp.where(valid, out, 0.0)


def _compress_kernel(r_ref, rn_ref, w1_ref, b1_ref, w2_ref, pe_ref, gk_ref, o_ref, *, n_cmp):
    row_w = 2 * KV_HEADS * HEAD_DIM

    def x_of(c):
        cols = [slice(p * row_w + c * HEAD_DIM, p * row_w + (c + 1) * HEAD_DIM) for p in range(NSA_CMP_STRIDE)]
        return (jnp.concatenate([r_ref[:, sl] for sl in cols], axis=1),
                jnp.concatenate([rn_ref[:, sl] for sl in cols], axis=1))

    _compress_body(x_of, w1_ref, b1_ref, w2_ref, pe_ref, gk_ref, o_ref, n_cmp)


PAGE_HALVES = PAGE_SIZE // NSA_CMP_STRIDE
PAGE_STREAMS = 2 * KV_HEADS


def _compress_paged_kernel(pt_ref, *refs, n_cmp):
    pages = refs[:PAGES_PER_STEP]
    nxt = refs[PAGES_PER_STEP]
    w1_ref, b1_ref, w2_ref, pe_ref, gk_ref, o_ref = refs[PAGES_PER_STEP + 1:]

    def flat(pg, c):
        return jnp.concatenate([pg[pl.ds(p * PAGE_STREAMS + c, PAGE_HALVES, stride=NSA_CMP_STRIDE * PAGE_STREAMS), :]
                                for p in range(NSA_CMP_STRIDE)], axis=1)

    def x_of(c):
        return jnp.concatenate([flat(pg, c) for pg in pages], axis=0), flat(nxt, c)

    _compress_body(x_of, w1_ref, b1_ref, w2_ref, pe_ref, gk_ref, o_ref, n_cmp)


def _compress_weights(w1, b1, w2, pe, gk):
    half = NSA_CMP_STRIDE * HEAD_DIM
    w1r = w1.reshape(2, 2, half, NSA_CMP_HIDDEN)
    pe8 = jnp.broadcast_to(pe.reshape(2, 1, 2 * half), (2, 8, 2 * half))
    args = (w1r.astype(BF16), b1.reshape(2, 1, NSA_CMP_HIDDEN), w2.astype(BF16), pe8.astype(BF16),
            gk.reshape(1, HEAD_DIM))
    return args, [a.shape for a in args]


def _compress(kv_flat, batch, w1, b1, w2, pe, gk):
    n_half = kv_flat.shape[0] // batch // NSA_CMP_STRIDE
    nblk = n_half // CMP_ROWS
    width = NSA_CMP_STRIDE * kv_flat.shape[1]
    r = kv_flat.reshape(batch * n_half, width)
    last8 = batch * n_half // 8 - 1
    wargs, wshapes = _compress_weights(w1, b1, w2, pe, gk)
    full = lambda shape: pl.BlockSpec(shape, lambda b, i: (0,) * len(shape))
    return pl.pallas_call(
        functools.partial(_compress_kernel, n_cmp=n_half - 1),
        grid=(batch, nblk),
        in_specs=[pl.BlockSpec((CMP_ROWS, width), lambda b, i: (b * nblk + i, 0)),
                  pl.BlockSpec((8, width), lambda b, i: (jnp.minimum((b * nblk + i + 1) * (CMP_ROWS // 8), last8), 0))]
                 + [full(s) for s in wshapes],
        out_specs=pl.BlockSpec((None, 2, KV_HEADS, CMP_ROWS, HEAD_DIM), lambda b, i: (b, 0, 0, i, 0)),
        out_shape=jax.ShapeDtypeStruct((batch, 2, KV_HEADS, n_half, HEAD_DIM), F32),
        compiler_params=_params(("parallel", "arbitrary"), 48),
    )(r, r, *wargs)


def _page_view(pool):
    depth, n_phys = pool.shape[:2]
    return pool.reshape(depth, n_phys, PAGE_SIZE * PAGE_STREAMS, HEAD_DIM)


def _compress_paged(pool, layer, page_table, w1, b1, w2, pe, gk):
    view = _page_view(pool)
    batch, n_pages = page_table.shape
    groups = n_pages // PAGES_PER_STEP
    n_half = n_pages * PAGE_HALVES
    wargs, wshapes = _compress_weights(w1, b1, w2, pe, gk)
    page = lambda fn: pl.BlockSpec((None, None, PAGE_SIZE * PAGE_STREAMS, HEAD_DIM), fn)
    in_specs = [page(lambda b, g, pt, r=r: (layer, pt[b, g * PAGES_PER_STEP + r], 0, 0)) for r in range(PAGES_PER_STEP)]
    in_specs.append(page(lambda b, g, pt: (layer, pt[b, jnp.minimum((g + 1) * PAGES_PER_STEP, n_pages - 1)], 0, 0)))
    in_specs += [pl.BlockSpec(s, lambda b, g, pt, n=len(s): (0,) * n) for s in wshapes]
    return pl.pallas_call(
        functools.partial(_compress_paged_kernel, n_cmp=n_half - 1),
        grid_spec=pltpu.PrefetchScalarGridSpec(
            num_scalar_prefetch=1, grid=(batch, groups), in_specs=in_specs,
            out_specs=pl.BlockSpec((None, 2, KV_HEADS, CMP_ROWS, HEAD_DIM), lambda b, g, pt: (b, 0, 0, g, 0))),
        out_shape=jax.ShapeDtypeStruct((batch, 2, KV_HEADS, n_half, HEAD_DIM), F32),
        compiler_params=_params(("parallel", "arbitrary"), 48),
    )(page_table, *([view] * (PAGES_PER_STEP + 1)), *wargs)


def _stack_heads(q):
    return jnp.concatenate([q[:, g * HEAD_DIM:(g + 1) * HEAD_DIM] for g in range(GROUP)], axis=0)


def _block_onehot(n_blocks, kpos, shift):
    blk = lax.broadcasted_iota(jnp.int32, (n_blocks, 1), 0)
    return jnp.where((kpos >> shift) == blk, 1.0, 0.0).astype(BF16)


def _nsa_kernel(*refs, tq, qpos0, n_sel, extents, has_tail, win_pos0, win_len):
    if len(extents) > 1:
        refs, osel_ref = refs[:-1], refs[-1]
    if has_tail:
        q_ref, ga_ref, kc_ref, vc_ref, ks_ref, vs_ref, kt_ref, vt_ref, kw_ref, vw_ref, o_ref = refs
    else:
        q_ref, ga_ref, kc_ref, vc_ref, ks_ref, vs_ref, kw_ref, vw_ref, o_ref = refs
    q0 = qpos0 + pl.program_id(2) * tq
    rows = GROUP * tq
    q4 = _stack_heads(q_ref[...]).astype(BF16)
    qpos = q0 + (lax.broadcasted_iota(jnp.int32, (rows, 1), 0) & (tq - 1))

    kc = kc_ref[...].astype(BF16)
    vc = vc_ref[...].astype(BF16)
    n_cp = kc.shape[0]
    cidx = lax.broadcasted_iota(jnp.int32, (1, n_cp), 1)
    s_cmp = _dot_nt(q4, kc) * ATT_SCALE
    (e_cmp,), den = _softmax_parts([s_cmp], [(cidx * NSA_CMP_STRIDE + (NSA_CMP_BLOCK - 1)) <= qpos])
    p_cmp = e_cmp / den
    o_cmp = _dot(p_cmp.astype(BF16), vc)

    n_sp = -(-n_sel // LANE) * LANE
    psum = functools.reduce(lambda a, b: a + b, [p_cmp[g * tq:(g + 1) * tq] for g in range(GROUP)])
    ci = lax.broadcasted_iota(jnp.int32, (n_cp, 1), 0) * NSA_CMP_STRIDE
    sj = lax.broadcasted_iota(jnp.int32, (1, n_sp), 1) * NSA_SEL_BLOCK
    cover = jnp.where((ci <= sj + (NSA_SEL_BLOCK - 1)) & (ci + (NSA_CMP_BLOCK - 1) >= sj), 1.0, 0.0).astype(BF16)
    imp = _dot_f32_lhs(psum, cover)
    col = lax.broadcasted_iota(jnp.int32, (1, n_sp), 1)
    qpos_t = q0 + lax.broadcasted_iota(jnp.int32, (tq, 1), 0)
    back = (qpos_t >> 6) - col
    valid = back >= 0
    forced = (col == 0) | (valid & (back < NSA_N_LOCAL))
    score = jnp.where(valid, jnp.where(forced, FORCE_SCORE, imp), NEG_INF)
    rank = _rank(score, n_sel, col)
    sel = jnp.where((rank < NSA_SEL_TOP) & (score > 0.5 * NEG_INF), 1.0, 0.0).astype(BF16)
    sel4 = jnp.concatenate([sel] * GROUP, axis=0)

    def sel_attend(n_keys):
        s_list, m_list, v_list = [], [], []

        def add_tile(k_bf, v_bf, kpos):
            picked = _dot(sel4, _block_onehot(n_sp, kpos, 6))
            s_list.append(_dot_nt(q4, k_bf) * ATT_SCALE)
            m_list.append(jnp.where(kpos <= qpos, picked, 0.0) > 0.5)
            v_list.append(v_bf)

        kt = min(n_keys, KEY_TILE)
        for t0 in range(0, n_keys, kt):
            add_tile(ks_ref[t0:t0 + kt, :].astype(BF16), vs_ref[t0:t0 + kt, :].astype(BF16),
                     t0 + lax.broadcasted_iota(jnp.int32, (1, kt), 1))
        if has_tail:
            add_tile(kt_ref[...].astype(BF16), vt_ref[...].astype(BF16),
                     n_keys + lax.broadcasted_iota(jnp.int32, (1, TAIL_ROWS), 1))
        e_list, den = _softmax_parts(s_list, m_list, never_empty=True)
        return functools.reduce(lambda a, b: a + b, [_dot(e.astype(BF16), v) for e, v in zip(e_list, v_list)]) / den

    if len(extents) == 1:
        o_sel = sel_attend(extents[0])
    else:
        need = (q0 - qpos0 + tq - 1) // extents[0]
        for idx, ext in enumerate(extents):
            @pl.when(need == idx)
            def _(ext=ext):
                osel_ref[...] = sel_attend(ext)
        o_sel = osel_ref[...]

    start = jnp.clip(q0 - NSA_WINDOW - win_pos0, 0, win_len - WIN_BAND)
    start = pl.multiple_of(start, LANE)
    kw = kw_ref[pl.ds(start, WIN_BAND), :].astype(BF16)
    vw = vw_ref[pl.ds(start, WIN_BAND), :].astype(BF16)
    kposw = win_pos0 + start + lax.broadcasted_iota(jnp.int32, (1, WIN_BAND), 1)
    s_win = _dot_nt(q4, kw) * ATT_SCALE
    m_win = jnp.where(kposw <= qpos, qpos - kposw, NSA_WINDOW) < NSA_WINDOW
    (e_win,), den = _softmax_parts([s_win], [m_win], never_empty=True)
    o_win = _dot(e_win.astype(BF16), vw) / den

    ga = ga_ref[...]
    ga = jnp.where(pl.program_id(1) == 0, ga, pltpu.roll(ga, LANE - GATES_PER_KV, 1))
    outs = []
    for g in range(GROUP):
        r = slice(g * tq, (g + 1) * tq)
        outs.append(ga[:, 3 * g:3 * g + 1] * o_cmp[r] + ga[:, 3 * g + 1:3 * g + 2] * o_sel[r]
                    + ga[:, 3 * g + 2:3 * g + 3] * o_win[r])
    o_ref[...] = jnp.concatenate(outs, axis=1).astype(o_ref.dtype)


CAUSAL_STEP = 512


def _causal_extents(tk, tq):
    if tk % CAUSAL_STEP or CAUSAL_STEP % tq:
        return (tk,)
    return tuple(range(CAUSAL_STEP, tk + 1, CAUSAL_STEP))


def _nsa(zq, tq_total, tq, qpos0, cmp_kv, sel_main, sel_cols, sel_tail, win, win_cols, win_pos0, n_total,
         causal_skip):
    batch = cmp_kv.shape[0]
    nq = tq_total // tq
    n_cp = cmp_kv.shape[3]
    tk = sel_main.shape[0] // batch
    lw = win.shape[0] // batch
    n_sel = -(-n_total // NSA_SEL_BLOCK)
    extents = _causal_extents(tk, tq) if causal_skip else (tk,)
    scratch = [pltpu.VMEM((GROUP * tq, HEAD_DIM), F32)] if len(extents) > 1 else []
    qa_blk = SEG["qa"] // (GROUP * HEAD_DIM)
    ga_blk = SEG["ga"] // LANE
    in_specs = [
        pl.BlockSpec((tq, GROUP * HEAD_DIM), lambda b, k, i: (b * nq + i, qa_blk + k)),
        pl.BlockSpec((tq, LANE), lambda b, k, i: (b * nq + i, ga_blk)),
        pl.BlockSpec((None, None, None, n_cp, HEAD_DIM), lambda b, k, i: (b, 0, k, 0, 0)),
        pl.BlockSpec((None, None, None, n_cp, HEAD_DIM), lambda b, k, i: (b, 1, k, 0, 0)),
        pl.BlockSpec((tk, HEAD_DIM), lambda b, k, i: (b, sel_cols[0] + k)),
        pl.BlockSpec((tk, HEAD_DIM), lambda b, k, i: (b, sel_cols[1] + k)),
    ]
    args = [zq, zq, cmp_kv, cmp_kv, sel_main, sel_main]
    if sel_tail is not None:
        in_specs += [pl.BlockSpec((TAIL_ROWS, HEAD_DIM), lambda b, k, i: (b, k)),
                     pl.BlockSpec((TAIL_ROWS, HEAD_DIM), lambda b, k, i: (b, KV_HEADS + k))]
        args += [sel_tail, sel_tail]
    in_specs += [pl.BlockSpec((lw, HEAD_DIM), lambda b, k, i: (b, win_cols[0] + k)),
                 pl.BlockSpec((lw, HEAD_DIM), lambda b, k, i: (b, win_cols[1] + k))]
    args += [win, win]
    return pl.pallas_call(
        functools.partial(_nsa_kernel, tq=tq, qpos0=qpos0, n_sel=n_sel, extents=extents,
                          has_tail=sel_tail is not None, win_pos0=win_pos0, win_len=lw),
        grid=(batch, KV_HEADS, nq),
        in_specs=in_specs,
        out_specs=pl.BlockSpec((tq, GROUP * HEAD_DIM), lambda b, k, i: (b * nq + i, k)),
        out_shape=jax.ShapeDtypeStruct((batch * tq_total, WIDTH), BF16),
        scratch_shapes=scratch,
        compiler_params=_params(("parallel", "parallel", "arbitrary"), 56),
    )(*args)


def _moba_kernel(*refs, tq, qpos0, extents, has_tail):
    if has_tail:
        q_ref, k_ref, v_ref, kt_ref, vt_ref, o_ref = refs
    else:
        q_ref, k_ref, v_ref, o_ref = refs
    q0 = qpos0 + pl.program_id(2) * tq
    rows = GROUP * tq
    q4f = _stack_heads(q_ref[...])
    q4 = q4f.astype(BF16)
    qpos = q0 + (lax.broadcasted_iota(jnp.int32, (rows, 1), 0) & (tq - 1))
    cur = qpos >> 8
    col = lax.broadcasted_iota(jnp.int32, (1, LANE), 1)

    def attend(n_keys):
        kt = min(n_keys, KEY_TILE)
        n_blk = n_keys // MOBA_BLOCK
        means = [jnp.sum(k_ref[t0:t0 + kt, :].reshape(kt // MOBA_BLOCK, MOBA_BLOCK, HEAD_DIM), axis=1)
                 * (1.0 / MOBA_BLOCK) for t0 in range(0, n_keys, kt)]
        if n_blk < LANE:
            means.append(jnp.zeros((LANE - n_blk, HEAD_DIM), F32))
        kmean = jnp.concatenate(means, axis=0)
        gate = jnp.where(col < cur, _dot_nt_f32(q4f, kmean), NEG_INF)
        rank = _rank(gate, n_blk, col)
        sel = jnp.where((rank < MOBA_TOP) & (gate > 0.5 * NEG_INF), 1.0, 0.0).astype(BF16)

        s_list, m_list, v_list = [], [], []

        def add_tile(k_bf, v_bf, kpos):
            picked = _dot(sel, _block_onehot(LANE, kpos, 8))
            own = jnp.where((kpos >> 8) == cur, jnp.where(kpos <= qpos, 1.0, 0.0), 0.0)
            s_list.append(_dot_nt(q4, k_bf) * ATT_SCALE)
            m_list.append((picked + own) > 0.5)
            v_list.append(v_bf)

        for t0 in range(0, n_keys, kt):
            add_tile(k_ref[t0:t0 + kt, :].astype(BF16), v_ref[t0:t0 + kt, :].astype(BF16),
                     t0 + lax.broadcasted_iota(jnp.int32, (1, kt), 1))
        if has_tail:
            add_tile(kt_ref[...].astype(BF16), vt_ref[...].astype(BF16),
                     n_keys + lax.broadcasted_iota(jnp.int32, (1, TAIL_ROWS), 1))
        e_list, den = _softmax_parts(s_list, m_list, never_empty=True)
        o = functools.reduce(lambda a, b: a + b, [_dot(e.astype(BF16), v) for e, v in zip(e_list, v_list)]) / den
        o_ref[...] = jnp.concatenate([o[g * tq:(g + 1) * tq] for g in range(GROUP)], axis=1).astype(o_ref.dtype)

    if len(extents) == 1:
        attend(extents[0])
    else:
        need = (q0 - qpos0 + tq - 1) // extents[0]
        for idx, ext in enumerate(extents):
            @pl.when(need == idx)
            def _(ext=ext):
                attend(ext)


def _moba(zq, tq_total, tq, qpos0, batch, main, cols, tail, causal_skip):
    nq = tq_total // tq
    tk = main.shape[0] // batch
    extents = _causal_extents(tk, tq) if causal_skip else (tk,)
    qc_blk = SEG["qc"] // (GROUP * HEAD_DIM)
    in_specs = [pl.BlockSpec((tq, GROUP * HEAD_DIM), lambda b, k, i: (b * nq + i, qc_blk + k)),
                pl.BlockSpec((tk, HEAD_DIM), lambda b, k, i: (b, cols[0] + k)),
                pl.BlockSpec((tk, HEAD_DIM), lambda b, k, i: (b, cols[1] + k))]
    args = [zq, main, main]
    if tail is not None:
        in_specs += [pl.BlockSpec((TAIL_ROWS, HEAD_DIM), lambda b, k, i: (b, k)),
                     pl.BlockSpec((TAIL_ROWS, HEAD_DIM), lambda b, k, i: (b, KV_HEADS + k))]
        args += [tail, tail]
    return pl.pallas_call(
        functools.partial(_moba_kernel, tq=tq, qpos0=qpos0, extents=extents, has_tail=tail is not None),
        grid=(batch, KV_HEADS, nq),
        in_specs=in_specs,
        out_specs=pl.BlockSpec((tq, GROUP * HEAD_DIM), lambda b, k, i: (b * nq + i, k)),
        out_shape=jax.ShapeDtypeStruct((batch * tq_total, WIDTH), BF16),
        compiler_params=_params(("parallel", "parallel", "arbitrary"), 56),
    )(*args)


def _layer_norm(v, g, b, eps=1e-5):
    mu = jnp.mean(v, axis=-1, keepdims=True)
    var = jnp.mean(jnp.square(v - mu), axis=-1, keepdims=True)
    return (v - mu) * lax.rsqrt(var + eps) * g + b


def _gmlp_kernel(u_ref, v_ref, ws_ref, bst_ref, g_ref, b_ref, o_ref):
    vn = _layer_norm(v_ref[...], g_ref[...], b_ref[...]).astype(BF16)
    r = lax.broadcasted_iota(jnp.int32, (GMLP_CHUNK, GMLP_CHUNK), 0)
    c = lax.broadcasted_iota(jnp.int32, (GMLP_CHUNK, GMLP_CHUNK), 1)
    bst = bst_ref[...]
    for g in range(N_HEADS):
        sl = slice(g * HEAD_DIM, (g + 1) * HEAD_DIM)
        wm = jnp.where(c <= r, ws_ref[g], 0.0).astype(BF16)
        sv = _dot(wm, vn[:, sl]) + bst[:, g:g + 1]
        o_ref[:, sl] = (u_ref[:, sl] * sv).astype(o_ref.dtype)


def _gmlp(z, ws, bs, ln_g, ln_b):
    m = z.shape[0]
    ub, vb = SEG["u"] // WIDTH, SEG["v"] // WIDTH
    full = lambda shape: pl.BlockSpec(shape, lambda i: (0,) * len(shape))
    return pl.pallas_call(
        _gmlp_kernel,
        grid=(m // GMLP_CHUNK,),
        in_specs=[pl.BlockSpec((GMLP_CHUNK, WIDTH), lambda i: (i, ub)),
                  pl.BlockSpec((GMLP_CHUNK, WIDTH), lambda i: (i, vb)),
                  full(ws.shape), full((GMLP_CHUNK, N_HEADS)), full((1, WIDTH)), full((1, WIDTH))],
        out_specs=pl.BlockSpec((GMLP_CHUNK, WIDTH), lambda i: (i, 0)),
        out_shape=jax.ShapeDtypeStruct((m, WIDTH), BF16),
        compiler_params=_params(("parallel",), 32),
    )(z, z, ws, bs.T, ln_g.reshape(1, WIDTH), ln_b.reshape(1, WIDTH))


def _hgrn_gates(fd, lb):
    f_gate = lb + (1.0 - lb) * _sigmoid(fd)
    return jnp.log(jnp.maximum(f_gate, TINY)), (1.0 - lb) * _sigmoid(-fd)


def _hgrn_readout(o, gd, g_out):
    return _rms(o, g_out) * _silu(gd)


HGRN_HEADS_PER_STEP = 4


def _hgrn_kernel(q_ref, f_ref, v_ref, gd_ref, lb_ref, go_ref, o_ref, s_ref):
    c = HGRN_CHUNK
    t_total = q_ref.shape[0]
    g_out = go_ref[...]
    ri = lax.broadcasted_iota(jnp.int32, (c, c), 0)
    ci = lax.broadcasted_iota(jnp.int32, (c, c), 1)
    tri = jnp.where(ci <= ri, 1.0, 0.0).astype(BF16)
    lane = lax.broadcasted_iota(jnp.int32, (1, c), 1)
    diag_mask = (ci <= ri) & ((ci >> 3) == (ri >> 3))

    def head_chunk(r0, hh, st):
        hs = slice(hh * HEAD_DIM, (hh + 1) * HEAD_DIM)
        q = q_ref[pl.ds(r0, c), hs]
        v = v_ref[pl.ds(r0, c), hs]
        lf, k = _hgrn_gates(f_ref[pl.ds(r0, c), hs], lb_ref[:, hs])
        cum = _dot_f32_rhs(tri, lf)
        a = jnp.zeros((c, c), F32)
        for bs in (32, 16, 8):
            nb = c // bs
            refq = jnp.concatenate(
                [jnp.zeros((bs, HEAD_DIM), F32)]
                + [jnp.broadcast_to(cum[b * bs - 1:b * bs], (bs, HEAD_DIM)) for b in range(1, nb)], axis=0)
            refk = jnp.concatenate(
                [jnp.broadcast_to(cum[(b + 1) * bs - 1:(b + 1) * bs], (bs, HEAD_DIM)) for b in range(nb)], axis=0)
            qt = (q * jnp.exp(jnp.minimum(cum - refq, 0.0))).astype(BF16)
            kt = (k * jnp.exp(jnp.minimum(refk - cum, 0.0))).astype(BF16)
            sh = bs.bit_length() - 1
            lvl = (((ri >> sh) & 1) == 1) & ((ci >> sh) == (ri >> sh) - 1)
            a = a + jnp.where(lvl, _dot_nt(qt, kt), 0.0)
        rows = []
        for blk in range(c // 8):
            b0 = blk * 8
            qb, kb, cb = q[b0:b0 + 8], k[b0:b0 + 8], cum[b0:b0 + 8]
            acc = jnp.zeros((8, c), F32)
            for s in range(8):
                w = qb * kb[s:s + 1] * jnp.exp(jnp.minimum(cb - cb[s:s + 1], 0.0))
                acc = jnp.where(lane == b0 + s, jnp.sum(w, axis=1, keepdims=True), acc)
            rows.append(acc)
        a = a + jnp.where(diag_mask, jnp.concatenate(rows, axis=0), 0.0)
        o = _dot(a.astype(BF16), v.astype(BF16)) + _dot_nt((q * jnp.exp(cum)).astype(BF16), st.astype(BF16))
        o_ref[pl.ds(r0, c), hs] = _hgrn_readout(o, gd_ref[pl.ds(r0, c), hs], g_out).astype(o_ref.dtype)
        last = cum[c - 1:c]
        kk = (k * jnp.exp(last - cum)).astype(BF16)
        return st * jnp.exp(last) + _dot_tn(v.astype(BF16), kk)

    def chunk(n, sts):
        r0 = pl.multiple_of(n * c, c)
        return tuple(head_chunk(r0, hh, st) for hh, st in enumerate(sts))

    zero = jnp.zeros((HEAD_DIM, HEAD_DIM), F32)
    sts = lax.fori_loop(0, t_total // c, chunk, (zero,) * HGRN_HEADS_PER_STEP, unroll=2)
    for hh, st in enumerate(sts):
        s_ref[hh] = st.T


def _hgrn(z, batch, lb, g_out):
    t = z.shape[0] // batch
    hw = HGRN_HEADS_PER_STEP * HEAD_DIM
    blk = lambda name: SEG[name] // hw
    col = lambda name: pl.BlockSpec((t, hw), lambda b, h: (b, blk(name) + h))
    return pl.pallas_call(
        _hgrn_kernel,
        grid=(batch, N_HEADS // HGRN_HEADS_PER_STEP),
        in_specs=[col("qd"), col("fd"), col("vd"), col("gd"),
                  pl.BlockSpec((1, hw), lambda b, h: (0, h)),
                  pl.BlockSpec((1, HEAD_DIM), lambda b, h: (0, 0))],
        out_specs=[pl.BlockSpec((t, hw), lambda b, h: (b, h)),
                   pl.BlockSpec((None, HGRN_HEADS_PER_STEP, HEAD_DIM, HEAD_DIM), lambda b, h: (b, h, 0, 0))],
        out_shape=[jax.ShapeDtypeStruct((batch * t, WIDTH), BF16),
                   jax.ShapeDtypeStruct((batch, N_HEADS, HEAD_DIM, HEAD_DIM), F32)],
        compiler_params=_params(("parallel", "parallel"), 48),
    )(z, z, z, z, lb.reshape(1, WIDTH), g_out.reshape(1, HEAD_DIM))


def _gmlp_step_kernel(u_ref, v_ref, w_ref, b_ref, g_ref, bb_ref, o_ref, vn_ref):
    vn = _layer_norm(v_ref[...], g_ref[...], bb_ref[...])
    vn_ref[...] = vn
    o_ref[...] = (u_ref[...] * (w_ref[...] * vn + b_ref[...])).astype(o_ref.dtype)


def _gmlp_step(u, v, ws, bs, ln_g, ln_b):
    rows = u.shape[0]
    w_row = jnp.repeat(ws[:, 0, 0], HEAD_DIM).reshape(1, WIDTH)
    b_row = jnp.repeat(bs[:, 0], HEAD_DIM).reshape(1, WIDTH)
    return pl.pallas_call(
        _gmlp_step_kernel,
        out_shape=[jax.ShapeDtypeStruct((rows, WIDTH), BF16), jax.ShapeDtypeStruct((rows, WIDTH), F32)],
    )(u, v, w_row, b_row, ln_g.reshape(1, WIDTH), ln_b.reshape(1, WIDTH))


def _hgrn_step_kernel(qc_ref, fc_ref, lbc_ref, v_ref, gd_ref, go_ref, s0_ref, o_ref, s_ref):
    lf, k = _hgrn_gates(fc_ref[...], lbc_ref[...])
    s_new = jnp.exp(lf) * s0_ref[...] + k * v_ref[...]
    s_ref[...] = s_new
    o = jnp.sum(qc_ref[...] * s_new, axis=0, keepdims=True)
    o_ref[...] = _hgrn_readout(o, gd_ref[...], go_ref[...]).astype(o_ref.dtype)


def _hgrn_step(qd, fd, vd, gd, lb, g_out, state, layer):
    batch = qd.shape[0]
    colv = lambda a: a.reshape(batch, N_HEADS, HEAD_DIM, 1)
    rowv = lambda a: a.reshape(batch, N_HEADS, 1, HEAD_DIM)
    cspec = pl.BlockSpec((None, None, HEAD_DIM, 1), lambda b, h: (b, h, 0, 0))
    rspec = pl.BlockSpec((None, None, 1, HEAD_DIM), lambda b, h: (b, h, 0, 0))
    return pl.pallas_call(
        _hgrn_step_kernel,
        grid=(batch, N_HEADS),
        in_specs=[cspec, cspec, pl.BlockSpec((None, HEAD_DIM, 1), lambda b, h: (h, 0, 0)), rspec, rspec,
                  pl.BlockSpec((1, HEAD_DIM), lambda b, h: (0, 0)),
                  pl.BlockSpec((None, None, None, HEAD_DIM, HEAD_DIM), lambda b, h: (layer, b, h, 0, 0))],
        out_specs=[rspec, pl.BlockSpec((None, None, HEAD_DIM, HEAD_DIM), lambda b, h: (b, h, 0, 0))],
        out_shape=[jax.ShapeDtypeStruct((batch, N_HEADS, 1, HEAD_DIM), BF16),
                   jax.ShapeDtypeStruct((batch, N_HEADS, HEAD_DIM, HEAD_DIM), F32)],
        compiler_params=_params(("parallel", "parallel"), 32),
    )(colv(qd), colv(fd), lb.reshape(N_HEADS, HEAD_DIM, 1), rowv(vd), rowv(gd), g_out.reshape(1, HEAD_DIM), state)


def _gather_kernel(pt_ref, *refs):
    o_ref = refs[-1]
    for r in range(PAGES_PER_STEP):
        for c in range(PAGE_STREAMS):
            o_ref[r * PAGE_SIZE:(r + 1) * PAGE_SIZE, c * HEAD_DIM:(c + 1) * HEAD_DIM] = (
                refs[r][pl.ds(c, PAGE_SIZE, stride=PAGE_STREAMS), :])


def _gather_pages(pool, layer, page_table):
    view = _page_view(pool)
    width = PAGE_STREAMS * HEAD_DIM
    batch, n_pages = page_table.shape
    groups = n_pages // PAGES_PER_STEP
    rows = PAGES_PER_STEP * PAGE_SIZE
    in_specs = [pl.BlockSpec((None, None, PAGE_SIZE * PAGE_STREAMS, HEAD_DIM),
                             lambda b, g, pt, r=r: (layer, pt[b, g * PAGES_PER_STEP + r], 0, 0))
                for r in range(PAGES_PER_STEP)]
    return pl.pallas_call(
        _gather_kernel,
        grid_spec=pltpu.PrefetchScalarGridSpec(
            num_scalar_prefetch=1, grid=(batch, groups), in_specs=in_specs,
            out_specs=pl.BlockSpec((rows, width), lambda b, g, pt: (b * groups + g, 0))),
        out_shape=jax.ShapeDtypeStruct((batch * n_pages * PAGE_SIZE, width), pool.dtype),
        compiler_params=_params(("parallel", "arbitrary"), 40),
    )(page_table, *([view] * PAGES_PER_STEP))


def _ffn(xp, xs, mod_p, mod_s, sub, mod_idx, g, wg, wu, wd, layer, rows_per_group):
    act = _norm_glu(xp, xs, g, mod_p, mod_s, mod_idx, wg, wu, layer, sub, rows_per_group)
    return _down(act, wd, layer, sub, xp, xs, mod_p[:, mod_idx, 2], mod_s[:, mod_idx, 2], 0.5)


def _kv5(a, batch):
    return a.reshape(batch, -1, 2, KV_HEADS, HEAD_DIM)


def _cache_rows_kernel(*refs):
    z_refs, o_ref = refs[:-1], refs[-1]
    rows = z_refs[0].shape[0]
    for li, z_ref in enumerate(z_refs):
        @pl.when(pl.program_id(0) == li)
        def _(z_ref=z_ref):
            for c in range(PAGE_STREAMS):
                o_ref[pl.ds(c, rows, stride=PAGE_STREAMS), :] = z_ref[:, c * HEAD_DIM:(c + 1) * HEAD_DIM]


def _cache_rows(z_layers, name, batch, keep):
    depth = len(z_layers)
    t = z_layers[0].shape[0] // batch
    rows = min(keep, ROW_TILE)
    per_batch, first = keep // rows, (t - keep) // rows
    n_steps = batch * per_batch
    width = PAGE_STREAMS * HEAD_DIM
    blk = SEG[name] // width

    def z_spec(li):
        def index(l, s):
            s_eff = jnp.where(l == li, s, jnp.where(l < li, 0, n_steps - 1))
            return ((s_eff // per_batch) * (t // rows) + first + s_eff % per_batch, blk)
        return pl.BlockSpec((rows, width), index)

    out = pl.pallas_call(
        _cache_rows_kernel,
        grid=(depth, n_steps),
        in_specs=[z_spec(li) for li in range(depth)],
        out_specs=pl.BlockSpec((None, rows * PAGE_STREAMS, HEAD_DIM), lambda l, s: (l, s, 0)),
        out_shape=jax.ShapeDtypeStruct((depth, batch * keep * PAGE_STREAMS, HEAD_DIM), F32),
        compiler_params=_params(("arbitrary", "arbitrary"), 32),
    )(*z_layers)
    return out.reshape(depth, batch, keep, 2, KV_HEADS, HEAD_DIM)


def _prompt_mix(z, batch, mixw):
    t = z.shape[0] // batch
    kvc = z[:, SEG["kvc"]:SEG["kvc"] + PAGE_STREAMS * HEAD_DIM]
    cmp_kv = _compress(kvc, batch, mixw["cw1"], mixw["cb1"], mixw["cw2"], mixw["cpe"], mixw["qkg"][1])
    kb = lambda name: SEG[name] // HEAD_DIM
    o_a = _nsa(z, t, 128, 0, cmp_kv, z, (kb("kvs"), kb("kvs") + KV_HEADS), None,
               z, (kb("kvw"), kb("kvw") + KV_HEADS), 0, t, True)
    o_b = _gmlp(z, mixw["ws"], mixw["bs"], mixw["ln_g"], mixw["ln_b"])
    o_c = _moba(z, t, 128, 0, batch, z, (kb("kvm"), kb("kvm") + KV_HEADS), None, True)
    o_d, s_d = _hgrn(z, batch, mixw["lb"], mixw["out_g"])
    return (o_a, o_b, o_c, o_d), s_d


def _pad_rows(a, batch, rows):
    out = jnp.zeros((batch, rows, a.shape[1]), a.dtype).at[:, 0].set(a)
    return out.reshape(batch * rows, a.shape[1])


def _sample_mix(z_pad, batch, mixw, pool_cmp, pool_sel, pool_moba, win_buf, state, page_table, layer):
    past = page_table.shape[1] * PAGE_SIZE
    z = z_pad[:batch]
    seg = lambda name, w: z[:, SEG[name]:SEG[name] + w]
    kvc, kvs, kvw, kvm = seg("kvc", 512), seg("kvs", 512), seg("kvw", 512), seg("kvm", 512)
    zq = _pad_rows(z, batch, SAMPLE_ROWS)
    dense_sel = _gather_pages(pool_sel, layer, page_table)
    dense_moba = _gather_pages(pool_moba, layer, page_table)
    cmp_kv = _compress_paged(pool_cmp, layer, page_table, mixw["cw1"], mixw["cb1"], mixw["cw2"], mixw["cpe"],
                             mixw["qkg"][1])
    wlen = win_buf.shape[2]
    band = jnp.concatenate([win_buf[layer].reshape(batch, wlen, 512), kvw[:, None, :]], axis=1)
    win = jnp.concatenate([band, jnp.zeros((batch, WIN_BAND - wlen - 1, 512), F32)], axis=1)
    o_a = _nsa(zq, SAMPLE_ROWS, SAMPLE_ROWS, past, cmp_kv, dense_sel, (0, KV_HEADS),
               _pad_rows(kvs, batch, TAIL_ROWS), win.reshape(batch * WIN_BAND, 512), (0, KV_HEADS),
               past - wlen, past + 1, False)
    o_c = _moba(zq, SAMPLE_ROWS, SAMPLE_ROWS, past, batch, dense_moba, (0, KV_HEADS),
                _pad_rows(kvm, batch, TAIL_ROWS), False)
    o_b, v_n = _gmlp_step(seg("u", WIDTH), seg("v", WIDTH), mixw["ws"], mixw["bs"], mixw["ln_g"], mixw["ln_b"])
    o_d, s_d = _hgrn_step(seg("qd", WIDTH), seg("fd", WIDTH), seg("vd", WIDTH), seg("gd", WIDTH),
                          mixw["lb"], mixw["out_g"], state, layer)
    mixed = jnp.concatenate([o_a[::SAMPLE_ROWS], o_b, o_c[::SAMPLE_ROWS], o_d.reshape(batch, WIDTH)], axis=1)
    mixed = jnp.concatenate([mixed, jnp.zeros((z_pad.shape[0] - batch, mixed.shape[1]), mixed.dtype)], axis=0)
    new_state = (_kv5(kvc, batch), _kv5(kvs, batch), _kv5(kvm, batch),
                 band[:, 1:].reshape(batch, wlen, 2, KV_HEADS, HEAD_DIM), s_d, v_n.reshape(batch, 1, WIDTH))
    return mixed, new_state


def kernel(x_prompt, x_sample, c_prompt, c_sample, cache_nsa_cmp_kv, cache_nsa_sel_kv, cache_moba_kv, cache_nsa_win_kv, state_hgrn, page_table, w_ada, b_ada, norm_g, w_ffn_gate, w_ffn_up, w_ffn_down, w_in, w_out, qk_norm_g, nsa_cmp_w1, nsa_cmp_b1, nsa_cmp_w2, nsa_cmp_pos, gmlp_ln_g, gmlp_ln_b, gmlp_ws, gmlp_bs, hgrn_lb_logits, hgrn_out_g):
    bp, t, d = x_prompt.shape
    bs = x_sample.shape[0]
    depth = w_in.shape[0]
    srows = SAMPLE_PAD
    lb_p = jax.nn.softmax(hgrn_lb_logits.astype(F32), axis=0)
    lb_all = jnp.clip(jnp.cumsum(lb_p, axis=0) - lb_p[0:1], 0.0, 1.0)

    c_all = jnp.concatenate([c_prompt, c_sample, jnp.zeros((16 - bp - bs, d), F32)], axis=0)
    mod = _ada_mod(c_all, w_ada, b_ada).reshape(depth, 16, 3, 3, d)

    xp = x_prompt.reshape(bp * t, d)
    xs = jnp.concatenate([x_sample.reshape(bs, d), jnp.zeros((srows - bs, d), F32)], axis=0)
    z_layers, hgrn_p, st_s = [], [], []
    for l in range(depth):
        mod_p = mod[l, :bp]
        mod_s = jnp.concatenate([mod[l, bp:bp + bs], jnp.zeros((srows - bs, 3, 3, d), F32)], axis=0)
        gains = _proj_gains(qk_norm_g[l])
        mixw = dict(qkg=qk_norm_g[l], cw1=nsa_cmp_w1[l], cb1=nsa_cmp_b1[l], cw2=nsa_cmp_w2[l], cpe=nsa_cmp_pos[l],
                    ln_g=gmlp_ln_g[l], ln_b=gmlp_ln_b[l], ws=gmlp_ws[l], bs=gmlp_bs[l], lb=lb_all[l],
                    out_g=hgrn_out_g[l])
        g = norm_g[l]

        xp, xs = _ffn(xp, xs, mod_p, mod_s, 0, 0, g[0], w_ffn_gate, w_ffn_up, w_ffn_down, l, t)
        zp, zs = _proj(xp, xs, g[1], mod_p, mod_s, 1, w_in, l, gains, t)
        mixed_p, sp = _prompt_mix(zp, bp, mixw)
        mixed_s, ss = _sample_mix(zs, bs, mixw, cache_nsa_cmp_kv, cache_nsa_sel_kv, cache_moba_kv,
                                  cache_nsa_win_kv, state_hgrn, page_table, l)
        xp, xs = _out_proj(mixed_p, mixed_s, w_out, l, xp, xs, mod_p[:, 1, 2], mod_s[:, 1, 2], t)
        xp, xs = _ffn(xp, xs, mod_p, mod_s, 1, 2, g[2], w_ffn_gate, w_ffn_up, w_ffn_down, l, t)
        z_layers.append(zp)
        hgrn_p.append(sp)
        st_s.append(ss)

    stack = lambda sts, i: jnp.stack([s[i] for s in sts])
    return (xp.reshape(bp, t, d), xs[:bs].reshape(bs, 1, d),
            _cache_rows(z_layers, "kvc", bp, t), _cache_rows(z_layers, "kvs", bp, t),
            _cache_rows(z_layers, "kvm", bp, t), _cache_rows(z_layers, "kvw", bp, min(NSA_WINDOW, t)),
            jnp.stack(hgrn_p),
            stack(st_s, 0), stack(st_s, 1), stack(st_s, 2), stack(st_s, 3), stack(st_s, 4), stack(st_s, 5))
```

```python
import functools

import jax
import jax.numpy as jnp
from jax import lax
from jax.experimental import pallas as pl
from jax.experimental.pallas import tpu as pltpu

F32 = jnp.float32
BF16 = jnp.bfloat16

HEAD_DIM = 128
N_HEADS = 8
KV_HEADS = 2
GROUP = N_HEADS // KV_HEADS
WIDTH = N_HEADS * HEAD_DIM
PAGE_SIZE = 128
PAGES_PER_STEP = 16

NSA_CMP_STRIDE = 16
NSA_CMP_BLOCK = 32
NSA_CMP_HIDDEN = 256
NSA_SEL_BLOCK = 64
NSA_SEL_TOP = 16
NSA_N_LOCAL = 2
NSA_WINDOW = 512
GMLP_CHUNK = 128
MOBA_BLOCK = 256
MOBA_TOP = 3
HGRN_CHUNK = 64
NEG_INF = -1e30
FORCE_SCORE = 1e4
TINY = 1e-30
ATT_SCALE = HEAD_DIM ** -0.5

LANE = 128
KEY_TILE = 4096
WIN_BAND = NSA_WINDOW + 128
SAMPLE_ROWS = 8
TAIL_ROWS = 256

PROJ_TN = 256
PROJ_TILES_PER_STEP = 2
SEG = {}
_off = 0
for _name, _w in (("qa", 1024), ("u", 1024), ("v", 1024), ("qc", 1024), ("qd", 1024), ("fd", 1024),
                  ("vd", 1024), ("gd", 1024), ("kvc", 512), ("kvs", 512), ("kvw", 512), ("kvm", 512),
                  ("ga", 256), ("pad", 256)):
    SEG[_name] = _off
    _off += _w
N_PROJ = _off

_ORIG = {}
_o = 0
for _name, _w in (("qa", 1024), ("kvc", 512), ("kvs", 512), ("kvw", 512), ("ga", 24), ("u", 1024), ("v", 1024),
                  ("qc", 1024), ("kvm", 512), ("qd", 1024), ("fd", 1024), ("vd", 1024), ("gd", 1024)):
    _ORIG[_name] = (_o, _w)
    _o += _w


def _params(sem, vmem_mb):
    return pltpu.CompilerParams(dimension_semantics=sem, vmem_limit_bytes=vmem_mb << 20)


def _sigmoid(x):
    return 1.0 / (1.0 + jnp.exp(-x))


def _silu(x):
    return x * _sigmoid(x)


def _gelu(x):
    return 0.5 * x * (1.0 + jnp.tanh(0.7978845608028654 * (x + 0.044715 * (x * x * x))))


def _dot(a, b):
    return jnp.dot(a, b, preferred_element_type=F32)


def _dot_nt(a, b):
    return lax.dot_general(a, b, (((1,), (1,)), ((), ())), preferred_element_type=F32)


def _dot_tn(a, b):
    return lax.dot_general(a, b, (((0,), (0,)), ((), ())), preferred_element_type=F32)


def _split3(a):
    hi = a.astype(BF16)
    r1 = a - hi.astype(F32)
    mid = r1.astype(BF16)
    lo = (r1 - mid.astype(F32)).astype(BF16)
    return hi, mid, lo


def _dot_f32_lhs(a, b_exact):
    hi, mid, lo = _split3(a)
    return _dot(hi, b_exact) + _dot(mid, b_exact) + _dot(lo, b_exact)


def _dot_f32_rhs(a_exact, b):
    hi, mid, lo = _split3(b)
    return _dot(a_exact, hi) + _dot(a_exact, mid) + _dot(a_exact, lo)


def _dot_nt_f32(a, b):
    ah, am, _ = _split3(a)
    bh, bm, _ = _split3(b)
    return _dot_nt(ah, bh) + _dot_nt(ah, bm) + _dot_nt(am, bh)


def _rms(x, g, eps=1e-6):
    return x * lax.rsqrt(jnp.mean(x * x, axis=-1, keepdims=True) + eps) * g


def _softmax_parts(s_list, m_list, never_empty=False):
    sm = [jnp.where(m, s, NEG_INF) for s, m in zip(s_list, m_list)]
    mx = functools.reduce(jnp.maximum, [jnp.max(s, axis=-1, keepdims=True) for s in sm])
    if never_empty:
        e = [jnp.exp(s - mx) for s in sm]
    else:
        e = [jnp.where(m, jnp.exp(s - mx), 0.0) for s, m in zip(sm, m_list)]
    den = functools.reduce(lambda a, b: a + b, [jnp.sum(x, axis=-1, keepdims=True) for x in e])
    return e, jnp.maximum(den, 1e-30)


def _rank(score, n, col):
    rank = jnp.zeros(score.shape, jnp.int32)
    for i in range(n):
        si = score[:, i:i + 1]
        before = jnp.where(si > score, 1, jnp.where((si == score) & (col > i), 1, 0))
        rank = rank + before
    return rank


def _ada_kernel(c_ref, w_ref, b_ref, o_ref):
    a = _silu(c_ref[...]).astype(BF16)
    o_ref[...] = _dot(a, w_ref[...].astype(BF16)) + b_ref[...]


def _ada_mod(c, w_ada, b_ada):
    depth, d, n = w_ada.shape
    rows = c.shape[0]
    tn = 512
    return pl.pallas_call(
        _ada_kernel,
        grid=(depth, n // tn),
        in_specs=[pl.BlockSpec((rows, d), lambda l, j: (0, 0)),
                  pl.BlockSpec((None, d, tn), lambda l, j: (l, 0, j)),
                  pl.BlockSpec((None, 1, tn), lambda l, j: (l, 0, j))],
        out_specs=pl.BlockSpec((None, rows, tn), lambda l, j: (l, 0, j)),
        out_shape=jax.ShapeDtypeStruct((depth, rows, n), F32),
        compiler_params=_params(("parallel", "parallel"), 40),
    )(c, w_ada, b_ada.reshape(depth, 1, n))


ROW_TILE = 1024
SAMPLE_PAD = 16
COMBINED_TILE = ROW_TILE + SAMPLE_PAD
COL_TILE = 256
NORM_CHUNK = 128


def _single_buffered(shape, index_map):
    return pl.BlockSpec(shape, index_map, pipeline_mode=pl.Buffered(1))


def _sample_out_spec(tn):
    return pl.BlockSpec((None, SAMPLE_PAD, tn), lambda i, j, *_: (i, 0, j))


def _stage_modulated(lhs_ref, xp_ref, xs_ref, g_ref, scp_ref, shp_ref, scs_ref, shs_ref):
    g = g_ref[...]
    one_sc, sh = 1.0 + scp_ref[...], shp_ref[...]

    def body(c, carry):
        r0 = pl.multiple_of(c * NORM_CHUNK, NORM_CHUNK)
        lhs_ref[pl.ds(r0, NORM_CHUNK), :] = (_rms(xp_ref[pl.ds(r0, NORM_CHUNK), :], g) * one_sc + sh).astype(BF16)
        return carry

    lax.fori_loop(0, ROW_TILE // NORM_CHUNK, body, 0)
    lhs_ref[ROW_TILE:, :] = (_rms(xs_ref[...], g) * (1.0 + scs_ref[...]) + shs_ref[...]).astype(BF16)


def _mod_specs(d, rows_per_group):
    per = rows_per_group // ROW_TILE
    vec_p = pl.BlockSpec((None, 1, d), lambda i, j, *_: (i // per, 0, 0))
    vec_s = pl.BlockSpec((SAMPLE_PAD, d), lambda i, j, *_: (0, 0))
    return [_single_buffered((ROW_TILE, d), lambda i, j, *_: (i, 0)), vec_s,
            pl.BlockSpec((1, d), lambda i, j, *_: (0, 0)), vec_p, vec_p, vec_s, vec_s]


def _mod_args(xp, xs, g, mod_p, mod_s, idx):
    d = xp.shape[1]
    return (xp, xs, g.reshape(1, d), mod_p[:, idx, 1].reshape(-1, 1, d), mod_p[:, idx, 0].reshape(-1, 1, d),
            mod_s[:, idx, 1], mod_s[:, idx, 0])


def _norm_glu_kernel(xp_ref, xs_ref, g_ref, scp_ref, shp_ref, scs_ref, shs_ref, wg_ref, wu_ref, o_ref, lhs_ref):
    @pl.when(pl.program_id(1) == 0)
    def _():
        _stage_modulated(lhs_ref, xp_ref, xs_ref, g_ref, scp_ref, shp_ref, scs_ref, shs_ref)

    h = lhs_ref[...]
    a = _dot(h, wg_ref[...].astype(BF16))
    b = _dot(h, wu_ref[...].astype(BF16))
    o_ref[...] = (_silu(a) * b).astype(o_ref.dtype)


def _norm_glu(xp, xs, g, mod_p, mod_s, idx, wg, wu, layer, sub, rows_per_group):
    m, d = xp.shape
    f = wg.shape[-1]
    nt = m // ROW_TILE
    wspec = pl.BlockSpec((None, None, d, COL_TILE), lambda i, j: (layer, sub, 0, j))
    return pl.pallas_call(
        _norm_glu_kernel,
        grid=(nt, f // COL_TILE),
        in_specs=_mod_specs(d, rows_per_group) + [wspec, wspec],
        out_specs=pl.BlockSpec((COMBINED_TILE, COL_TILE), lambda i, j: (i, j)),
        out_shape=jax.ShapeDtypeStruct((nt * COMBINED_TILE, f), BF16),
        scratch_shapes=[pltpu.VMEM((COMBINED_TILE, d), BF16)],
        compiler_params=_params(("parallel", "arbitrary"), 52),
    )(*_mod_args(xp, xs, g, mod_p, mod_s, idx), wg, wu)


DOWN_TILES = 2


def _down_kernel(x_ref, w_ref, rp_ref, rs_ref, gp_ref, gs_ref, op_ref, os_ref, *, coef):
    acc = _dot(x_ref[...], w_ref[...].astype(BF16))
    gp = coef * gp_ref[...]
    for r in range(DOWN_TILES):
        rows = slice(r * ROW_TILE, (r + 1) * ROW_TILE)
        op_ref[rows, :] = rp_ref[rows, :] + gp * acc[r * COMBINED_TILE:r * COMBINED_TILE + ROW_TILE]
    os_ref[...] = rs_ref[...] + (coef * gs_ref[...]) * acc[ROW_TILE:COMBINED_TILE]


def _down(act, w, layer, sub, xp, xs, gate_p, gate_s, coef):
    m, n = xp.shape
    kdim = act.shape[1]
    tk = kdim // 2
    rows = DOWN_TILES * ROW_TILE
    assert rows == m // gate_p.shape[0]
    for ks in range(2):
        xp, xs = pl.pallas_call(
            functools.partial(_down_kernel, coef=coef),
            grid=(m // rows, n // COL_TILE),
            in_specs=[_single_buffered((DOWN_TILES * COMBINED_TILE, tk), lambda i, j, ks=ks: (i, ks)),
                      pl.BlockSpec((None, None, tk, COL_TILE), lambda i, j, ks=ks: (layer, sub, ks, j)),
                      pl.BlockSpec((rows, COL_TILE), lambda i, j: (i, j)),
                      pl.BlockSpec((SAMPLE_PAD, COL_TILE), lambda i, j: (0, j)),
                      pl.BlockSpec((None, 1, COL_TILE), lambda i, j: (i, 0, j)),
                      pl.BlockSpec((SAMPLE_PAD, COL_TILE), lambda i, j: (0, j))],
            out_specs=[pl.BlockSpec((rows, COL_TILE), lambda i, j: (i, j)), _sample_out_spec(COL_TILE)],
            out_shape=[jax.ShapeDtypeStruct((m, n), F32), jax.ShapeDtypeStruct((m // rows, SAMPLE_PAD, n), F32)],
            compiler_params=_params(("parallel", "arbitrary"), 56),
        )(act, w, xp, xs, gate_p.reshape(-1, 1, n), gate_s)
        xs = xs[0]
    return xp, xs


def _out_proj_kernel(*refs, n_parts):
    x_refs = refs[:n_parts]
    xs_ref, w_ref, rp_ref, rs_ref, gp_ref, gs_ref, op_ref, os_ref, lhs_ref = refs[n_parts:]

    @pl.when(pl.program_id(1) == 0)
    def _():
        off = 0
        for x_ref in x_refs:
            lhs_ref[:ROW_TILE, off:off + x_ref.shape[1]] = x_ref[...]
            off += x_ref.shape[1]
        lhs_ref[ROW_TILE:, :] = xs_ref[...]

    acc = _dot(lhs_ref[...], w_ref[...].astype(BF16))
    op_ref[...] = rp_ref[...] + gp_ref[...] * acc[:ROW_TILE]
    os_ref[...] = rs_ref[...] + gs_ref[...] * acc[ROW_TILE:]


def _out_proj(parts, mixed_s, w, layer, xp, xs, gate_p, gate_s, rows_per_group):
    m, n = xp.shape
    kdim = w.shape[1]
    per = rows_per_group // ROW_TILE
    in_specs = [pl.BlockSpec((ROW_TILE, x.shape[1]), lambda i, j: (i, 0)) for x in parts]
    in_specs += [pl.BlockSpec((SAMPLE_PAD, kdim), lambda i, j: (0, 0)),
                 pl.BlockSpec((None, kdim, COL_TILE), lambda i, j: (layer, 0, j)),
                 pl.BlockSpec((ROW_TILE, COL_TILE), lambda i, j: (i, j)),
                 pl.BlockSpec((SAMPLE_PAD, COL_TILE), lambda i, j: (0, j)),
                 pl.BlockSpec((None, 1, COL_TILE), lambda i, j: (i // per, 0, j)),
                 pl.BlockSpec((SAMPLE_PAD, COL_TILE), lambda i, j: (0, j))]
    xp, xs = pl.pallas_call(
        functools.partial(_out_proj_kernel, n_parts=len(parts)),
        grid=(m // ROW_TILE, n // COL_TILE),
        in_specs=in_specs,
        out_specs=[pl.BlockSpec((ROW_TILE, COL_TILE), lambda i, j: (i, j)), _sample_out_spec(COL_TILE)],
        out_shape=[jax.ShapeDtypeStruct((m, n), F32), jax.ShapeDtypeStruct((m // ROW_TILE, SAMPLE_PAD, n), F32)],
        scratch_shapes=[pltpu.VMEM((COMBINED_TILE, kdim), BF16)],
        compiler_params=_params(("parallel", "arbitrary"), 48),
    )(*parts, mixed_s, w, xp, xs, gate_p.reshape(-1, 1, n), gate_s)
    return xp, xs[0]


def _tiles(name, width):
    a = SEG[name] // PROJ_TN
    return a, a + width // PROJ_TN


_NORM_TILES = (_tiles("qa", 1024), _tiles("qc", 1024), _tiles("kvs", 256), _tiles("kvw", 256), _tiles("kvm", 256))
_GELU_TILES = (_tiles("u", 2048),)
_SILU_TILES = (_tiles("qd", 1024),)
_SIGM_TILES = (_tiles("ga", 256),)


def _in_ranges(j, ranges):
    return functools.reduce(jnp.logical_or, [(j >= a) & (j < b) for a, b in ranges])


def _proj_kernel(off_ref, xp_ref, xs_ref, g_ref, scp_ref, shp_ref, scs_ref, shs_ref, w_ref, gain_ref, zp_ref, zs_ref,
                 lhs_ref):
    j = pl.program_id(1)

    @pl.when(j == 0)
    def _():
        _stage_modulated(lhs_ref, xp_ref, xs_ref, g_ref, scp_ref, shp_ref, scs_ref, shs_ref)

    lhs = lhs_ref[...]
    accs = [_dot_nt(lhs, w_ref[0, c * PROJ_TN:(c + 1) * PROJ_TN, :].astype(BF16)) for c in range(PROJ_TILES_PER_STEP)]
    for c, acc in enumerate(accs):
        _proj_epilogue(j * PROJ_TILES_PER_STEP + c, acc, gain_ref, zp_ref, zs_ref,
                       slice(c * PROJ_TN, (c + 1) * PROJ_TN))


def _proj_epilogue(tile, acc, gain_ref, zp_ref, zs_ref, cols):
    is_norm = _in_ranges(tile, _NORM_TILES)
    is_gelu = _in_ranges(tile, _GELU_TILES)
    is_silu = _in_ranges(tile, _SILU_TILES)
    is_sigm = _in_ranges(tile, _SIGM_TILES)

    def store(val):
        zp_ref[:, cols] = val[:ROW_TILE]
        zs_ref[:, cols] = val[ROW_TILE:]

    @pl.when(is_norm)
    def _():
        gain = gain_ref[:, cols]
        heads = [slice(hh * HEAD_DIM, (hh + 1) * HEAD_DIM) for hh in range(PROJ_TN // HEAD_DIM)]
        store(jnp.concatenate([_rms(acc[:, sl], gain[:, sl]) for sl in heads], axis=1))

    @pl.when(is_gelu)
    def _():
        store(_gelu(acc))

    @pl.when(is_silu)
    def _():
        store(_silu(acc) * ATT_SCALE)

    @pl.when(is_sigm)
    def _():
        store(_sigmoid(acc))

    @pl.when(jnp.logical_not(is_norm | is_gelu | is_silu | is_sigm))
    def _():
        store(acc)


def _proj_row_offsets():
    tn = PROJ_TILES_PER_STEP * PROJ_TN
    offs = []
    for name in ("qa", "u", "v", "qc", "qd", "fd", "vd", "gd", "kvc", "kvs", "kvw", "kvm"):
        a, w = _ORIG[name]
        offs += [a + k * tn for k in range(w // tn)]
    offs.append(_ORIG["ga"][0])
    assert len(offs) == N_PROJ // tn and all(o % 8 == 0 for o in offs)
    return offs


def _proj(xp, xs, g, mod_p, mod_s, idx, w_in, layer, gains, rows_per_group):
    m, d = xp.shape
    tn = PROJ_TILES_PER_STEP * PROJ_TN
    w_t = jnp.swapaxes(w_in, 1, 2)
    w_spec = pl.BlockSpec((pl.Element(1), pl.Element(tn), pl.Element(d)), lambda i, j, off: (layer, off[j] * 8, 0))
    zp, zs = pl.pallas_call(
        _proj_kernel,
        grid_spec=pltpu.PrefetchScalarGridSpec(
            num_scalar_prefetch=1, grid=(m // ROW_TILE, N_PROJ // tn),
            in_specs=_mod_specs(d, rows_per_group) + [w_spec,
                                                       pl.BlockSpec((None, 1, tn), lambda i, j, off: (j, 0, 0))],
            out_specs=[pl.BlockSpec((ROW_TILE, tn), lambda i, j, off: (i, j)), _sample_out_spec(tn)],
            scratch_shapes=[pltpu.VMEM((COMBINED_TILE, d), BF16)]),
        out_shape=[jax.ShapeDtypeStruct((m, N_PROJ), F32),
                   jax.ShapeDtypeStruct((m // ROW_TILE, SAMPLE_PAD, N_PROJ), F32)],
        compiler_params=_params(("parallel", "arbitrary"), 52),
    )(jnp.asarray(_proj_row_offsets(), jnp.int32) // 8, *_mod_args(xp, xs, g, mod_p, mod_s, idx), w_t, gains)
    return zp, zs[0]


GATES_PER_KV = 3 * GROUP


def _proj_gains(qkg):
    g = jnp.ones((N_PROJ // PROJ_TN, PROJ_TN), F32)
    two = lambda v: jnp.tile(v, PROJ_TN // HEAD_DIM)
    for name, idx, ntile in (("qa", 0, 4), ("qc", 4, 4), ("kvs", 2, 1), ("kvw", 3, 1), ("kvm", 5, 1)):
        t0 = SEG[name] // PROJ_TN
        g = g.at[t0:t0 + ntile].set(two(qkg[idx])[None, :])
    return g.reshape(-1, 1, PROJ_TILES_PER_STEP * PROJ_TN)


CMP_ROWS = 128


def _compress_body(x_of, w1_ref, b1_ref, w2_ref, pe_ref, gk_ref, o_ref, n_cmp):
    i = pl.program_id(1)
    row = lax.broadcasted_iota(jnp.int32, (CMP_ROWS, 1), 0)
    valid = (i * CMP_ROWS + row) < n_cmp
    half = NSA_CMP_STRIDE * HEAD_DIM
    hid_w = NSA_CMP_HIDDEN
    for s in range(2):
        w1ab = w1_ref[s]
        w1b = w1ab[:, hid_w:]
        pe = pe_ref[s]
        bias = (_dot(pe[:, :half], w1ab[:, :hid_w]) + _dot(pe[:, half:], w1b))[0:1] + b1_ref[s]
        w2 = w2_ref[s]
        xs = [x_of(s * KV_HEADS + k) for k in range(KV_HEADS)]
        h = _dot(jnp.concatenate([x.astype(BF16) for x, _ in xs], axis=0), w1ab)
        hn = _dot(jnp.concatenate([xn.astype(BF16) for _, xn in xs], axis=0), w1b)
        for k in range(KV_HEADS):
            h1 = h[k * CMP_ROWS:(k + 1) * CMP_ROWS, :hid_w]
            h2 = h[k * CMP_ROWS:(k + 1) * CMP_ROWS, hid_w:]
            h2s = jnp.where(row == CMP_ROWS - 1, hn[k * 8:k * 8 + 1], pltpu.roll(h2, CMP_ROWS - 1, 0))
            hid = _gelu(h1 + h2s + bias)
            out = _dot(hid.astype(BF16), w2)
            if s == 0:
                out = _rms(out, gk_ref[...])
            o_ref[s, k] = jnp.where(valid, out, 0.0)


def _compress_kernel(r_ref, rn_ref, w1_ref, b1_ref, w2_ref, pe_ref, gk_ref, o_ref, *, n_cmp):
    row_w = 2 * KV_HEADS * HEAD_DIM

    def x_of(c):
        cols = [slice(p * row_w + c * HEAD_DIM, p * row_w + (c + 1) * HEAD_DIM) for p in range(NSA_CMP_STRIDE)]
        return (jnp.concatenate([r_ref[:, sl] for sl in cols], axis=1),
                jnp.concatenate([rn_ref[:, sl] for sl in cols], axis=1))

    _compress_body(x_of, w1_ref, b1_ref, w2_ref, pe_ref, gk_ref, o_ref, n_cmp)


PAGE_HALVES = PAGE_SIZE // NSA_CMP_STRIDE
PAGE_STREAMS = 2 * KV_HEADS


def _compress_paged_kernel(pt_ref, *refs, n_cmp):
    pages = refs[:PAGES_PER_STEP]
    nxt = refs[PAGES_PER_STEP]
    w1_ref, b1_ref, w2_ref, pe_ref, gk_ref, o_ref = refs[PAGES_PER_STEP + 1:]

    def flat(pg, c):
        return jnp.concatenate([pg[pl.ds(p * PAGE_STREAMS + c, PAGE_HALVES, stride=NSA_CMP_STRIDE * PAGE_STREAMS), :]
                                for p in range(NSA_CMP_STRIDE)], axis=1)

    def x_of(c):
        return jnp.concatenate([flat(pg, c) for pg in pages], axis=0), flat(nxt, c)

    _compress_body(x_of, w1_ref, b1_ref, w2_ref, pe_ref, gk_ref, o_ref, n_cmp)


def _compress_weights(w1, b1, w2, pe, gk):
    half = NSA_CMP_STRIDE * HEAD_DIM
    w1r = w1.reshape(2, 2, half, NSA_CMP_HIDDEN)
    w1ab = jnp.concatenate([w1r[:, 0], w1r[:, 1]], axis=-1)
    pe8 = jnp.broadcast_to(pe.reshape(2, 1, 2 * half), (2, 8, 2 * half))
    args = (w1ab.astype(BF16), b1.reshape(2, 1, NSA_CMP_HIDDEN), w2.astype(BF16), pe8.astype(BF16),
            gk.reshape(1, HEAD_DIM))
    return args, [a.shape for a in args]


def _compress(kv_flat, batch, w1, b1, w2, pe, gk):
    n_half = kv_flat.shape[0] // batch // NSA_CMP_STRIDE
    nblk = n_half // CMP_ROWS
    width = NSA_CMP_STRIDE * kv_flat.shape[1]
    r = kv_flat.reshape(batch * n_half, width)
    last8 = batch * n_half // 8 - 1
    wargs, wshapes = _compress_weights(w1, b1, w2, pe, gk)
    full = lambda shape: pl.BlockSpec(shape, lambda b, i: (0,) * len(shape))
    return pl.pallas_call(
        functools.partial(_compress_kernel, n_cmp=n_half - 1),
        grid=(batch, nblk),
        in_specs=[pl.BlockSpec((CMP_ROWS, width), lambda b, i: (b * nblk + i, 0)),
                  pl.BlockSpec((8, width), lambda b, i: (jnp.minimum((b * nblk + i + 1) * (CMP_ROWS // 8), last8), 0))]
                 + [full(s) for s in wshapes],
        out_specs=pl.BlockSpec((None, 2, KV_HEADS, CMP_ROWS, HEAD_DIM), lambda b, i: (b, 0, 0, i, 0)),
        out_shape=jax.ShapeDtypeStruct((batch, 2, KV_HEADS, n_half, HEAD_DIM), F32),
        compiler_params=_params(("parallel", "arbitrary"), 48),
    )(r, r, *wargs)


def _page_view(pool):
    depth, n_phys = pool.shape[:2]
    return pool.reshape(depth, n_phys, PAGE_SIZE * PAGE_STREAMS, HEAD_DIM)


def _compress_paged(pool, layer, page_table, w1, b1, w2, pe, gk):
    view = _page_view(pool)
    batch, n_pages = page_table.shape
    groups = n_pages // PAGES_PER_STEP
    n_half = n_pages * PAGE_HALVES
    wargs, wshapes = _compress_weights(w1, b1, w2, pe, gk)
    page = lambda fn: pl.BlockSpec((None, None, PAGE_SIZE * PAGE_STREAMS, HEAD_DIM), fn)
    in_specs = [page(lambda b, g, pt, r=r: (layer, pt[b, g * PAGES_PER_STEP + r], 0, 0)) for r in range(PAGES_PER_STEP)]
    in_specs.append(page(lambda b, g, pt: (layer, pt[b, jnp.minimum((g + 1) * PAGES_PER_STEP, n_pages - 1)], 0, 0)))
    in_specs += [pl.BlockSpec(s, lambda b, g, pt, n=len(s): (0,) * n) for s in wshapes]
    return pl.pallas_call(
        functools.partial(_compress_paged_kernel, n_cmp=n_half - 1),
        grid_spec=pltpu.PrefetchScalarGridSpec(
            num_scalar_prefetch=1, grid=(batch, groups), in_specs=in_specs,
            out_specs=pl.BlockSpec((None, 2, KV_HEADS, CMP_ROWS, HEAD_DIM), lambda b, g, pt: (b, 0, 0, g, 0))),
        out_shape=jax.ShapeDtypeStruct((batch, 2, KV_HEADS, n_half, HEAD_DIM), F32),
        compiler_params=_params(("parallel", "arbitrary"), 48),
    )(page_table, *([view] * (PAGES_PER_STEP + 1)), *wargs)


def _stack_heads(q):
    return jnp.concatenate([q[:, g * HEAD_DIM:(g + 1) * HEAD_DIM] for g in range(GROUP)], axis=0)


def _block_onehot(n_blocks, kpos, shift):
    blk = lax.broadcasted_iota(jnp.int32, (n_blocks, 1), 0)
    return jnp.where((kpos >> shift) == blk, 1.0, 0.0).astype(BF16)


def _nsa_kernel(*refs, tq, qpos0, n_sel, extents, has_tail, win_pos0, win_len):
    if len(extents) > 1:
        refs, osel_ref = refs[:-1], refs[-1]
    if has_tail:
        q_ref, ga_ref, kc_ref, vc_ref, ks_ref, vs_ref, kt_ref, vt_ref, kw_ref, vw_ref, o_ref = refs
    else:
        q_ref, ga_ref, kc_ref, vc_ref, ks_ref, vs_ref, kw_ref, vw_ref, o_ref = refs
    q0 = qpos0 + pl.program_id(2) * tq
    rows = GROUP * tq
    q4 = _stack_heads(q_ref[...]).astype(BF16)
    qpos = q0 + (lax.broadcasted_iota(jnp.int32, (rows, 1), 0) & (tq - 1))

    kc = kc_ref[...].astype(BF16)
    vc = vc_ref[...].astype(BF16)
    n_cp = kc.shape[0]
    cidx = lax.broadcasted_iota(jnp.int32, (1, n_cp), 1)
    s_cmp = _dot_nt(q4, kc) * ATT_SCALE
    (e_cmp,), den = _softmax_parts([s_cmp], [(cidx * NSA_CMP_STRIDE + (NSA_CMP_BLOCK - 1)) <= qpos])
    p_cmp = e_cmp / den
    o_cmp = _dot(p_cmp.astype(BF16), vc)

    n_sp = -(-n_sel // LANE) * LANE
    psum = functools.reduce(lambda a, b: a + b, [p_cmp[g * tq:(g + 1) * tq] for g in range(GROUP)])
    ci = lax.broadcasted_iota(jnp.int32, (n_cp, 1), 0) * NSA_CMP_STRIDE
    sj = lax.broadcasted_iota(jnp.int32, (1, n_sp), 1) * NSA_SEL_BLOCK
    cover = jnp.where((ci <= sj + (NSA_SEL_BLOCK - 1)) & (ci + (NSA_CMP_BLOCK - 1) >= sj), 1.0, 0.0).astype(BF16)
    imp = _dot_f32_lhs(psum, cover)
    col = lax.broadcasted_iota(jnp.int32, (1, n_sp), 1)
    qpos_t = q0 + lax.broadcasted_iota(jnp.int32, (tq, 1), 0)
    back = (qpos_t >> 6) - col
    valid = back >= 0
    forced = (col == 0) | (valid & (back < NSA_N_LOCAL))
    score = jnp.where(valid, jnp.where(forced, FORCE_SCORE, imp), NEG_INF)
    rank = _rank(score, n_sel, col)
    sel = jnp.where((rank < NSA_SEL_TOP) & (score > 0.5 * NEG_INF), 1.0, 0.0).astype(BF16)
    sel4 = jnp.concatenate([sel] * GROUP, axis=0)

    def sel_attend(n_keys):
        s_list, m_list, v_list = [], [], []

        def add_tile(k_bf, v_bf, kpos):
            picked = _dot(sel4, _block_onehot(n_sp, kpos, 6))
            s_list.append(_dot_nt(q4, k_bf) * ATT_SCALE)
            m_list.append(jnp.where(kpos <= qpos, picked, 0.0) > 0.5)
            v_list.append(v_bf)

        kt = min(n_keys, KEY_TILE)
        for t0 in range(0, n_keys, kt):
            add_tile(ks_ref[t0:t0 + kt, :].astype(BF16), vs_ref[t0:t0 + kt, :].astype(BF16),
                     t0 + lax.broadcasted_iota(jnp.int32, (1, kt), 1))
        if has_tail:
            add_tile(kt_ref[...].astype(BF16), vt_ref[...].astype(BF16),
                     n_keys + lax.broadcasted_iota(jnp.int32, (1, TAIL_ROWS), 1))
        e_list, den = _softmax_parts(s_list, m_list, never_empty=True)
        return functools.reduce(lambda a, b: a + b, [_dot(e.astype(BF16), v) for e, v in zip(e_list, v_list)]) / den

    if len(extents) == 1:
        o_sel = sel_attend(extents[0])
    else:
        need = (q0 - qpos0 + tq - 1) // extents[0]
        for idx, ext in enumerate(extents):
            @pl.when(need == idx)
            def _(ext=ext):
                osel_ref[...] = sel_attend(ext)
        o_sel = osel_ref[...]

    start = jnp.clip(q0 - NSA_WINDOW - win_pos0, 0, win_len - WIN_BAND)
    start = pl.multiple_of(start, LANE)
    kw = kw_ref[pl.ds(start, WIN_BAND), :].astype(BF16)
    vw = vw_ref[pl.ds(start, WIN_BAND), :].astype(BF16)
    kposw = win_pos0 + start + lax.broadcasted_iota(jnp.int32, (1, WIN_BAND), 1)
    s_win = _dot_nt(q4, kw) * ATT_SCALE
    m_win = jnp.where(kposw <= qpos, qpos - kposw, NSA_WINDOW) < NSA_WINDOW
    (e_win,), den = _softmax_parts([s_win], [m_win], never_empty=True)
    o_win = _dot(e_win.astype(BF16), vw) / den

    ga = ga_ref[...]
    ga = jnp.where(pl.program_id(1) == 0, ga, pltpu.roll(ga, LANE - GATES_PER_KV, 1))
    outs = []
    for g in range(GROUP):
        r = slice(g * tq, (g + 1) * tq)
        outs.append(ga[:, 3 * g:3 * g + 1] * o_cmp[r] + ga[:, 3 * g + 1:3 * g + 2] * o_sel[r]
                    + ga[:, 3 * g + 2:3 * g + 3] * o_win[r])
    o_ref[...] = jnp.concatenate(outs, axis=1).astype(o_ref.dtype)


CAUSAL_STEP = 512


def _causal_extents(tk, tq):
    if tk % CAUSAL_STEP or CAUSAL_STEP % tq:
        return (tk,)
    return tuple(range(CAUSAL_STEP, tk + 1, CAUSAL_STEP))


def _nsa(zq, tq_total, tq, qpos0, cmp_kv, sel_main, sel_cols, sel_tail, win, win_cols, win_pos0, n_total,
         causal_skip):
    batch = cmp_kv.shape[0]
    nq = tq_total // tq
    n_cp = cmp_kv.shape[3]
    tk = sel_main.shape[0] // batch
    lw = win.shape[0] // batch
    n_sel = -(-n_total // NSA_SEL_BLOCK)
    extents = _causal_extents(tk, tq) if causal_skip else (tk,)
    scratch = [pltpu.VMEM((GROUP * tq, HEAD_DIM), F32)] if len(extents) > 1 else []
    qa_blk = SEG["qa"] // (GROUP * HEAD_DIM)
    ga_blk = SEG["ga"] // LANE
    in_specs = [
        pl.BlockSpec((tq, GROUP * HEAD_DIM), lambda b, k, i: (b * nq + i, qa_blk + k)),
        pl.BlockSpec((tq, LANE), lambda b, k, i: (b * nq + i, ga_blk)),
        pl.BlockSpec((None, None, None, n_cp, HEAD_DIM), lambda b, k, i: (b, 0, k, 0, 0)),
        pl.BlockSpec((None, None, None, n_cp, HEAD_DIM), lambda b, k, i: (b, 1, k, 0, 0)),
        pl.BlockSpec((tk, HEAD_DIM), lambda b, k, i: (b, sel_cols[0] + k)),
        pl.BlockSpec((tk, HEAD_DIM), lambda b, k, i: (b, sel_cols[1] + k)),
    ]
    args = [zq, zq, cmp_kv, cmp_kv, sel_main, sel_main]
    if sel_tail is not None:
        in_specs += [pl.BlockSpec((TAIL_ROWS, HEAD_DIM), lambda b, k, i: (b, k)),
                     pl.BlockSpec((TAIL_ROWS, HEAD_DIM), lambda b, k, i: (b, KV_HEADS + k))]
        args += [sel_tail, sel_tail]
    in_specs += [pl.BlockSpec((lw, HEAD_DIM), lambda b, k, i: (b, win_cols[0] + k)),
                 pl.BlockSpec((lw, HEAD_DIM), lambda b, k, i: (b, win_cols[1] + k))]
    args += [win, win]
    return pl.pallas_call(
        functools.partial(_nsa_kernel, tq=tq, qpos0=qpos0, n_sel=n_sel, extents=extents,
                          has_tail=sel_tail is not None, win_pos0=win_pos0, win_len=lw),
        grid=(batch, KV_HEADS, nq),
        in_specs=in_specs,
        out_specs=pl.BlockSpec((tq, GROUP * HEAD_DIM), lambda b, k, i: (b * nq + i, k)),
        out_shape=jax.ShapeDtypeStruct((batch * tq_total, WIDTH), BF16),
        scratch_shapes=scratch,
        compiler_params=_params(("parallel", "parallel", "arbitrary"), 56),
    )(*args)


def _moba_kernel(*refs, tq, qpos0, extents, has_tail):
    if has_tail:
        q_ref, k_ref, v_ref, kt_ref, vt_ref, o_ref = refs
    else:
        q_ref, k_ref, v_ref, o_ref = refs
    q0 = qpos0 + pl.program_id(2) * tq
    rows = GROUP * tq
    q4f = _stack_heads(q_ref[...])
    q4 = q4f.astype(BF16)
    qpos = q0 + (lax.broadcasted_iota(jnp.int32, (rows, 1), 0) & (tq - 1))
    cur = qpos >> 8
    col = lax.broadcasted_iota(jnp.int32, (1, LANE), 1)

    def attend(n_keys):
        kt = min(n_keys, KEY_TILE)
        n_blk = n_keys // MOBA_BLOCK
        means = [jnp.sum(k_ref[t0:t0 + kt, :].reshape(kt // MOBA_BLOCK, MOBA_BLOCK, HEAD_DIM), axis=1)
                 * (1.0 / MOBA_BLOCK) for t0 in range(0, n_keys, kt)]
        if n_blk < LANE:
            means.append(jnp.zeros((LANE - n_blk, HEAD_DIM), F32))
        kmean = jnp.concatenate(means, axis=0)
        gate = jnp.where(col < cur, _dot_nt_f32(q4f, kmean), NEG_INF)
        rank = _rank(gate, n_blk, col)
        sel = jnp.where((rank < MOBA_TOP) & (gate > 0.5 * NEG_INF), 1.0, 0.0).astype(BF16)

        s_list, m_list, v_list = [], [], []

        def add_tile(k_bf, v_bf, kpos):
            picked = _dot(sel, _block_onehot(LANE, kpos, 8))
            own = jnp.where((kpos >> 8) == cur, jnp.where(kpos <= qpos, 1.0, 0.0), 0.0)
            s_list.append(_dot_nt(q4, k_bf) * ATT_SCALE)
            m_list.append((picked + own) > 0.5)
            v_list.append(v_bf)

        for t0 in range(0, n_keys, kt):
            add_tile(k_ref[t0:t0 + kt, :].astype(BF16), v_ref[t0:t0 + kt, :].astype(BF16),
                     t0 + lax.broadcasted_iota(jnp.int32, (1, kt), 1))
        if has_tail:
            add_tile(kt_ref[...].astype(BF16), vt_ref[...].astype(BF16),
                     n_keys + lax.broadcasted_iota(jnp.int32, (1, TAIL_ROWS), 1))
        e_list, den = _softmax_parts(s_list, m_list, never_empty=True)
        o = functools.reduce(lambda a, b: a + b, [_dot(e.astype(BF16), v) for e, v in zip(e_list, v_list)]) / den
        o_ref[...] = jnp.concatenate([o[g * tq:(g + 1) * tq] for g in range(GROUP)], axis=1).astype(o_ref.dtype)

    if len(extents) == 1:
        attend(extents[0])
    else:
        need = (q0 - qpos0 + tq - 1) // extents[0]
        for idx, ext in enumerate(extents):
            @pl.when(need == idx)
            def _(ext=ext):
                attend(ext)


def _moba(zq, tq_total, tq, qpos0, batch, main, cols, tail, causal_skip):
    nq = tq_total // tq
    tk = main.shape[0] // batch
    extents = _causal_extents(tk, tq) if causal_skip else (tk,)
    qc_blk = SEG["qc"] // (GROUP * HEAD_DIM)
    in_specs = [pl.BlockSpec((tq, GROUP * HEAD_DIM), lambda b, k, i: (b * nq + i, qc_blk + k)),
                pl.BlockSpec((tk, HEAD_DIM), lambda b, k, i: (b, cols[0] + k)),
                pl.BlockSpec((tk, HEAD_DIM), lambda b, k, i: (b, cols[1] + k))]
    args = [zq, main, main]
    if tail is not None:
        in_specs += [pl.BlockSpec((TAIL_ROWS, HEAD_DIM), lambda b, k, i: (b, k)),
                     pl.BlockSpec((TAIL_ROWS, HEAD_DIM), lambda b, k, i: (b, KV_HEADS + k))]
        args += [tail, tail]
    return pl.pallas_call(
        functools.partial(_moba_kernel, tq=tq, qpos0=qpos0, extents=extents, has_tail=tail is not None),
        grid=(batch, KV_HEADS, nq),
        in_specs=in_specs,
        out_specs=pl.BlockSpec((tq, GROUP * HEAD_DIM), lambda b, k, i: (b * nq + i, k)),
        out_shape=jax.ShapeDtypeStruct((batch * tq_total, WIDTH), BF16),
        compiler_params=_params(("parallel", "parallel", "arbitrary"), 56),
    )(*args)


def _layer_norm(v, g, b, eps=1e-5):
    mu = jnp.mean(v, axis=-1, keepdims=True)
    var = jnp.mean(jnp.square(v - mu), axis=-1, keepdims=True)
    return (v - mu) * lax.rsqrt(var + eps) * g + b


def _gmlp_kernel(u_ref, v_ref, ws_ref, bst_ref, g_ref, b_ref, o_ref):
    vn = _layer_norm(v_ref[...], g_ref[...], b_ref[...]).astype(BF16)
    r = lax.broadcasted_iota(jnp.int32, (GMLP_CHUNK, GMLP_CHUNK), 0)
    c = lax.broadcasted_iota(jnp.int32, (GMLP_CHUNK, GMLP_CHUNK), 1)
    bst = bst_ref[...]
    for g in range(N_HEADS):
        sl = slice(g * HEAD_DIM, (g + 1) * HEAD_DIM)
        wm = jnp.where(c <= r, ws_ref[g], 0.0).astype(BF16)
        sv = _dot(wm, vn[:, sl]) + bst[:, g:g + 1]
        o_ref[:, sl] = (u_ref[:, sl] * sv).astype(o_ref.dtype)


def _gmlp(z, ws, bs, ln_g, ln_b):
    m = z.shape[0]
    ub, vb = SEG["u"] // WIDTH, SEG["v"] // WIDTH
    full = lambda shape: pl.BlockSpec(shape, lambda i: (0,) * len(shape))
    return pl.pallas_call(
        _gmlp_kernel,
        grid=(m // GMLP_CHUNK,),
        in_specs=[pl.BlockSpec((GMLP_CHUNK, WIDTH), lambda i: (i, ub)),
                  pl.BlockSpec((GMLP_CHUNK, WIDTH), lambda i: (i, vb)),
                  full(ws.shape), full((GMLP_CHUNK, N_HEADS)), full((1, WIDTH)), full((1, WIDTH))],
        out_specs=pl.BlockSpec((GMLP_CHUNK, WIDTH), lambda i: (i, 0)),
        out_shape=jax.ShapeDtypeStruct((m, WIDTH), BF16),
        compiler_params=_params(("parallel",), 32),
    )(z, z, ws, bs.T, ln_g.reshape(1, WIDTH), ln_b.reshape(1, WIDTH))


def _hgrn_gates(fd, lb):
    f_gate = lb + (1.0 - lb) * _sigmoid(fd)
    return jnp.log(jnp.maximum(f_gate, TINY)), (1.0 - lb) * _sigmoid(-fd)


def _hgrn_readout(o, gd, g_out):
    return _rms(o, g_out) * _silu(gd)


HGRN_HEADS_PER_STEP = 4


def _hgrn_kernel(q_ref, f_ref, v_ref, gd_ref, lb_ref, go_ref, o_ref, s_ref):
    c = HGRN_CHUNK
    t_total = q_ref.shape[0]
    g_out = go_ref[...]
    ri = lax.broadcasted_iota(jnp.int32, (c, c), 0)
    ci = lax.broadcasted_iota(jnp.int32, (c, c), 1)
    tri = jnp.where(ci <= ri, 1.0, 0.0).astype(BF16)
    lane = lax.broadcasted_iota(jnp.int32, (1, c), 1)
    diag_mask = (ci <= ri) & ((ci >> 3) == (ri >> 3))

    def head_chunk(r0, hh, st):
        hs = slice(hh * HEAD_DIM, (hh + 1) * HEAD_DIM)
        q = q_ref[pl.ds(r0, c), hs]
        v = v_ref[pl.ds(r0, c), hs]
        lf, k = _hgrn_gates(f_ref[pl.ds(r0, c), hs], lb_ref[:, hs])
        cum = _dot_f32_rhs(tri, lf)
        a = jnp.zeros((c, c), F32)
        for bs in (32, 16, 8):
            nb = c // bs
            refq = jnp.concatenate(
                [jnp.zeros((bs, HEAD_DIM), F32)]
                + [jnp.broadcast_to(cum[b * bs - 1:b * bs], (bs, HEAD_DIM)) for b in range(1, nb)], axis=0)
            refk = jnp.concatenate(
                [jnp.broadcast_to(cum[(b + 1) * bs - 1:(b + 1) * bs], (bs, HEAD_DIM)) for b in range(nb)], axis=0)
            qt = (q * jnp.exp(jnp.minimum(cum - refq, 0.0))).astype(BF16)
            kt = (k * jnp.exp(jnp.minimum(refk - cum, 0.0))).astype(BF16)
            sh = bs.bit_length() - 1
            lvl = (((ri >> sh) & 1) == 1) & ((ci >> sh) == (ri >> sh) - 1)
            a = a + jnp.where(lvl, _dot_nt(qt, kt), 0.0)
        rows = []
        for blk in range(c // 8):
            b0 = blk * 8
            qb, kb, cb = q[b0:b0 + 8], k[b0:b0 + 8], cum[b0:b0 + 8]
            acc = jnp.zeros((8, c), F32)
            for s in range(8):
                w = qb * kb[s:s + 1] * jnp.exp(jnp.minimum(cb - cb[s:s + 1], 0.0))
                acc = jnp.where(lane == b0 + s, jnp.sum(w, axis=1, keepdims=True), acc)
            rows.append(acc)
        a = a + jnp.where(diag_mask, jnp.concatenate(rows, axis=0), 0.0)
        o = _dot(a.astype(BF16), v.astype(BF16)) + _dot_nt((q * jnp.exp(cum)).astype(BF16), st.astype(BF16))
        o_ref[pl.ds(r0, c), hs] = _hgrn_readout(o, gd_ref[pl.ds(r0, c), hs], g_out).astype(o_ref.dtype)
        last = cum[c - 1:c]
        kk = (k * jnp.exp(last - cum)).astype(BF16)
        return st * jnp.exp(last) + _dot_tn(v.astype(BF16), kk)

    def chunk(n, sts):
        r0 = pl.multiple_of(n * c, c)
        return tuple(head_chunk(r0, hh, st) for hh, st in enumerate(sts))

    zero = jnp.zeros((HEAD_DIM, HEAD_DIM), F32)
    sts = lax.fori_loop(0, t_total // c, chunk, (zero,) * HGRN_HEADS_PER_STEP, unroll=2)
    for hh, st in enumerate(sts):
        s_ref[hh] = st.T


def _hgrn(z, batch, lb, g_out):
    t = z.shape[0] // batch
    hw = HGRN_HEADS_PER_STEP * HEAD_DIM
    blk = lambda name: SEG[name] // hw
    col = lambda name: pl.BlockSpec((t, hw), lambda b, h: (b, blk(name) + h))
    return pl.pallas_call(
        _hgrn_kernel,
        grid=(batch, N_HEADS // HGRN_HEADS_PER_STEP),
        in_specs=[col("qd"), col("fd"), col("vd"), col("gd"),
                  pl.BlockSpec((1, hw), lambda b, h: (0, h)),
                  pl.BlockSpec((1, HEAD_DIM), lambda b, h: (0, 0))],
        out_specs=[pl.BlockSpec((t, hw), lambda b, h: (b, h)),
                   pl.BlockSpec((None, HGRN_HEADS_PER_STEP, HEAD_DIM, HEAD_DIM), lambda b, h: (b, h, 0, 0))],
        out_shape=[jax.ShapeDtypeStruct((batch * t, WIDTH), BF16),
                   jax.ShapeDtypeStruct((batch, N_HEADS, HEAD_DIM, HEAD_DIM), F32)],
        compiler_params=_params(("parallel", "parallel"), 48),
    )(z, z, z, z, lb.reshape(1, WIDTH), g_out.reshape(1, HEAD_DIM))


def _gmlp_step_kernel(u_ref, v_ref, w_ref, b_ref, g_ref, bb_ref, o_ref, vn_ref):
    vn = _layer_norm(v_ref[...], g_ref[...], bb_ref[...])
    vn_ref[...] = vn
    o_ref[...] = (u_ref[...] * (w_ref[...] * vn + b_ref[...])).astype(o_ref.dtype)


def _gmlp_step(u, v, ws, bs, ln_g, ln_b):
    rows = u.shape[0]
    w_row = jnp.repeat(ws[:, 0, 0], HEAD_DIM).reshape(1, WIDTH)
    b_row = jnp.repeat(bs[:, 0], HEAD_DIM).reshape(1, WIDTH)
    return pl.pallas_call(
        _gmlp_step_kernel,
        out_shape=[jax.ShapeDtypeStruct((rows, WIDTH), BF16), jax.ShapeDtypeStruct((rows, WIDTH), F32)],
    )(u, v, w_row, b_row, ln_g.reshape(1, WIDTH), ln_b.reshape(1, WIDTH))


def _hgrn_step_kernel(qc_ref, fc_ref, lbc_ref, v_ref, gd_ref, go_ref, s0_ref, o_ref, s_ref):
    lf, k = _hgrn_gates(fc_ref[...], lbc_ref[...])
    s_new = jnp.exp(lf) * s0_ref[...] + k * v_ref[...]
    s_ref[...] = s_new
    o = jnp.sum(qc_ref[...] * s_new, axis=0, keepdims=True)
    o_ref[...] = _hgrn_readout(o, gd_ref[...], go_ref[...]).astype(o_ref.dtype)


def _hgrn_step(qd, fd, vd, gd, lb, g_out, state, layer):
    batch = qd.shape[0]
    colv = lambda a: a.reshape(batch, N_HEADS, HEAD_DIM, 1)
    rowv = lambda a: a.reshape(batch, N_HEADS, 1, HEAD_DIM)
    cspec = pl.BlockSpec((None, None, HEAD_DIM, 1), lambda b, h: (b, h, 0, 0))
    rspec = pl.BlockSpec((None, None, 1, HEAD_DIM), lambda b, h: (b, h, 0, 0))
    return pl.pallas_call(
        _hgrn_step_kernel,
        grid=(batch, N_HEADS),
        in_specs=[cspec, cspec, pl.BlockSpec((None, HEAD_DIM, 1), lambda b, h: (h, 0, 0)), rspec, rspec,
                  pl.BlockSpec((1, HEAD_DIM), lambda b, h: (0, 0)),
                  pl.BlockSpec((None, None, None, HEAD_DIM, HEAD_DIM), lambda b, h: (layer, b, h, 0, 0))],
        out_specs=[rspec, pl.BlockSpec((None, None, HEAD_DIM, HEAD_DIM), lambda b, h: (b, h, 0, 0))],
        out_shape=[jax.ShapeDtypeStruct((batch, N_HEADS, 1, HEAD_DIM), BF16),
                   jax.ShapeDtypeStruct((batch, N_HEADS, HEAD_DIM, HEAD_DIM), F32)],
        compiler_params=_params(("parallel", "parallel"), 32),
    )(colv(qd), colv(fd), lb.reshape(N_HEADS, HEAD_DIM, 1), rowv(vd), rowv(gd), g_out.reshape(1, HEAD_DIM), state)


GATHER_PAGES = 32


def _gather_kernel(pt_ref, *refs):
    o_ref = refs[-1]
    for r in range(GATHER_PAGES):
        for c in range(PAGE_STREAMS):
            o_ref[r * PAGE_SIZE:(r + 1) * PAGE_SIZE, c * HEAD_DIM:(c + 1) * HEAD_DIM] = (
                refs[r][pl.ds(c, PAGE_SIZE, stride=PAGE_STREAMS), :])


def _gather_pages(pool, layer, page_table):
    view = _page_view(pool)
    width = PAGE_STREAMS * HEAD_DIM
    batch, n_pages = page_table.shape
    groups = n_pages // GATHER_PAGES
    rows = GATHER_PAGES * PAGE_SIZE
    in_specs = [pl.BlockSpec((None, None, PAGE_SIZE * PAGE_STREAMS, HEAD_DIM),
                             lambda b, g, pt, r=r: (layer, pt[b, g * GATHER_PAGES + r], 0, 0))
                for r in range(GATHER_PAGES)]
    return pl.pallas_call(
        _gather_kernel,
        grid_spec=pltpu.PrefetchScalarGridSpec(
            num_scalar_prefetch=1, grid=(batch, groups), in_specs=in_specs,
            out_specs=pl.BlockSpec((rows, width), lambda b, g, pt: (b * groups + g, 0))),
        out_shape=jax.ShapeDtypeStruct((batch * n_pages * PAGE_SIZE, width), pool.dtype),
        compiler_params=_params(("parallel", "arbitrary"), 40),
    )(page_table, *([view] * GATHER_PAGES))


def _ffn(xp, xs, mod_p, mod_s, sub, mod_idx, g, wg, wu, wd, layer, rows_per_group):
    act = _norm_glu(xp, xs, g, mod_p, mod_s, mod_idx, wg, wu, layer, sub, rows_per_group)
    return _down(act, wd, layer, sub, xp, xs, mod_p[:, mod_idx, 2], mod_s[:, mod_idx, 2], 0.5)


def _kv5(a, batch):
    return a.reshape(batch, -1, 2, KV_HEADS, HEAD_DIM)


def _cache_rows_kernel(*refs):
    z_refs, o_ref = refs[:-1], refs[-1]
    rows = z_refs[0].shape[0]
    for li, z_ref in enumerate(z_refs):
        @pl.when(pl.program_id(0) == li)
        def _(z_ref=z_ref):
            for c in range(PAGE_STREAMS):
                o_ref[pl.ds(c, rows, stride=PAGE_STREAMS), :] = z_ref[:, c * HEAD_DIM:(c + 1) * HEAD_DIM]


def _cache_rows(z_layers, name, batch, keep):
    depth = len(z_layers)
    t = z_layers[0].shape[0] // batch
    rows = min(keep, ROW_TILE)
    per_batch, first = keep // rows, (t - keep) // rows
    n_steps = batch * per_batch
    width = PAGE_STREAMS * HEAD_DIM
    blk = SEG[name] // width

    def z_spec(li):
        def index(l, s):
            s_eff = jnp.where(l == li, s, jnp.where(l < li, 0, n_steps - 1))
            return ((s_eff // per_batch) * (t // rows) + first + s_eff % per_batch, blk)
        return pl.BlockSpec((rows, width), index)

    out = pl.pallas_call(
        _cache_rows_kernel,
        grid=(depth, n_steps),
        in_specs=[z_spec(li) for li in range(depth)],
        out_specs=pl.BlockSpec((None, rows * PAGE_STREAMS, HEAD_DIM), lambda l, s: (l, s, 0)),
        out_shape=jax.ShapeDtypeStruct((depth, batch * keep * PAGE_STREAMS, HEAD_DIM), F32),
        compiler_params=_params(("arbitrary", "arbitrary"), 32),
    )(*z_layers)
    return out.reshape(depth, batch, keep, 2, KV_HEADS, HEAD_DIM)


def _prompt_mix(z, batch, mixw):
    t = z.shape[0] // batch
    kvc = z[:, SEG["kvc"]:SEG["kvc"] + PAGE_STREAMS * HEAD_DIM]
    cmp_kv = _compress(kvc, batch, mixw["cw1"], mixw["cb1"], mixw["cw2"], mixw["cpe"], mixw["qkg"][1])
    kb = lambda name: SEG[name] // HEAD_DIM
    o_a = _nsa(z, t, 128, 0, cmp_kv, z, (kb("kvs"), kb("kvs") + KV_HEADS), None,
               z, (kb("kvw"), kb("kvw") + KV_HEADS), 0, t, True)
    o_b = _gmlp(z, mixw["ws"], mixw["bs"], mixw["ln_g"], mixw["ln_b"])
    o_c = _moba(z, t, 128, 0, batch, z, (kb("kvm"), kb("kvm") + KV_HEADS), None, True)
    o_d, s_d = _hgrn(z, batch, mixw["lb"], mixw["out_g"])
    return (o_a, o_b, o_c, o_d), s_d


def _pad_rows(a, batch, rows):
    out = jnp.zeros((batch, rows, a.shape[1]), a.dtype).at[:, 0].set(a)
    return out.reshape(batch * rows, a.shape[1])


def _sample_mix(z_pad, batch, mixw, pool_cmp, pool_sel, pool_moba, win_buf, state, page_table, layer):
    past = page_table.shape[1] * PAGE_SIZE
    z = z_pad[:batch]
    seg = lambda name, w: z[:, SEG[name]:SEG[name] + w]
    kvc, kvs, kvw, kvm = seg("kvc", 512), seg("kvs", 512), seg("kvw", 512), seg("kvm", 512)
    zq = _pad_rows(z, batch, SAMPLE_ROWS)
    dense_sel = _gather_pages(pool_sel, layer, page_table)
    dense_moba = _gather_pages(pool_moba, layer, page_table)
    cmp_kv = _compress_paged(pool_cmp, layer, page_table, mixw["cw1"], mixw["cb1"], mixw["cw2"], mixw["cpe"],
                             mixw["qkg"][1])
    wlen = win_buf.shape[2]
    band = jnp.concatenate([win_buf[layer].reshape(batch, wlen, 512), kvw[:, None, :]], axis=1)
    win = jnp.concatenate([band, jnp.zeros((batch, WIN_BAND - wlen - 1, 512), F32)], axis=1)
    o_a = _nsa(zq, SAMPLE_ROWS, SAMPLE_ROWS, past, cmp_kv, dense_sel, (0, KV_HEADS),
               _pad_rows(kvs, batch, TAIL_ROWS), win.reshape(batch * WIN_BAND, 512), (0, KV_HEADS),
               past - wlen, past + 1, False)
    o_c = _moba(zq, SAMPLE_ROWS, SAMPLE_ROWS, past, batch, dense_moba, (0, KV_HEADS),
                _pad_rows(kvm, batch, TAIL_ROWS), False)
    o_b, v_n = _gmlp_step(seg("u", WIDTH), seg("v", WIDTH), mixw["ws"], mixw["bs"], mixw["ln_g"], mixw["ln_b"])
    o_d, s_d = _hgrn_step(seg("qd", WIDTH), seg("fd", WIDTH), seg("vd", WIDTH), seg("gd", WIDTH),
                          mixw["lb"], mixw["out_g"], state, layer)
    mixed = jnp.concatenate([o_a[::SAMPLE_ROWS], o_b, o_c[::SAMPLE_ROWS], o_d.reshape(batch, WIDTH)], axis=1)
    mixed = jnp.concatenate([mixed, jnp.zeros((z_pad.shape[0] - batch, mixed.shape[1]), mixed.dtype)], axis=0)
    new_state = (_kv5(kvc, batch), _kv5(kvs, batch), _kv5(kvm, batch),
                 band[:, 1:].reshape(batch, wlen, 2, KV_HEADS, HEAD_DIM), s_d, v_n.reshape(batch, 1, WIDTH))
    return mixed, new_state


def kernel(x_prompt, x_sample, c_prompt, c_sample, cache_nsa_cmp_kv, cache_nsa_sel_kv, cache_moba_kv, cache_nsa_win_kv, state_hgrn, page_table, w_ada, b_ada, norm_g, w_ffn_gate, w_ffn_up, w_ffn_down, w_in, w_out, qk_norm_g, nsa_cmp_w1, nsa_cmp_b1, nsa_cmp_w2, nsa_cmp_pos, gmlp_ln_g, gmlp_ln_b, gmlp_ws, gmlp_bs, hgrn_lb_logits, hgrn_out_g):
    bp, t, d = x_prompt.shape
    bs = x_sample.shape[0]
    depth = w_in.shape[0]
    srows = SAMPLE_PAD
    lb_p = jax.nn.softmax(hgrn_lb_logits.astype(F32), axis=0)
    lb_all = jnp.clip(jnp.cumsum(lb_p, axis=0) - lb_p[0:1], 0.0, 1.0)

    c_all = jnp.concatenate([c_prompt, c_sample, jnp.zeros((16 - bp - bs, d), F32)], axis=0)
    mod = _ada_mod(c_all, w_ada, b_ada).reshape(depth, 16, 3, 3, d)

    xp = x_prompt.reshape(bp * t, d)
    xs = jnp.concatenate([x_sample.reshape(bs, d), jnp.zeros((srows - bs, d), F32)], axis=0)
    z_layers, hgrn_p, st_s = [], [], []
    for l in range(depth):
        mod_p = mod[l, :bp]
        mod_s = jnp.concatenate([mod[l, bp:bp + bs], jnp.zeros((srows - bs, 3, 3, d), F32)], axis=0)
        gains = _proj_gains(qk_norm_g[l])
        mixw = dict(qkg=qk_norm_g[l], cw1=nsa_cmp_w1[l], cb1=nsa_cmp_b1[l], cw2=nsa_cmp_w2[l], cpe=nsa_cmp_pos[l],
                    ln_g=gmlp_ln_g[l], ln_b=gmlp_ln_b[l], ws=gmlp_ws[l], bs=gmlp_bs[l], lb=lb_all[l],
                    out_g=hgrn_out_g[l])
        g = norm_g[l]

        xp, xs = _ffn(xp, xs, mod_p, mod_s, 0, 0, g[0], w_ffn_gate, w_ffn_up, w_ffn_down, l, t)
        zp, zs = _proj(xp, xs, g[1], mod_p, mod_s, 1, w_in, l, gains, t)
        mixed_p, sp = _prompt_mix(zp, bp, mixw)
        mixed_s, ss = _sample_mix(zs, bs, mixw, cache_nsa_cmp_kv, cache_nsa_sel_kv, cache_moba_kv,
                                  cache_nsa_win_kv, state_hgrn, page_table, l)
        xp, xs = _out_proj(mixed_p, mixed_s, w_out, l, xp, xs, mod_p[:, 1, 2], mod_s[:, 1, 2], t)
        xp, xs = _ffn(xp, xs, mod_p, mod_s, 1, 2, g[2], w_ffn_gate, w_ffn_up, w_ffn_down, l, t)
        z_layers.append(zp)
        hgrn_p.append(sp)
        st_s.append(ss)

    stack = lambda sts, i: jnp.stack([s[i] for s in sts])
    return (xp.reshape(bp, t, d), xs[:bs].reshape(bs, 1, d),
            _cache_rows(z_layers, "kvc", bp, t), _cache_rows(z_layers, "kvs", bp, t),
            _cache_rows(z_layers, "kvm", bp, t), _cache_rows(z_layers, "kvw", bp, min(NSA_WINDOW, t)),
            jnp.stack(hgrn_p),
            stack(st_s, 0), stack(st_s, 1), stack(st_s, 2), stack(st_s, 3), stack(st_s, 4), stack(st_s, 5))
```

```python
import functools

import jax
import jax.numpy as jnp
from jax import lax
from jax.experimental import pallas as pl
from jax.experimental.pallas import tpu as pltpu

F32 = jnp.float32
BF16 = jnp.bfloat16

HEAD_DIM = 128
N_HEADS = 8
KV_HEADS = 2
GROUP = N_HEADS // KV_HEADS
WIDTH = N_HEADS * HEAD_DIM
PAGE_SIZE = 128
PAGES_PER_STEP = 32

NSA_CMP_STRIDE = 16
NSA_CMP_BLOCK = 32
NSA_CMP_HIDDEN = 256
NSA_SEL_BLOCK = 64
NSA_SEL_TOP = 16
NSA_N_LOCAL = 2
NSA_WINDOW = 512
GMLP_CHUNK = 128
MOBA_BLOCK = 256
MOBA_TOP = 3
HGRN_CHUNK = 64
NEG_INF = -1e30
FORCE_SCORE = 1e4
TINY = 1e-30
ATT_SCALE = HEAD_DIM ** -0.5

LANE = 128
KEY_TILE = 4096
WIN_BAND = NSA_WINDOW + 128
SAMPLE_ROWS = 8
TAIL_ROWS = 256

PROJ_TN = 256
PROJ_TILES_PER_STEP = 2
SEG = {}
_off = 0
for _name, _w in (("qa", 1024), ("u", 1024), ("v", 1024), ("qc", 1024), ("qd", 1024), ("fd", 1024),
                  ("vd", 1024), ("gd", 1024), ("kvc", 512), ("kvs", 512), ("kvw", 512), ("kvm", 512),
                  ("ga", 256), ("pad", 256)):
    SEG[_name] = _off
    _off += _w
N_PROJ = _off

_ORIG = {}
_o = 0
for _name, _w in (("qa", 1024), ("kvc", 512), ("kvs", 512), ("kvw", 512), ("ga", 24), ("u", 1024), ("v", 1024),
                  ("qc", 1024), ("kvm", 512), ("qd", 1024), ("fd", 1024), ("vd", 1024), ("gd", 1024)):
    _ORIG[_name] = (_o, _w)
    _o += _w


def _params(sem, vmem_mb):
    return pltpu.CompilerParams(dimension_semantics=sem, vmem_limit_bytes=vmem_mb << 20)


def _sigmoid(x):
    return 1.0 / (1.0 + jnp.exp(-x))


def _silu(x):
    return x * _sigmoid(x)


def _gelu(x):
    return 0.5 * x * (1.0 + jnp.tanh(0.7978845608028654 * (x + 0.044715 * (x * x * x))))


def _dot(a, b):
    return jnp.dot(a, b, preferred_element_type=F32)


def _dot_nt(a, b):
    return lax.dot_general(a, b, (((1,), (1,)), ((), ())), preferred_element_type=F32)


def _dot_tn(a, b):
    return lax.dot_general(a, b, (((0,), (0,)), ((), ())), preferred_element_type=F32)


def _split3(a):
    hi = a.astype(BF16)
    r1 = a - hi.astype(F32)
    mid = r1.astype(BF16)
    lo = (r1 - mid.astype(F32)).astype(BF16)
    return hi, mid, lo


def _dot_f32_lhs(a, b_exact):
    hi, mid, lo = _split3(a)
    return _dot(hi, b_exact) + _dot(mid, b_exact) + _dot(lo, b_exact)


def _dot_f32_rhs(a_exact, b):
    hi, mid, lo = _split3(b)
    return _dot(a_exact, hi) + _dot(a_exact, mid) + _dot(a_exact, lo)


def _dot_nt_f32(a, b):
    ah, am, _ = _split3(a)
    bh, bm, _ = _split3(b)
    return _dot_nt(ah, bh) + _dot_nt(ah, bm) + _dot_nt(am, bh)


def _rms(x, g, eps=1e-6):
    return x * lax.rsqrt(jnp.mean(x * x, axis=-1, keepdims=True) + eps) * g


def _softmax_parts(s_list, m_list, never_empty=False):
    sm = [jnp.where(m, s, NEG_INF) for s, m in zip(s_list, m_list)]
    mx = functools.reduce(jnp.maximum, [jnp.max(s, axis=-1, keepdims=True) for s in sm])
    if never_empty:
        e = [jnp.exp(s - mx) for s in sm]
    else:
        e = [jnp.where(m, jnp.exp(s - mx), 0.0) for s, m in zip(sm, m_list)]
    den = functools.reduce(lambda a, b: a + b, [jnp.sum(x, axis=-1, keepdims=True) for x in e])
    return e, jnp.maximum(den, 1e-30)


def _rank(score, n, col):
    rank = jnp.zeros(score.shape, jnp.int32)
    for i in range(n):
        si = score[:, i:i + 1]
        before = jnp.where(si > score, 1, jnp.where((si == score) & (col > i), 1, 0))
        rank = rank + before
    return rank


def _ada_kernel(c_ref, w_ref, b_ref, o_ref):
    a = _silu(c_ref[...]).astype(BF16)
    o_ref[...] = _dot(a, w_ref[...].astype(BF16)) + b_ref[...]


def _ada_mod(c, w_ada, b_ada):
    depth, d, n = w_ada.shape
    rows = c.shape[0]
    tn = 512
    return pl.pallas_call(
        _ada_kernel,
        grid=(depth, n // tn),
        in_specs=[pl.BlockSpec((rows, d), lambda l, j: (0, 0)),
                  pl.BlockSpec((None, d, tn), lambda l, j: (l, 0, j)),
                  pl.BlockSpec((None, 1, tn), lambda l, j: (l, 0, j))],
        out_specs=pl.BlockSpec((None, rows, tn), lambda l, j: (l, 0, j)),
        out_shape=jax.ShapeDtypeStruct((depth, rows, n), F32),
        compiler_params=_params(("parallel", "parallel"), 40),
    )(c, w_ada, b_ada.reshape(depth, 1, n))


ROW_TILE = 1024
SAMPLE_PAD = 16
COMBINED_TILE = ROW_TILE + SAMPLE_PAD
COL_TILE = 256
NORM_CHUNK = 128


def _single_buffered(shape, index_map):
    return pl.BlockSpec(shape, index_map, pipeline_mode=pl.Buffered(1))


def _sample_out_spec(tn):
    return pl.BlockSpec((None, SAMPLE_PAD, tn), lambda i, j, *_: (i, 0, j))


def _stage_modulated(lhs_ref, xp_ref, xs_ref, g_ref, scp_ref, shp_ref, scs_ref, shs_ref):
    g = g_ref[...]
    one_sc, sh = 1.0 + scp_ref[...], shp_ref[...]

    def body(c, carry):
        r0 = pl.multiple_of(c * NORM_CHUNK, NORM_CHUNK)
        lhs_ref[pl.ds(r0, NORM_CHUNK), :] = (_rms(xp_ref[pl.ds(r0, NORM_CHUNK), :], g) * one_sc + sh).astype(BF16)
        return carry

    lax.fori_loop(0, ROW_TILE // NORM_CHUNK, body, 0)
    lhs_ref[ROW_TILE:, :] = (_rms(xs_ref[...], g) * (1.0 + scs_ref[...]) + shs_ref[...]).astype(BF16)


def _mod_specs(d, rows_per_group):
    per = rows_per_group // ROW_TILE
    vec_p = pl.BlockSpec((None, 1, d), lambda i, j, *_: (i // per, 0, 0))
    vec_s = pl.BlockSpec((SAMPLE_PAD, d), lambda i, j, *_: (0, 0))
    return [_single_buffered((ROW_TILE, d), lambda i, j, *_: (i, 0)), vec_s,
            pl.BlockSpec((1, d), lambda i, j, *_: (0, 0)), vec_p, vec_p, vec_s, vec_s]


def _mod_args(xp, xs, g, mod_p, mod_s, idx):
    d = xp.shape[1]
    return (xp, xs, g.reshape(1, d), mod_p[:, idx, 1].reshape(-1, 1, d), mod_p[:, idx, 0].reshape(-1, 1, d),
            mod_s[:, idx, 1], mod_s[:, idx, 0])


def _norm_glu_kernel(xp_ref, xs_ref, g_ref, scp_ref, shp_ref, scs_ref, shs_ref, wg_ref, wu_ref, o_ref, lhs_ref):
    @pl.when(pl.program_id(1) == 0)
    def _():
        _stage_modulated(lhs_ref, xp_ref, xs_ref, g_ref, scp_ref, shp_ref, scs_ref, shs_ref)

    h = lhs_ref[...]
    a = _dot(h, wg_ref[...].astype(BF16))
    b = _dot(h, wu_ref[...].astype(BF16))
    o_ref[...] = (_silu(a) * b).astype(o_ref.dtype)


def _norm_glu(xp, xs, g, mod_p, mod_s, idx, wg, wu, layer, sub, rows_per_group):
    m, d = xp.shape
    f = wg.shape[-1]
    nt = m // ROW_TILE
    wspec = pl.BlockSpec((None, None, d, COL_TILE), lambda i, j: (layer, sub, 0, j))
    return pl.pallas_call(
        _norm_glu_kernel,
        grid=(nt, f // COL_TILE),
        in_specs=_mod_specs(d, rows_per_group) + [wspec, wspec],
        out_specs=pl.BlockSpec((COMBINED_TILE, COL_TILE), lambda i, j: (i, j)),
        out_shape=jax.ShapeDtypeStruct((nt * COMBINED_TILE, f), BF16),
        scratch_shapes=[pltpu.VMEM((COMBINED_TILE, d), BF16)],
        compiler_params=_params(("parallel", "arbitrary"), 52),
    )(*_mod_args(xp, xs, g, mod_p, mod_s, idx), wg, wu)


DOWN_TILES = 2


def _down_kernel(x_ref, w_ref, rp_ref, rs_ref, gp_ref, gs_ref, op_ref, os_ref, *, coef):
    acc = _dot(x_ref[...], w_ref[...].astype(BF16))
    gp = coef * gp_ref[...]
    for r in range(DOWN_TILES):
        rows = slice(r * ROW_TILE, (r + 1) * ROW_TILE)
        op_ref[rows, :] = rp_ref[rows, :] + gp * acc[r * COMBINED_TILE:r * COMBINED_TILE + ROW_TILE]
    os_ref[...] = rs_ref[...] + (coef * gs_ref[...]) * acc[ROW_TILE:COMBINED_TILE]


def _down(act, w, layer, sub, xp, xs, gate_p, gate_s, coef):
    m, n = xp.shape
    kdim = act.shape[1]
    tk = kdim // 2
    rows = DOWN_TILES * ROW_TILE
    assert rows == m // gate_p.shape[0]
    for ks in range(2):
        xp, xs = pl.pallas_call(
            functools.partial(_down_kernel, coef=coef),
            grid=(m // rows, n // COL_TILE),
            in_specs=[_single_buffered((DOWN_TILES * COMBINED_TILE, tk), lambda i, j, ks=ks: (i, ks)),
                      pl.BlockSpec((None, None, tk, COL_TILE), lambda i, j, ks=ks: (layer, sub, ks, j)),
                      pl.BlockSpec((rows, COL_TILE), lambda i, j: (i, j)),
                      pl.BlockSpec((SAMPLE_PAD, COL_TILE), lambda i, j: (0, j)),
                      pl.BlockSpec((None, 1, COL_TILE), lambda i, j: (i, 0, j)),
                      pl.BlockSpec((SAMPLE_PAD, COL_TILE), lambda i, j: (0, j))],
            out_specs=[pl.BlockSpec((rows, COL_TILE), lambda i, j: (i, j)), _sample_out_spec(COL_TILE)],
            out_shape=[jax.ShapeDtypeStruct((m, n), F32), jax.ShapeDtypeStruct((m // rows, SAMPLE_PAD, n), F32)],
            compiler_params=_params(("parallel", "arbitrary"), 56),
        )(act, w, xp, xs, gate_p.reshape(-1, 1, n), gate_s)
        xs = xs[0]
    return xp, xs


def _out_proj_kernel(*refs, n_parts):
    x_refs = refs[:n_parts]
    xs_ref, w_ref, rp_ref, rs_ref, gp_ref, gs_ref, op_ref, os_ref, lhs_ref = refs[n_parts:]

    @pl.when(pl.program_id(1) == 0)
    def _():
        off = 0
        for x_ref in x_refs:
            lhs_ref[:ROW_TILE, off:off + x_ref.shape[1]] = x_ref[...]
            off += x_ref.shape[1]
        lhs_ref[ROW_TILE:, :] = xs_ref[...]

    acc = _dot(lhs_ref[...], w_ref[...].astype(BF16))
    op_ref[...] = rp_ref[...] + gp_ref[...] * acc[:ROW_TILE]
    os_ref[...] = rs_ref[...] + gs_ref[...] * acc[ROW_TILE:]


def _out_proj(parts, mixed_s, w, layer, xp, xs, gate_p, gate_s, rows_per_group):
    m, n = xp.shape
    kdim = w.shape[1]
    per = rows_per_group // ROW_TILE
    in_specs = [pl.BlockSpec((ROW_TILE, x.shape[1]), lambda i, j: (i, 0)) for x in parts]
    in_specs += [pl.BlockSpec((SAMPLE_PAD, kdim), lambda i, j: (0, 0)),
                 pl.BlockSpec((None, kdim, COL_TILE), lambda i, j: (layer, 0, j)),
                 pl.BlockSpec((ROW_TILE, COL_TILE), lambda i, j: (i, j)),
                 pl.BlockSpec((SAMPLE_PAD, COL_TILE), lambda i, j: (0, j)),
                 pl.BlockSpec((None, 1, COL_TILE), lambda i, j: (i // per, 0, j)),
                 pl.BlockSpec((SAMPLE_PAD, COL_TILE), lambda i, j: (0, j))]
    xp, xs = pl.pallas_call(
        functools.partial(_out_proj_kernel, n_parts=len(parts)),
        grid=(m // ROW_TILE, n // COL_TILE),
        in_specs=in_specs,
        out_specs=[pl.BlockSpec((ROW_TILE, COL_TILE), lambda i, j: (i, j)), _sample_out_spec(COL_TILE)],
        out_shape=[jax.ShapeDtypeStruct((m, n), F32), jax.ShapeDtypeStruct((m // ROW_TILE, SAMPLE_PAD, n), F32)],
        scratch_shapes=[pltpu.VMEM((COMBINED_TILE, kdim), BF16)],
        compiler_params=_params(("parallel", "arbitrary"), 48),
    )(*parts, mixed_s, w, xp, xs, gate_p.reshape(-1, 1, n), gate_s)
    return xp, xs[0]


def _tiles(name, width):
    a = SEG[name] // PROJ_TN
    return a, a + width // PROJ_TN


_NORM_TILES = (_tiles("qa", 1024), _tiles("qc", 1024), _tiles("kvs", 256), _tiles("kvw", 256), _tiles("kvm", 256))
_GELU_TILES = (_tiles("u", 2048),)
_SILU_TILES = (_tiles("qd", 1024),)
_SIGM_TILES = (_tiles("ga", 256),)


def _in_ranges(j, ranges):
    return functools.reduce(jnp.logical_or, [(j >= a) & (j < b) for a, b in ranges])


def _proj_kernel(off_ref, xp_ref, xs_ref, g_ref, scp_ref, shp_ref, scs_ref, shs_ref, w_ref, gain_ref, zp_ref, zs_ref,
                 lhs_ref):
    j = pl.program_id(1)

    @pl.when(j == 0)
    def _():
        _stage_modulated(lhs_ref, xp_ref, xs_ref, g_ref, scp_ref, shp_ref, scs_ref, shs_ref)

    lhs = lhs_ref[...]
    accs = [_dot_nt(lhs, w_ref[0, c * PROJ_TN:(c + 1) * PROJ_TN, :].astype(BF16)) for c in range(PROJ_TILES_PER_STEP)]
    for c, acc in enumerate(accs):
        _proj_epilogue(j * PROJ_TILES_PER_STEP + c, acc, gain_ref, zp_ref, zs_ref,
                       slice(c * PROJ_TN, (c + 1) * PROJ_TN))


def _proj_epilogue(tile, acc, gain_ref, zp_ref, zs_ref, cols):
    is_norm = _in_ranges(tile, _NORM_TILES)
    is_gelu = _in_ranges(tile, _GELU_TILES)
    is_silu = _in_ranges(tile, _SILU_TILES)
    is_sigm = _in_ranges(tile, _SIGM_TILES)

    def store(val):
        zp_ref[:, cols] = val[:ROW_TILE]
        zs_ref[:, cols] = val[ROW_TILE:]

    @pl.when(is_norm)
    def _():
        gain = gain_ref[:, cols]
        heads = [slice(hh * HEAD_DIM, (hh + 1) * HEAD_DIM) for hh in range(PROJ_TN // HEAD_DIM)]
        store(jnp.concatenate([_rms(acc[:, sl], gain[:, sl]) for sl in heads], axis=1))

    @pl.when(is_gelu)
    def _():
        store(_gelu(acc))

    @pl.when(is_silu)
    def _():
        store(_silu(acc) * ATT_SCALE)

    @pl.when(is_sigm)
    def _():
        store(_sigmoid(acc))

    @pl.when(jnp.logical_not(is_norm | is_gelu | is_silu | is_sigm))
    def _():
        store(acc)


def _proj_row_offsets():
    tn = PROJ_TILES_PER_STEP * PROJ_TN
    offs = []
    for name in ("qa", "u", "v", "qc", "qd", "fd", "vd", "gd", "kvc", "kvs", "kvw", "kvm"):
        a, w = _ORIG[name]
        offs += [a + k * tn for k in range(w // tn)]
    offs.append(_ORIG["ga"][0])
    assert len(offs) == N_PROJ // tn and all(o % 8 == 0 for o in offs)
    return offs


def _proj(xp, xs, g, mod_p, mod_s, idx, w_in, layer, gains, rows_per_group):
    m, d = xp.shape
    tn = PROJ_TILES_PER_STEP * PROJ_TN
    w_t = jnp.swapaxes(w_in, 1, 2)
    w_spec = pl.BlockSpec((pl.Element(1), pl.Element(tn), pl.Element(d)), lambda i, j, off: (layer, off[j] * 8, 0))
    zp, zs = pl.pallas_call(
        _proj_kernel,
        grid_spec=pltpu.PrefetchScalarGridSpec(
            num_scalar_prefetch=1, grid=(m // ROW_TILE, N_PROJ // tn),
            in_specs=_mod_specs(d, rows_per_group) + [w_spec,
                                                       pl.BlockSpec((None, 1, tn), lambda i, j, off: (j, 0, 0))],
            out_specs=[pl.BlockSpec((ROW_TILE, tn), lambda i, j, off: (i, j)), _sample_out_spec(tn)],
            scratch_shapes=[pltpu.VMEM((COMBINED_TILE, d), BF16)]),
        out_shape=[jax.ShapeDtypeStruct((m, N_PROJ), F32),
                   jax.ShapeDtypeStruct((m // ROW_TILE, SAMPLE_PAD, N_PROJ), F32)],
        compiler_params=_params(("parallel", "arbitrary"), 52),
    )(jnp.asarray(_proj_row_offsets(), jnp.int32) // 8, *_mod_args(xp, xs, g, mod_p, mod_s, idx), w_t, gains)
    return zp, zs[0]


GATES_PER_KV = 3 * GROUP


def _proj_gains(qkg):
    g = jnp.ones((N_PROJ // PROJ_TN, PROJ_TN), F32)
    two = lambda v: jnp.tile(v, PROJ_TN // HEAD_DIM)
    for name, idx, ntile in (("qa", 0, 4), ("qc", 4, 4), ("kvs", 2, 1), ("kvw", 3, 1), ("kvm", 5, 1)):
        t0 = SEG[name] // PROJ_TN
        g = g.at[t0:t0 + ntile].set(two(qkg[idx])[None, :])
    return g.reshape(-1, 1, PROJ_TILES_PER_STEP * PROJ_TN)


CMP_ROWS = 128


def _compress_body(x_of, w1_ref, b1_ref, w2_ref, pe_ref, gk_ref, o_ref, n_cmp):
    i = pl.program_id(1)
    rows = o_ref.shape[2]
    row = lax.broadcasted_iota(jnp.int32, (rows, 1), 0)
    valid = (i * rows + row) < n_cmp
    half = NSA_CMP_STRIDE * HEAD_DIM
    hid_w = NSA_CMP_HIDDEN
    for s in range(2):
        w1ab = w1_ref[s]
        w1b = w1ab[:, hid_w:]
        pe = pe_ref[s]
        bias = (_dot(pe[:, :half], w1ab[:, :hid_w]) + _dot(pe[:, half:], w1b))[0:1] + b1_ref[s]
        w2 = w2_ref[s]
        xs = [x_of(s * KV_HEADS + k) for k in range(KV_HEADS)]
        h = _dot(jnp.concatenate([x.astype(BF16) for x, _ in xs], axis=0), w1ab)
        hn = _dot(jnp.concatenate([xn.astype(BF16) for _, xn in xs], axis=0), w1b)
        for k in range(KV_HEADS):
            h1 = h[k * rows:(k + 1) * rows, :hid_w]
            h2 = h[k * rows:(k + 1) * rows, hid_w:]
            h2s = jnp.where(row == rows - 1, hn[k * 8:k * 8 + 1], pltpu.roll(h2, rows - 1, 0))
            hid = _gelu(h1 + h2s + bias)
            out = _dot(hid.astype(BF16), w2)
            if s == 0:
                out = _rms(out, gk_ref[...])
            o_ref[s, k] = jnp.where(valid, out, 0.0)


def _compress_kernel(r_ref, rn_ref, w1_ref, b1_ref, w2_ref, pe_ref, gk_ref, o_ref, *, n_cmp):
    row_w = 2 * KV_HEADS * HEAD_DIM

    def x_of(c):
        cols = [slice(p * row_w + c * HEAD_DIM, p * row_w + (c + 1) * HEAD_DIM) for p in range(NSA_CMP_STRIDE)]
        return (jnp.concatenate([r_ref[:, sl] for sl in cols], axis=1),
                jnp.concatenate([rn_ref[:, sl] for sl in cols], axis=1))

    _compress_body(x_of, w1_ref, b1_ref, w2_ref, pe_ref, gk_ref, o_ref, n_cmp)


PAGE_HALVES = PAGE_SIZE // NSA_CMP_STRIDE
PAGE_STREAMS = 2 * KV_HEADS


def _compress_paged_kernel(pt_ref, *refs, n_cmp):
    pages = refs[:PAGES_PER_STEP]
    nxt = refs[PAGES_PER_STEP]
    w1_ref, b1_ref, w2_ref, pe_ref, gk_ref, o_ref = refs[PAGES_PER_STEP + 1:]

    def flat(pg, c):
        return jnp.concatenate([pg[pl.ds(p * PAGE_STREAMS + c, PAGE_HALVES, stride=NSA_CMP_STRIDE * PAGE_STREAMS), :]
                                for p in range(NSA_CMP_STRIDE)], axis=1)

    def x_of(c):
        return jnp.concatenate([flat(pg, c) for pg in pages], axis=0), flat(nxt, c)

    _compress_body(x_of, w1_ref, b1_ref, w2_ref, pe_ref, gk_ref, o_ref, n_cmp)


def _compress_weights(w1, b1, w2, pe, gk):
    half = NSA_CMP_STRIDE * HEAD_DIM
    w1r = w1.reshape(2, 2, half, NSA_CMP_HIDDEN)
    w1ab = jnp.concatenate([w1r[:, 0], w1r[:, 1]], axis=-1)
    pe8 = jnp.broadcast_to(pe.reshape(2, 1, 2 * half), (2, 8, 2 * half))
    args = (w1ab.astype(BF16), b1.reshape(2, 1, NSA_CMP_HIDDEN), w2.astype(BF16), pe8.astype(BF16),
            gk.reshape(1, HEAD_DIM))
    return args, [a.shape for a in args]


def _compress(kv_flat, batch, w1, b1, w2, pe, gk):
    n_half = kv_flat.shape[0] // batch // NSA_CMP_STRIDE
    nblk = n_half // CMP_ROWS
    width = NSA_CMP_STRIDE * kv_flat.shape[1]
    r = kv_flat.reshape(batch * n_half, width)
    last8 = batch * n_half // 8 - 1
    wargs, wshapes = _compress_weights(w1, b1, w2, pe, gk)
    full = lambda shape: pl.BlockSpec(shape, lambda b, i: (0,) * len(shape))
    return pl.pallas_call(
        functools.partial(_compress_kernel, n_cmp=n_half - 1),
        grid=(batch, nblk),
        in_specs=[pl.BlockSpec((CMP_ROWS, width), lambda b, i: (b * nblk + i, 0)),
                  pl.BlockSpec((8, width), lambda b, i: (jnp.minimum((b * nblk + i + 1) * (CMP_ROWS // 8), last8), 0))]
                 + [full(s) for s in wshapes],
        out_specs=pl.BlockSpec((None, 2, KV_HEADS, CMP_ROWS, HEAD_DIM), lambda b, i: (b, 0, 0, i, 0)),
        out_shape=jax.ShapeDtypeStruct((batch, 2, KV_HEADS, n_half, HEAD_DIM), F32),
        compiler_params=_params(("parallel", "arbitrary"), 48),
    )(r, r, *wargs)


def _page_view(pool):
    depth, n_phys = pool.shape[:2]
    return pool.reshape(depth, n_phys, PAGE_SIZE * PAGE_STREAMS, HEAD_DIM)


def _compress_paged(pool, layer, page_table, w1, b1, w2, pe, gk):
    view = _page_view(pool)
    batch, n_pages = page_table.shape
    groups = n_pages // PAGES_PER_STEP
    n_half = n_pages * PAGE_HALVES
    wargs, wshapes = _compress_weights(w1, b1, w2, pe, gk)
    page = lambda fn: pl.BlockSpec((None, None, PAGE_SIZE * PAGE_STREAMS, HEAD_DIM), fn)
    in_specs = [page(lambda b, g, pt, r=r: (layer, pt[b, g * PAGES_PER_STEP + r], 0, 0)) for r in range(PAGES_PER_STEP)]
    in_specs.append(page(lambda b, g, pt: (layer, pt[b, jnp.minimum((g + 1) * PAGES_PER_STEP, n_pages - 1)], 0, 0)))
    in_specs += [pl.BlockSpec(s, lambda b, g, pt, n=len(s): (0,) * n) for s in wshapes]
    return pl.pallas_call(
        functools.partial(_compress_paged_kernel, n_cmp=n_half - 1),
        grid_spec=pltpu.PrefetchScalarGridSpec(
            num_scalar_prefetch=1, grid=(batch, groups), in_specs=in_specs,
            out_specs=pl.BlockSpec((None, 2, KV_HEADS, PAGES_PER_STEP * PAGE_HALVES, HEAD_DIM),
                                   lambda b, g, pt: (b, 0, 0, g, 0))),
        out_shape=jax.ShapeDtypeStruct((batch, 2, KV_HEADS, n_half, HEAD_DIM), F32),
        compiler_params=_params(("parallel", "arbitrary"), 56),
    )(page_table, *([view] * (PAGES_PER_STEP + 1)), *wargs)


def _stack_heads(q):
    return jnp.concatenate([q[:, g * HEAD_DIM:(g + 1) * HEAD_DIM] for g in range(GROUP)], axis=0)


def _block_onehot(n_blocks, kpos, shift):
    blk = lax.broadcasted_iota(jnp.int32, (n_blocks, 1), 0)
    return jnp.where((kpos >> shift) == blk, 1.0, 0.0).astype(BF16)


def _nsa_kernel(*refs, tq, qpos0, n_sel, extents, has_tail, win_pos0, win_len):
    if len(extents) > 1:
        refs, osel_ref = refs[:-1], refs[-1]
    if has_tail:
        q_ref, ga_ref, kc_ref, vc_ref, ks_ref, vs_ref, kt_ref, vt_ref, kw_ref, vw_ref, o_ref = refs
    else:
        q_ref, ga_ref, kc_ref, vc_ref, ks_ref, vs_ref, kw_ref, vw_ref, o_ref = refs
    q0 = qpos0 + pl.program_id(2) * tq
    rows = GROUP * tq
    q4 = _stack_heads(q_ref[...]).astype(BF16)
    qpos = q0 + (lax.broadcasted_iota(jnp.int32, (rows, 1), 0) & (tq - 1))

    kc = kc_ref[...].astype(BF16)
    vc = vc_ref[...].astype(BF16)
    n_cp = kc.shape[0]
    cidx = lax.broadcasted_iota(jnp.int32, (1, n_cp), 1)
    s_cmp = _dot_nt(q4, kc) * ATT_SCALE
    (e_cmp,), den = _softmax_parts([s_cmp], [(cidx * NSA_CMP_STRIDE + (NSA_CMP_BLOCK - 1)) <= qpos])
    p_cmp = e_cmp / den
    o_cmp = _dot(p_cmp.astype(BF16), vc)

    n_sp = -(-n_sel // LANE) * LANE
    psum = functools.reduce(lambda a, b: a + b, [p_cmp[g * tq:(g + 1) * tq] for g in range(GROUP)])
    ci = lax.broadcasted_iota(jnp.int32, (n_cp, 1), 0) * NSA_CMP_STRIDE
    sj = lax.broadcasted_iota(jnp.int32, (1, n_sp), 1) * NSA_SEL_BLOCK
    cover = jnp.where((ci <= sj + (NSA_SEL_BLOCK - 1)) & (ci + (NSA_CMP_BLOCK - 1) >= sj), 1.0, 0.0).astype(BF16)
    imp = _dot_f32_lhs(psum, cover)
    col = lax.broadcasted_iota(jnp.int32, (1, n_sp), 1)
    qpos_t = q0 + lax.broadcasted_iota(jnp.int32, (tq, 1), 0)
    back = (qpos_t >> 6) - col
    valid = back >= 0
    forced = (col == 0) | (valid & (back < NSA_N_LOCAL))
    score = jnp.where(valid, jnp.where(forced, FORCE_SCORE, imp), NEG_INF)
    rank = _rank(score, n_sel, col)
    sel = jnp.where((rank < NSA_SEL_TOP) & (score > 0.5 * NEG_INF), 1.0, 0.0).astype(BF16)
    sel4 = jnp.concatenate([sel] * GROUP, axis=0)

    def sel_attend(n_keys):
        s_list, m_list, v_list = [], [], []

        def add_tile(k_bf, v_bf, kpos):
            picked = _dot(sel4, _block_onehot(n_sp, kpos, 6))
            s_list.append(_dot_nt(q4, k_bf) * ATT_SCALE)
            m_list.append(jnp.where(kpos <= qpos, picked, 0.0) > 0.5)
            v_list.append(v_bf)

        kt = min(n_keys, KEY_TILE)
        for t0 in range(0, n_keys, kt):
            add_tile(ks_ref[t0:t0 + kt, :].astype(BF16), vs_ref[t0:t0 + kt, :].astype(BF16),
                     t0 + lax.broadcasted_iota(jnp.int32, (1, kt), 1))
        if has_tail:
            add_tile(kt_ref[...].astype(BF16), vt_ref[...].astype(BF16),
                     n_keys + lax.broadcasted_iota(jnp.int32, (1, TAIL_ROWS), 1))
        e_list, den = _softmax_parts(s_list, m_list, never_empty=True)
        return functools.reduce(lambda a, b: a + b, [_dot(e.astype(BF16), v) for e, v in zip(e_list, v_list)]) / den

    if len(extents) == 1:
        o_sel = sel_attend(extents[0])
    else:
        need = (q0 - qpos0 + tq - 1) // extents[0]
        for idx, ext in enumerate(extents):
            @pl.when(need == idx)
            def _(ext=ext):
                osel_ref[...] = sel_attend(ext)
        o_sel = osel_ref[...]

    start = jnp.clip(q0 - NSA_WINDOW - win_pos0, 0, win_len - WIN_BAND)
    start = pl.multiple_of(start, LANE)
    kw = kw_ref[pl.ds(start, WIN_BAND), :].astype(BF16)
    vw = vw_ref[pl.ds(start, WIN_BAND), :].astype(BF16)
    kposw = win_pos0 + start + lax.broadcasted_iota(jnp.int32, (1, WIN_BAND), 1)
    s_win = _dot_nt(q4, kw) * ATT_SCALE
    m_win = jnp.where(kposw <= qpos, qpos - kposw, NSA_WINDOW) < NSA_WINDOW
    (e_win,), den = _softmax_parts([s_win], [m_win], never_empty=True)
    o_win = _dot(e_win.astype(BF16), vw) / den

    ga = ga_ref[...]
    ga = jnp.where(pl.program_id(1) == 0, ga, pltpu.roll(ga, LANE - GATES_PER_KV, 1))
    outs = []
    for g in range(GROUP):
        r = slice(g * tq, (g + 1) * tq)
        outs.append(ga[:, 3 * g:3 * g + 1] * o_cmp[r] + ga[:, 3 * g + 1:3 * g + 2] * o_sel[r]
                    + ga[:, 3 * g + 2:3 * g + 3] * o_win[r])
    o_ref[...] = jnp.concatenate(outs, axis=1).astype(o_ref.dtype)


CAUSAL_STEP = 512


def _causal_extents(tk, tq):
    if tk % CAUSAL_STEP or CAUSAL_STEP % tq:
        return (tk,)
    return tuple(range(CAUSAL_STEP, tk + 1, CAUSAL_STEP))


def _nsa(zq, tq_total, tq, qpos0, cmp_kv, sel_main, sel_cols, sel_tail, win, win_cols, win_pos0, n_total,
         causal_skip):
    batch = cmp_kv.shape[0]
    nq = tq_total // tq
    n_cp = cmp_kv.shape[3]
    tk = sel_main.shape[0] // batch
    lw = win.shape[0] // batch
    n_sel = -(-n_total // NSA_SEL_BLOCK)
    extents = _causal_extents(tk, tq) if causal_skip else (tk,)
    scratch = [pltpu.VMEM((GROUP * tq, HEAD_DIM), F32)] if len(extents) > 1 else []
    qa_blk = SEG["qa"] // (GROUP * HEAD_DIM)
    ga_blk = SEG["ga"] // LANE
    in_specs = [
        pl.BlockSpec((tq, GROUP * HEAD_DIM), lambda b, k, i: (b * nq + i, qa_blk + k)),
        pl.BlockSpec((tq, LANE), lambda b, k, i: (b * nq + i, ga_blk)),
        pl.BlockSpec((None, None, None, n_cp, HEAD_DIM), lambda b, k, i: (b, 0, k, 0, 0)),
        pl.BlockSpec((None, None, None, n_cp, HEAD_DIM), lambda b, k, i: (b, 1, k, 0, 0)),
        pl.BlockSpec((tk, HEAD_DIM), lambda b, k, i: (b, sel_cols[0] + k)),
        pl.BlockSpec((tk, HEAD_DIM), lambda b, k, i: (b, sel_cols[1] + k)),
    ]
    args = [zq, zq, cmp_kv, cmp_kv, sel_main, sel_main]
    if sel_tail is not None:
        in_specs += [pl.BlockSpec((TAIL_ROWS, HEAD_DIM), lambda b, k, i: (b, k)),
                     pl.BlockSpec((TAIL_ROWS, HEAD_DIM), lambda b, k, i: (b, KV_HEADS + k))]
        args += [sel_tail, sel_tail]
    in_specs += [pl.BlockSpec((lw, HEAD_DIM), lambda b, k, i: (b, win_cols[0] + k)),
                 pl.BlockSpec((lw, HEAD_DIM), lambda b, k, i: (b, win_cols[1] + k))]
    args += [win, win]
    return pl.pallas_call(
        functools.partial(_nsa_kernel, tq=tq, qpos0=qpos0, n_sel=n_sel, extents=extents,
                          has_tail=sel_tail is not None, win_pos0=win_pos0, win_len=lw),
        grid=(batch, KV_HEADS, nq),
        in_specs=in_specs,
        out_specs=pl.BlockSpec((tq, GROUP * HEAD_DIM), lambda b, k, i: (b * nq + i, k)),
        out_shape=jax.ShapeDtypeStruct((batch * tq_total, WIDTH), BF16),
        scratch_shapes=scratch,
        compiler_params=_params(("parallel", "parallel", "arbitrary"), 56),
    )(*args)


def _moba_kernel(*refs, tq, qpos0, extents, has_tail):
    if has_tail:
        q_ref, k_ref, v_ref, kt_ref, vt_ref, o_ref = refs
    else:
        q_ref, k_ref, v_ref, o_ref = refs
    q0 = qpos0 + pl.program_id(2) * tq
    rows = GROUP * tq
    q4f = _stack_heads(q_ref[...])
    q4 = q4f.astype(BF16)
    qpos = q0 + (lax.broadcasted_iota(jnp.int32, (rows, 1), 0) & (tq - 1))
    cur = qpos >> 8
    col = lax.broadcasted_iota(jnp.int32, (1, LANE), 1)

    def attend(n_keys):
        kt = min(n_keys, KEY_TILE)
        n_blk = n_keys // MOBA_BLOCK
        means = [jnp.sum(k_ref[t0:t0 + kt, :].reshape(kt // MOBA_BLOCK, MOBA_BLOCK, HEAD_DIM), axis=1)
                 * (1.0 / MOBA_BLOCK) for t0 in range(0, n_keys, kt)]
        if n_blk < LANE:
            means.append(jnp.zeros((LANE - n_blk, HEAD_DIM), F32))
        kmean = jnp.concatenate(means, axis=0)
        gate = jnp.where(col < cur, _dot_nt_f32(q4f, kmean), NEG_INF)
        rank = _rank(gate, n_blk, col)
        sel = jnp.where((rank < MOBA_TOP) & (gate > 0.5 * NEG_INF), 1.0, 0.0).astype(BF16)

        s_list, m_list, v_list = [], [], []

        def add_tile(k_bf, v_bf, kpos):
            picked = _dot(sel, _block_onehot(LANE, kpos, 8))
            own = jnp.where((kpos >> 8) == cur, jnp.where(kpos <= qpos, 1.0, 0.0), 0.0)
            s_list.append(_dot_nt(q4, k_bf) * ATT_SCALE)
            m_list.append((picked + own) > 0.5)
            v_list.append(v_bf)

        for t0 in range(0, n_keys, kt):
            add_tile(k_ref[t0:t0 + kt, :].astype(BF16), v_ref[t0:t0 + kt, :].astype(BF16),
                     t0 + lax.broadcasted_iota(jnp.int32, (1, kt), 1))
        if has_tail:
            add_tile(kt_ref[...].astype(BF16), vt_ref[...].astype(BF16),
                     n_keys + lax.broadcasted_iota(jnp.int32, (1, TAIL_ROWS), 1))
        e_list, den = _softmax_parts(s_list, m_list, never_empty=True)
        o = functools.reduce(lambda a, b: a + b, [_dot(e.astype(BF16), v) for e, v in zip(e_list, v_list)]) / den
        o_ref[...] = jnp.concatenate([o[g * tq:(g + 1) * tq] for g in range(GROUP)], axis=1).astype(o_ref.dtype)

    if len(extents) == 1:
        attend(extents[0])
    else:
        need = (q0 - qpos0 + tq - 1) // extents[0]
        for idx, ext in enumerate(extents):
            @pl.when(need == idx)
            def _(ext=ext):
                attend(ext)


def _moba(zq, tq_total, tq, qpos0, batch, main, cols, tail, causal_skip):
    nq = tq_total // tq
    tk = main.shape[0] // batch
    extents = _causal_extents(tk, tq) if causal_skip else (tk,)
    qc_blk = SEG["qc"] // (GROUP * HEAD_DIM)
    in_specs = [pl.BlockSpec((tq, GROUP * HEAD_DIM), lambda b, k, i: (b * nq + i, qc_blk + k)),
                pl.BlockSpec((tk, HEAD_DIM), lambda b, k, i: (b, cols[0] + k)),
                pl.BlockSpec((tk, HEAD_DIM), lambda b, k, i: (b, cols[1] + k))]
    args = [zq, main, main]
    if tail is not None:
        in_specs += [pl.BlockSpec((TAIL_ROWS, HEAD_DIM), lambda b, k, i: (b, k)),
                     pl.BlockSpec((TAIL_ROWS, HEAD_DIM), lambda b, k, i: (b, KV_HEADS + k))]
        args += [tail, tail]
    return pl.pallas_call(
        functools.partial(_moba_kernel, tq=tq, qpos0=qpos0, extents=extents, has_tail=tail is not None),
        grid=(batch, KV_HEADS, nq),
        in_specs=in_specs,
        out_specs=pl.BlockSpec((tq, GROUP * HEAD_DIM), lambda b, k, i: (b * nq + i, k)),
        out_shape=jax.ShapeDtypeStruct((batch * tq_total, WIDTH), BF16),
        compiler_params=_params(("parallel", "parallel", "arbitrary"), 56),
    )(*args)


def _layer_norm(v, g, b, eps=1e-5):
    mu = jnp.mean(v, axis=-1, keepdims=True)
    var = jnp.mean(jnp.square(v - mu), axis=-1, keepdims=True)
    return (v - mu) * lax.rsqrt(var + eps) * g + b


def _gmlp_kernel(u_ref, v_ref, ws_ref, bst_ref, g_ref, b_ref, o_ref):
    vn = _layer_norm(v_ref[...], g_ref[...], b_ref[...]).astype(BF16)
    r = lax.broadcasted_iota(jnp.int32, (GMLP_CHUNK, GMLP_CHUNK), 0)
    c = lax.broadcasted_iota(jnp.int32, (GMLP_CHUNK, GMLP_CHUNK), 1)
    bst = bst_ref[...]
    for g in range(N_HEADS):
        sl = slice(g * HEAD_DIM, (g + 1) * HEAD_DIM)
        wm = jnp.where(c <= r, ws_ref[g], 0.0).astype(BF16)
        sv = _dot(wm, vn[:, sl]) + bst[:, g:g + 1]
        o_ref[:, sl] = (u_ref[:, sl] * sv).astype(o_ref.dtype)


def _gmlp(z, ws, bs, ln_g, ln_b):
    m = z.shape[0]
    ub, vb = SEG["u"] // WIDTH, SEG["v"] // WIDTH
    full = lambda shape: pl.BlockSpec(shape, lambda i: (0,) * len(shape))
    return pl.pallas_call(
        _gmlp_kernel,
        grid=(m // GMLP_CHUNK,),
        in_specs=[pl.BlockSpec((GMLP_CHUNK, WIDTH), lambda i: (i, ub)),
                  pl.BlockSpec((GMLP_CHUNK, WIDTH), lambda i: (i, vb)),
                  full(ws.shape), full((GMLP_CHUNK, N_HEADS)), full((1, WIDTH)), full((1, WIDTH))],
        out_specs=pl.BlockSpec((GMLP_CHUNK, WIDTH), lambda i: (i, 0)),
        out_shape=jax.ShapeDtypeStruct((m, WIDTH), BF16),
        compiler_params=_params(("parallel",), 32),
    )(z, z, ws, bs.T, ln_g.reshape(1, WIDTH), ln_b.reshape(1, WIDTH))


def _hgrn_gates(fd, lb):
    f_gate = lb + (1.0 - lb) * _sigmoid(fd)
    return jnp.log(jnp.maximum(f_gate, TINY)), (1.0 - lb) * _sigmoid(-fd)


def _hgrn_readout(o, gd, g_out):
    return _rms(o, g_out) * _silu(gd)


HGRN_HEADS_PER_STEP = 4


def _hgrn_kernel(q_ref, f_ref, v_ref, gd_ref, lb_ref, go_ref, o_ref, s_ref):
    c = HGRN_CHUNK
    t_total = q_ref.shape[0]
    g_out = go_ref[...]
    ri = lax.broadcasted_iota(jnp.int32, (c, c), 0)
    ci = lax.broadcasted_iota(jnp.int32, (c, c), 1)
    tri = jnp.where(ci <= ri, 1.0, 0.0).astype(BF16)
    lane = lax.broadcasted_iota(jnp.int32, (1, c), 1)
    diag_mask = (ci <= ri) & ((ci >> 3) == (ri >> 3))

    def head_chunk(r0, hh, st):
        hs = slice(hh * HEAD_DIM, (hh + 1) * HEAD_DIM)
        q = q_ref[pl.ds(r0, c), hs]
        v = v_ref[pl.ds(r0, c), hs]
        lf, k = _hgrn_gates(f_ref[pl.ds(r0, c), hs], lb_ref[:, hs])
        cum = _dot_f32_rhs(tri, lf)
        a = jnp.zeros((c, c), F32)
        for bs in (32, 16, 8):
            nb = c // bs
            refq = jnp.concatenate(
                [jnp.zeros((bs, HEAD_DIM), F32)]
                + [jnp.broadcast_to(cum[b * bs - 1:b * bs], (bs, HEAD_DIM)) for b in range(1, nb)], axis=0)
            refk = jnp.concatenate(
                [jnp.broadcast_to(cum[(b + 1) * bs - 1:(b + 1) * bs], (bs, HEAD_DIM)) for b in range(nb)], axis=0)
            qt = (q * jnp.exp(jnp.minimum(cum - refq, 0.0))).astype(BF16)
            kt = (k * jnp.exp(jnp.minimum(refk - cum, 0.0))).astype(BF16)
            sh = bs.bit_length() - 1
            lvl = (((ri >> sh) & 1) == 1) & ((ci >> sh) == (ri >> sh) - 1)
            a = a + jnp.where(lvl, _dot_nt(qt, kt), 0.0)
        rows = []
        for blk in range(c // 8):
            b0 = blk * 8
            qb, kb, cb = q[b0:b0 + 8], k[b0:b0 + 8], cum[b0:b0 + 8]
            acc = jnp.zeros((8, c), F32)
            for s in range(8):
                w = qb * kb[s:s + 1] * jnp.exp(jnp.minimum(cb - cb[s:s + 1], 0.0))
                acc = jnp.where(lane == b0 + s, jnp.sum(w, axis=1, keepdims=True), acc)
            rows.append(acc)
        a = a + jnp.where(diag_mask, jnp.concatenate(rows, axis=0), 0.0)
        o = _dot(a.astype(BF16), v.astype(BF16)) + _dot_nt((q * jnp.exp(cum)).astype(BF16), st.astype(BF16))
        o_ref[pl.ds(r0, c), hs] = _hgrn_readout(o, gd_ref[pl.ds(r0, c), hs], g_out).astype(o_ref.dtype)
        last = cum[c - 1:c]
        kk = (k * jnp.exp(last - cum)).astype(BF16)
        return st * jnp.exp(last) + _dot_tn(v.astype(BF16), kk)

    def chunk(n, sts):
        r0 = pl.multiple_of(n * c, c)
        return tuple(head_chunk(r0, hh, st) for hh, st in enumerate(sts))

    zero = jnp.zeros((HEAD_DIM, HEAD_DIM), F32)
    sts = lax.fori_loop(0, t_total // c, chunk, (zero,) * HGRN_HEADS_PER_STEP, unroll=2)
    for hh, st in enumerate(sts):
        s_ref[hh] = st.T


def _hgrn(z, batch, lb, g_out):
    t = z.shape[0] // batch
    hw = HGRN_HEADS_PER_STEP * HEAD_DIM
    blk = lambda name: SEG[name] // hw
    col = lambda name: pl.BlockSpec((t, hw), lambda b, h: (b, blk(name) + h))
    return pl.pallas_call(
        _hgrn_kernel,
        grid=(batch, N_HEADS // HGRN_HEADS_PER_STEP),
        in_specs=[col("qd"), col("fd"), col("vd"), col("gd"),
                  pl.BlockSpec((1, hw), lambda b, h: (0, h)),
                  pl.BlockSpec((1, HEAD_DIM), lambda b, h: (0, 0))],
        out_specs=[pl.BlockSpec((t, hw), lambda b, h: (b, h)),
                   pl.BlockSpec((None, HGRN_HEADS_PER_STEP, HEAD_DIM, HEAD_DIM), lambda b, h: (b, h, 0, 0))],
        out_shape=[jax.ShapeDtypeStruct((batch * t, WIDTH), BF16),
                   jax.ShapeDtypeStruct((batch, N_HEADS, HEAD_DIM, HEAD_DIM), F32)],
        compiler_params=_params(("parallel", "parallel"), 48),
    )(z, z, z, z, lb.reshape(1, WIDTH), g_out.reshape(1, HEAD_DIM))


def _gmlp_step_kernel(u_ref, v_ref, w_ref, b_ref, g_ref, bb_ref, o_ref, vn_ref):
    vn = _layer_norm(v_ref[...], g_ref[...], bb_ref[...])
    vn_ref[...] = vn
    o_ref[...] = (u_ref[...] * (w_ref[...] * vn + b_ref[...])).astype(o_ref.dtype)


def _gmlp_step(u, v, ws, bs, ln_g, ln_b):
    rows = u.shape[0]
    w_row = jnp.repeat(ws[:, 0, 0], HEAD_DIM).reshape(1, WIDTH)
    b_row = jnp.repeat(bs[:, 0], HEAD_DIM).reshape(1, WIDTH)
    return pl.pallas_call(
        _gmlp_step_kernel,
        out_shape=[jax.ShapeDtypeStruct((rows, WIDTH), BF16), jax.ShapeDtypeStruct((rows, WIDTH), F32)],
    )(u, v, w_row, b_row, ln_g.reshape(1, WIDTH), ln_b.reshape(1, WIDTH))


def _hgrn_step_kernel(qc_ref, fc_ref, lbc_ref, v_ref, gd_ref, go_ref, s0_ref, o_ref, s_ref):
    lf, k = _hgrn_gates(fc_ref[...], lbc_ref[...])
    s_new = jnp.exp(lf) * s0_ref[...] + k * v_ref[...]
    s_ref[...] = s_new
    o = jnp.sum(qc_ref[...] * s_new, axis=0, keepdims=True)
    o_ref[...] = _hgrn_readout(o, gd_ref[...], go_ref[...]).astype(o_ref.dtype)


def _hgrn_step(qd, fd, vd, gd, lb, g_out, state, layer):
    batch = qd.shape[0]
    colv = lambda a: a.reshape(batch, N_HEADS, HEAD_DIM, 1)
    rowv = lambda a: a.reshape(batch, N_HEADS, 1, HEAD_DIM)
    cspec = pl.BlockSpec((None, None, HEAD_DIM, 1), lambda b, h: (b, h, 0, 0))
    rspec = pl.BlockSpec((None, None, 1, HEAD_DIM), lambda b, h: (b, h, 0, 0))
    return pl.pallas_call(
        _hgrn_step_kernel,
        grid=(batch, N_HEADS),
        in_specs=[cspec, cspec, pl.BlockSpec((None, HEAD_DIM, 1), lambda b, h: (h, 0, 0)), rspec, rspec,
                  pl.BlockSpec((1, HEAD_DIM), lambda b, h: (0, 0)),
                  pl.BlockSpec((None, None, None, HEAD_DIM, HEAD_DIM), lambda b, h: (layer, b, h, 0, 0))],
        out_specs=[rspec, pl.BlockSpec((None, None, HEAD_DIM, HEAD_DIM), lambda b, h: (b, h, 0, 0))],
        out_shape=[jax.ShapeDtypeStruct((batch, N_HEADS, 1, HEAD_DIM), BF16),
                   jax.ShapeDtypeStruct((batch, N_HEADS, HEAD_DIM, HEAD_DIM), F32)],
        compiler_params=_params(("parallel", "parallel"), 32),
    )(colv(qd), colv(fd), lb.reshape(N_HEADS, HEAD_DIM, 1), rowv(vd), rowv(gd), g_out.reshape(1, HEAD_DIM), state)


GATHER_PAGES = 32


def _gather_kernel(pt_ref, *refs):
    o_ref = refs[-1]
    for r in range(GATHER_PAGES):
        for c in range(PAGE_STREAMS):
            o_ref[r * PAGE_SIZE:(r + 1) * PAGE_SIZE, c * HEAD_DIM:(c + 1) * HEAD_DIM] = (
                refs[r][pl.ds(c, PAGE_SIZE, stride=PAGE_STREAMS), :])


def _gather_pages(pool, layer, page_table):
    view = _page_view(pool)
    width = PAGE_STREAMS * HEAD_DIM
    batch, n_pages = page_table.shape
    groups = n_pages // GATHER_PAGES
    rows = GATHER_PAGES * PAGE_SIZE
    in_specs = [pl.BlockSpec((None, None, PAGE_SIZE * PAGE_STREAMS, HEAD_DIM),
                             lambda b, g, pt, r=r: (layer, pt[b, g * GATHER_PAGES + r], 0, 0))
                for r in range(GATHER_PAGES)]
    return pl.pallas_call(
        _gather_kernel,
        grid_spec=pltpu.PrefetchScalarGridSpec(
            num_scalar_prefetch=1, grid=(batch, groups), in_specs=in_specs,
            out_specs=pl.BlockSpec((rows, width), lambda b, g, pt: (b * groups + g, 0))),
        out_shape=jax.ShapeDtypeStruct((batch * n_pages * PAGE_SIZE, width), pool.dtype),
        compiler_params=_params(("parallel", "arbitrary"), 40),
    )(page_table, *([view] * GATHER_PAGES))


def _ffn(xp, xs, mod_p, mod_s, sub, mod_idx, g, wg, wu, wd, layer, rows_per_group):
    act = _norm_glu(xp, xs, g, mod_p, mod_s, mod_idx, wg, wu, layer, sub, rows_per_group)
    return _down(act, wd, layer, sub, xp, xs, mod_p[:, mod_idx, 2], mod_s[:, mod_idx, 2], 0.5)


def _kv5(a, batch):
    return a.reshape(batch, -1, 2, KV_HEADS, HEAD_DIM)


def _cache_rows_kernel(*refs):
    z_refs, o_ref = refs[:-1], refs[-1]
    rows = z_refs[0].shape[0]
    for li, z_ref in enumerate(z_refs):
        @pl.when(pl.program_id(0) == li)
        def _(z_ref=z_ref):
            for c in range(PAGE_STREAMS):
                o_ref[pl.ds(c, rows, stride=PAGE_STREAMS), :] = z_ref[:, c * HEAD_DIM:(c + 1) * HEAD_DIM]


def _cache_rows(z_layers, name, batch, keep):
    depth = len(z_layers)
    t = z_layers[0].shape[0] // batch
    rows = min(keep, ROW_TILE)
    per_batch, first = keep // rows, (t - keep) // rows
    n_steps = batch * per_batch
    width = PAGE_STREAMS * HEAD_DIM
    blk = SEG[name] // width

    def z_spec(li):
        def index(l, s):
            s_eff = jnp.where(l == li, s, jnp.where(l < li, 0, n_steps - 1))
            return ((s_eff // per_batch) * (t // rows) + first + s_eff % per_batch, blk)
        return pl.BlockSpec((rows, width), index)

    out = pl.pallas_call(
        _cache_rows_kernel,
        grid=(depth, n_steps),
        in_specs=[z_spec(li) for li in range(depth)],
        out_specs=pl.BlockSpec((None, rows * PAGE_STREAMS, HEAD_DIM), lambda l, s: (l, s, 0)),
        out_shape=jax.ShapeDtypeStruct((depth, batch * keep * PAGE_STREAMS, HEAD_DIM), F32),
        compiler_params=_params(("arbitrary", "arbitrary"), 32),
    )(*z_layers)
    return out.reshape(depth, batch, keep, 2, KV_HEADS, HEAD_DIM)


def _prompt_mix(z, batch, mixw):
    t = z.shape[0] // batch
    kvc = z[:, SEG["kvc"]:SEG["kvc"] + PAGE_STREAMS * HEAD_DIM]
    cmp_kv = _compress(kvc, batch, mixw["cw1"], mixw["cb1"], mixw["cw2"], mixw["cpe"], mixw["qkg"][1])
    kb = lambda name: SEG[name] // HEAD_DIM
    o_a = _nsa(z, t, 128, 0, cmp_kv, z, (kb("kvs"), kb("kvs") + KV_HEADS), None,
               z, (kb("kvw"), kb("kvw") + KV_HEADS), 0, t, True)
    o_b = _gmlp(z, mixw["ws"], mixw["bs"], mixw["ln_g"], mixw["ln_b"])
    o_c = _moba(z, t, 128, 0, batch, z, (kb("kvm"), kb("kvm") + KV_HEADS), None, True)
    o_d, s_d = _hgrn(z, batch, mixw["lb"], mixw["out_g"])
    return (o_a, o_b, o_c, o_d), s_d


def _pad_rows(a, batch, rows):
    out = jnp.zeros((batch, rows, a.shape[1]), a.dtype).at[:, 0].set(a)
    return out.reshape(batch * rows, a.shape[1])


def _sample_mix(z_pad, batch, mixw, pool_cmp, pool_sel, pool_moba, win_buf, state, page_table, layer):
    past = page_table.shape[1] * PAGE_SIZE
    z = z_pad[:batch]
    seg = lambda name, w: z[:, SEG[name]:SEG[name] + w]
    kvc, kvs, kvw, kvm = seg("kvc", 512), seg("kvs", 512), seg("kvw", 512), seg("kvm", 512)
    zq = _pad_rows(z, batch, SAMPLE_ROWS)
    dense_sel = _gather_pages(pool_sel, layer, page_table)
    dense_moba = _gather_pages(pool_moba, layer, page_table)
    cmp_kv = _compress_paged(pool_cmp, layer, page_table, mixw["cw1"], mixw["cb1"], mixw["cw2"], mixw["cpe"],
                             mixw["qkg"][1])
    wlen = win_buf.shape[2]
    band = jnp.concatenate([win_buf[layer].reshape(batch, wlen, 512), kvw[:, None, :]], axis=1)
    win = jnp.concatenate([band, jnp.zeros((batch, WIN_BAND - wlen - 1, 512), F32)], axis=1)
    o_a = _nsa(zq, SAMPLE_ROWS, SAMPLE_ROWS, past, cmp_kv, dense_sel, (0, KV_HEADS),
               _pad_rows(kvs, batch, TAIL_ROWS), win.reshape(batch * WIN_BAND, 512), (0, KV_HEADS),
               past - wlen, past + 1, False)
    o_c = _moba(zq, SAMPLE_ROWS, SAMPLE_ROWS, past, batch, dense_moba, (0, KV_HEADS),
                _pad_rows(kvm, batch, TAIL_ROWS), False)
    o_b, v_n = _gmlp_step(seg("u", WIDTH), seg("v", WIDTH), mixw["ws"], mixw["bs"], mixw["ln_g"], mixw["ln_b"])
    o_d, s_d = _hgrn_step(seg("qd", WIDTH), seg("fd", WIDTH), seg("vd", WIDTH), seg("gd", WIDTH),
                          mixw["lb"], mixw["out_g"], state, layer)
    mixed = jnp.concatenate([o_a[::SAMPLE_ROWS], o_b, o_c[::SAMPLE_ROWS], o_d.reshape(batch, WIDTH)], axis=1)
    mixed = jnp.concatenate([mixed, jnp.zeros((z_pad.shape[0] - batch, mixed.shape[1]), mixed.dtype)], axis=0)
    new_state = (_kv5(kvc, batch), _kv5(kvs, batch), _kv5(kvm, batch),
                 band[:, 1:].reshape(batch, wlen, 2, KV_HEADS, HEAD_DIM), s_d, v_n.reshape(batch, 1, WIDTH))
    return mixed, new_state


def kernel(x_prompt, x_sample, c_prompt, c_sample, cache_nsa_cmp_kv, cache_nsa_sel_kv, cache_moba_kv, cache_nsa_win_kv, state_hgrn, page_table, w_ada, b_ada, norm_g, w_ffn_gate, w_ffn_up, w_ffn_down, w_in, w_out, qk_norm_g, nsa_cmp_w1, nsa_cmp_b1, nsa_cmp_w2, nsa_cmp_pos, gmlp_ln_g, gmlp_ln_b, gmlp_ws, gmlp_bs, hgrn_lb_logits, hgrn_out_g):
    bp, t, d = x_prompt.shape
    bs = x_sample.shape[0]
    depth = w_in.shape[0]
    srows = SAMPLE_PAD
    lb_p = jax.nn.softmax(hgrn_lb_logits.astype(F32), axis=0)
    lb_all = jnp.clip(jnp.cumsum(lb_p, axis=0) - lb_p[0:1], 0.0, 1.0)

    c_all = jnp.concatenate([c_prompt, c_sample, jnp.zeros((16 - bp - bs, d), F32)], axis=0)
    mod = _ada_mod(c_all, w_ada, b_ada).reshape(depth, 16, 3, 3, d)

    xp = x_prompt.reshape(bp * t, d)
    xs = jnp.concatenate([x_sample.reshape(bs, d), jnp.zeros((srows - bs, d), F32)], axis=0)
    z_layers, hgrn_p, st_s = [], [], []
    for l in range(depth):
        mod_p = mod[l, :bp]
        mod_s = jnp.concatenate([mod[l, bp:bp + bs], jnp.zeros((srows - bs, 3, 3, d), F32)], axis=0)
        gains = _proj_gains(qk_norm_g[l])
        mixw = dict(qkg=qk_norm_g[l], cw1=nsa_cmp_w1[l], cb1=nsa_cmp_b1[l], cw2=nsa_cmp_w2[l], cpe=nsa_cmp_pos[l],
                    ln_g=gmlp_ln_g[l], ln_b=gmlp_ln_b[l], ws=gmlp_ws[l], bs=gmlp_bs[l], lb=lb_all[l],
                    out_g=hgrn_out_g[l])
        g = norm_g[l]

        xp, xs = _ffn(xp, xs, mod_p, mod_s, 0, 0, g[0], w_ffn_gate, w_ffn_up, w_ffn_down, l, t)
        zp, zs = _proj(xp, xs, g[1], mod_p, mod_s, 1, w_in, l, gains, t)
        mixed_p, sp = _prompt_mix(zp, bp, mixw)
        mixed_s, ss = _sample_mix(zs, bs, mixw, cache_nsa_cmp_kv, cache_nsa_sel_kv, cache_moba_kv,
                                  cache_nsa_win_kv, state_hgrn, page_table, l)
        xp, xs = _out_proj(mixed_p, mixed_s, w_out, l, xp, xs, mod_p[:, 1, 2], mod_s[:, 1, 2], t)
        xp, xs = _ffn(xp, xs, mod_p, mod_s, 1, 2, g[2], w_ffn_gate, w_ffn_up, w_ffn_down, l, t)
        z_layers.append(zp)
        hgrn_p.append(sp)
        st_s.append(ss)

    stack = lambda sts, i: jnp.stack([s[i] for s in sts])
    return (xp.reshape(bp, t, d), xs[:bs].reshape(bs, 1, d),
            _cache_rows(z_layers, "kvc", bp, t), _cache_rows(z_layers, "kvs", bp, t),
            _cache_rows(z_layers, "kvm", bp, t), _cache_rows(z_layers, "kvw", bp, min(NSA_WINDOW, t)),
            jnp.stack(hgrn_p),
            stack(st_s, 0), stack(st_s, 1), stack(st_s, 2), stack(st_s, 3), stack(st_s, 4), stack(st_s, 5))
```

```python
import functools

import jax
import jax.numpy as jnp
from jax import lax
from jax.experimental import pallas as pl
from jax.experimental.pallas import tpu as pltpu

F32 = jnp.float32
BF16 = jnp.bfloat16

HEAD_DIM = 128
N_HEADS = 8
KV_HEADS = 2
GROUP = N_HEADS // KV_HEADS
WIDTH = N_HEADS * HEAD_DIM
PAGE_SIZE = 128
PAGES_PER_STEP = 32

NSA_CMP_STRIDE = 16
NSA_CMP_BLOCK = 32
NSA_CMP_HIDDEN = 256
NSA_SEL_BLOCK = 64
NSA_SEL_TOP = 16
NSA_N_LOCAL = 2
NSA_WINDOW = 512
GMLP_CHUNK = 128
MOBA_BLOCK = 256
MOBA_TOP = 3
HGRN_CHUNK = 64
NEG_INF = -1e30
FORCE_SCORE = 1e4
TINY = 1e-30
ATT_SCALE = HEAD_DIM ** -0.5

LANE = 128
KEY_TILE = 4096
WIN_BAND = NSA_WINDOW + 128
SAMPLE_ROWS = 8
TAIL_ROWS = 256

PROJ_TN = 256
PROJ_TILES_PER_STEP = 2
SEG = {}
_off = 0
for _name, _w in (("qa", 1024), ("u", 1024), ("v", 1024), ("qc", 1024), ("qd", 1024), ("fd", 1024),
                  ("vd", 1024), ("gd", 1024), ("kvc", 512), ("kvs", 512), ("kvw", 512), ("kvm", 512),
                  ("ga", 256), ("pad", 256)):
    SEG[_name] = _off
    _off += _w
N_PROJ = _off

_ORIG = {}
_o = 0
for _name, _w in (("qa", 1024), ("kvc", 512), ("kvs", 512), ("kvw", 512), ("ga", 24), ("u", 1024), ("v", 1024),
                  ("qc", 1024), ("kvm", 512), ("qd", 1024), ("fd", 1024), ("vd", 1024), ("gd", 1024)):
    _ORIG[_name] = (_o, _w)
    _o += _w


def _params(sem, vmem_mb):
    return pltpu.CompilerParams(dimension_semantics=sem, vmem_limit_bytes=vmem_mb << 20)


def _sigmoid(x):
    return 1.0 / (1.0 + jnp.exp(-x))


def _silu(x):
    return x * _sigmoid(x)


def _gelu(x):
    return 0.5 * x * (1.0 + jnp.tanh(0.7978845608028654 * (x + 0.044715 * (x * x * x))))


def _dot(a, b):
    return jnp.dot(a, b, preferred_element_type=F32)


def _dot_nt(a, b):
    return lax.dot_general(a, b, (((1,), (1,)), ((), ())), preferred_element_type=F32)


def _dot_tn(a, b):
    return lax.dot_general(a, b, (((0,), (0,)), ((), ())), preferred_element_type=F32)


def _split3(a):
    hi = a.astype(BF16)
    r1 = a - hi.astype(F32)
    mid = r1.astype(BF16)
    lo = (r1 - mid.astype(F32)).astype(BF16)
    return hi, mid, lo


def _dot_f32_lhs(a, b_exact):
    hi, mid, lo = _split3(a)
    return _dot(hi, b_exact) + _dot(mid, b_exact) + _dot(lo, b_exact)


def _dot_f32_rhs(a_exact, b):
    hi, mid, lo = _split3(b)
    return _dot(a_exact, hi) + _dot(a_exact, mid) + _dot(a_exact, lo)


def _dot_nt_f32(a, b):
    ah, am, _ = _split3(a)
    bh, bm, _ = _split3(b)
    return _dot_nt(ah, bh) + _dot_nt(ah, bm) + _dot_nt(am, bh)


def _rms(x, g, eps=1e-6):
    return x * lax.rsqrt(jnp.mean(x * x, axis=-1, keepdims=True) + eps) * g


def _softmax_parts(s_list, m_list, never_empty=False):
    sm = [jnp.where(m, s, NEG_INF) for s, m in zip(s_list, m_list)]
    mx = functools.reduce(jnp.maximum, [jnp.max(s, axis=-1, keepdims=True) for s in sm])
    if never_empty:
        e = [jnp.exp(s - mx) for s in sm]
    else:
        e = [jnp.where(m, jnp.exp(s - mx), 0.0) for s, m in zip(sm, m_list)]
    den = functools.reduce(lambda a, b: a + b, [jnp.sum(x, axis=-1, keepdims=True) for x in e])
    return e, jnp.maximum(den, 1e-30)


def _rank(score, n, col):
    rank = jnp.zeros(score.shape, jnp.int32)
    for i in range(n):
        si = score[:, i:i + 1]
        before = jnp.where(si > score, 1, jnp.where((si == score) & (col > i), 1, 0))
        rank = rank + before
    return rank


def _ada_kernel(c_ref, w_ref, b_ref, o_ref):
    a = _silu(c_ref[...]).astype(BF16)
    o_ref[...] = _dot(a, w_ref[...].astype(BF16)) + b_ref[...]


def _ada_mod(c, w_ada, b_ada):
    depth, d, n = w_ada.shape
    rows = c.shape[0]
    tn = 512
    return pl.pallas_call(
        _ada_kernel,
        grid=(depth, n // tn),
        in_specs=[pl.BlockSpec((rows, d), lambda l, j: (0, 0)),
                  pl.BlockSpec((None, d, tn), lambda l, j: (l, 0, j)),
                  pl.BlockSpec((None, 1, tn), lambda l, j: (l, 0, j))],
        out_specs=pl.BlockSpec((None, rows, tn), lambda l, j: (l, 0, j)),
        out_shape=jax.ShapeDtypeStruct((depth, rows, n), F32),
        compiler_params=_params(("parallel", "parallel"), 40),
    )(c, w_ada, b_ada.reshape(depth, 1, n))


ROW_TILE = 1024
SAMPLE_PAD = 16
COMBINED_TILE = ROW_TILE + SAMPLE_PAD
COL_TILE = 256
NORM_CHUNK = 128


def _single_buffered(shape, index_map):
    return pl.BlockSpec(shape, index_map, pipeline_mode=pl.Buffered(1))


def _sample_out_spec(tn):
    return pl.BlockSpec((None, SAMPLE_PAD, tn), lambda i, j, *_: (i, 0, j))


def _stage_modulated(lhs_ref, xp_ref, xs_ref, g_ref, scp_ref, shp_ref, scs_ref, shs_ref):
    g = g_ref[...]
    one_sc, sh = 1.0 + scp_ref[...], shp_ref[...]

    def body(c, carry):
        r0 = pl.multiple_of(c * NORM_CHUNK, NORM_CHUNK)
        lhs_ref[pl.ds(r0, NORM_CHUNK), :] = (_rms(xp_ref[pl.ds(r0, NORM_CHUNK), :], g) * one_sc + sh).astype(BF16)
        return carry

    lax.fori_loop(0, ROW_TILE // NORM_CHUNK, body, 0)
    lhs_ref[ROW_TILE:, :] = (_rms(xs_ref[...], g) * (1.0 + scs_ref[...]) + shs_ref[...]).astype(BF16)


def _mod_specs(d, rows_per_group):
    per = rows_per_group // ROW_TILE
    vec_p = pl.BlockSpec((None, 1, d), lambda i, j, *_: (i // per, 0, 0))
    vec_s = pl.BlockSpec((SAMPLE_PAD, d), lambda i, j, *_: (0, 0))
    return [_single_buffered((ROW_TILE, d), lambda i, j, *_: (i, 0)), vec_s,
            pl.BlockSpec((1, d), lambda i, j, *_: (0, 0)), vec_p, vec_p, vec_s, vec_s]


def _mod_args(xp, xs, g, mod_p, mod_s, idx):
    d = xp.shape[1]
    return (xp, xs, g.reshape(1, d), mod_p[:, idx, 1].reshape(-1, 1, d), mod_p[:, idx, 0].reshape(-1, 1, d),
            mod_s[:, idx, 1], mod_s[:, idx, 0])


def _norm_glu_kernel(xp_ref, xs_ref, g_ref, scp_ref, shp_ref, scs_ref, shs_ref, wg_ref, wu_ref, o_ref, lhs_ref):
    @pl.when(pl.program_id(1) == 0)
    def _():
        _stage_modulated(lhs_ref, xp_ref, xs_ref, g_ref, scp_ref, shp_ref, scs_ref, shs_ref)

    h = lhs_ref[...]
    a = _dot(h, wg_ref[...].astype(BF16))
    b = _dot(h, wu_ref[...].astype(BF16))
    o_ref[...] = (_silu(a) * b).astype(o_ref.dtype)


def _norm_glu(xp, xs, g, mod_p, mod_s, idx, wg, wu, layer, sub, rows_per_group):
    m, d = xp.shape
    f = wg.shape[-1]
    nt = m // ROW_TILE
    wspec = pl.BlockSpec((None, None, d, COL_TILE), lambda i, j: (layer, sub, 0, j))
    return pl.pallas_call(
        _norm_glu_kernel,
        grid=(nt, f // COL_TILE),
        in_specs=_mod_specs(d, rows_per_group) + [wspec, wspec],
        out_specs=pl.BlockSpec((COMBINED_TILE, COL_TILE), lambda i, j: (i, j)),
        out_shape=jax.ShapeDtypeStruct((nt * COMBINED_TILE, f), BF16),
        scratch_shapes=[pltpu.VMEM((COMBINED_TILE, d), BF16)],
        compiler_params=_params(("parallel", "arbitrary"), 52),
    )(*_mod_args(xp, xs, g, mod_p, mod_s, idx), wg, wu)


DOWN_TILES = 2


def _down_kernel(x_ref, w_ref, rp_ref, rs_ref, gp_ref, gs_ref, op_ref, os_ref, *, coef):
    acc = _dot(x_ref[...], w_ref[...].astype(BF16))
    gp = coef * gp_ref[...]
    for r in range(DOWN_TILES):
        rows = slice(r * ROW_TILE, (r + 1) * ROW_TILE)
        op_ref[rows, :] = rp_ref[rows, :] + gp * acc[r * COMBINED_TILE:r * COMBINED_TILE + ROW_TILE]
    os_ref[...] = rs_ref[...] + (coef * gs_ref[...]) * acc[ROW_TILE:COMBINED_TILE]


def _down(act, w, layer, sub, xp, xs, gate_p, gate_s, coef):
    m, n = xp.shape
    kdim = act.shape[1]
    tk = kdim // 2
    rows = DOWN_TILES * ROW_TILE
    assert rows == m // gate_p.shape[0]
    for ks in range(2):
        xp, xs = pl.pallas_call(
            functools.partial(_down_kernel, coef=coef),
            grid=(m // rows, n // COL_TILE),
            in_specs=[_single_buffered((DOWN_TILES * COMBINED_TILE, tk), lambda i, j, ks=ks: (i, ks)),
                      pl.BlockSpec((None, None, tk, COL_TILE), lambda i, j, ks=ks: (layer, sub, ks, j)),
                      pl.BlockSpec((rows, COL_TILE), lambda i, j: (i, j)),
                      pl.BlockSpec((SAMPLE_PAD, COL_TILE), lambda i, j: (0, j)),
                      pl.BlockSpec((None, 1, COL_TILE), lambda i, j: (i, 0, j)),
                      pl.BlockSpec((SAMPLE_PAD, COL_TILE), lambda i, j: (0, j))],
            out_specs=[pl.BlockSpec((rows, COL_TILE), lambda i, j: (i, j)), _sample_out_spec(COL_TILE)],
            out_shape=[jax.ShapeDtypeStruct((m, n), F32), jax.ShapeDtypeStruct((m // rows, SAMPLE_PAD, n), F32)],
            compiler_params=_params(("parallel", "arbitrary"), 56),
        )(act, w, xp, xs, gate_p.reshape(-1, 1, n), gate_s)
        xs = xs[0]
    return xp, xs


def _out_proj_kernel(*refs, n_parts):
    x_refs = refs[:n_parts]
    xs_ref, w_ref, rp_ref, rs_ref, gp_ref, gs_ref, op_ref, os_ref, lhs_ref = refs[n_parts:]

    @pl.when(pl.program_id(1) == 0)
    def _():
        off = 0
        for x_ref in x_refs:
            lhs_ref[:ROW_TILE, off:off + x_ref.shape[1]] = x_ref[...]
            off += x_ref.shape[1]
        lhs_ref[ROW_TILE:, :] = xs_ref[...]

    acc = _dot(lhs_ref[...], w_ref[...].astype(BF16))
    op_ref[...] = rp_ref[...] + gp_ref[...] * acc[:ROW_TILE]
    os_ref[...] = rs_ref[...] + gs_ref[...] * acc[ROW_TILE:]


def _out_proj(parts, mixed_s, w, layer, xp, xs, gate_p, gate_s, rows_per_group):
    m, n = xp.shape
    kdim = w.shape[1]
    per = rows_per_group // ROW_TILE
    in_specs = [pl.BlockSpec((ROW_TILE, x.shape[1]), lambda i, j: (i, 0)) for x in parts]
    in_specs += [pl.BlockSpec((SAMPLE_PAD, kdim), lambda i, j: (0, 0)),
                 pl.BlockSpec((None, kdim, COL_TILE), lambda i, j: (layer, 0, j)),
                 pl.BlockSpec((ROW_TILE, COL_TILE), lambda i, j: (i, j)),
                 pl.BlockSpec((SAMPLE_PAD, COL_TILE), lambda i, j: (0, j)),
                 pl.BlockSpec((None, 1, COL_TILE), lambda i, j: (i // per, 0, j)),
                 pl.BlockSpec((SAMPLE_PAD, COL_TILE), lambda i, j: (0, j))]
    xp, xs = pl.pallas_call(
        functools.partial(_out_proj_kernel, n_parts=len(parts)),
        grid=(m // ROW_TILE, n // COL_TILE),
        in_specs=in_specs,
        out_specs=[pl.BlockSpec((ROW_TILE, COL_TILE), lambda i, j: (i, j)), _sample_out_spec(COL_TILE)],
        out_shape=[jax.ShapeDtypeStruct((m, n), F32), jax.ShapeDtypeStruct((m // ROW_TILE, SAMPLE_PAD, n), F32)],
        scratch_shapes=[pltpu.VMEM((COMBINED_TILE, kdim), BF16)],
        compiler_params=_params(("parallel", "arbitrary"), 48),
    )(*parts, mixed_s, w, xp, xs, gate_p.reshape(-1, 1, n), gate_s)
    return xp, xs[0]


def _tiles(name, width):
    a = SEG[name] // PROJ_TN
    return a, a + width // PROJ_TN


_NORM_TILES = (_tiles("qa", 1024), _tiles("qc", 1024), _tiles("kvs", 256), _tiles("kvw", 256), _tiles("kvm", 256))
_GELU_TILES = (_tiles("u", 2048),)
_SILU_TILES = (_tiles("qd", 1024),)
_SIGM_TILES = (_tiles("ga", 256),)


def _in_ranges(j, ranges):
    return functools.reduce(jnp.logical_or, [(j >= a) & (j < b) for a, b in ranges])


def _proj_kernel(off_ref, xp_ref, xs_ref, g_ref, scp_ref, shp_ref, scs_ref, shs_ref, w_ref, gain_ref, zp_ref, zs_ref,
                 lhs_ref):
    j = pl.program_id(1)

    @pl.when(j == 0)
    def _():
        _stage_modulated(lhs_ref, xp_ref, xs_ref, g_ref, scp_ref, shp_ref, scs_ref, shs_ref)

    lhs = lhs_ref[...]
    accs = [_dot_nt(lhs, w_ref[0, c * PROJ_TN:(c + 1) * PROJ_TN, :].astype(BF16)) for c in range(PROJ_TILES_PER_STEP)]
    for c, acc in enumerate(accs):
        _proj_epilogue(j * PROJ_TILES_PER_STEP + c, acc, gain_ref, zp_ref, zs_ref,
                       slice(c * PROJ_TN, (c + 1) * PROJ_TN))


def _proj_epilogue(tile, acc, gain_ref, zp_ref, zs_ref, cols):
    is_norm = _in_ranges(tile, _NORM_TILES)
    is_gelu = _in_ranges(tile, _GELU_TILES)
    is_silu = _in_ranges(tile, _SILU_TILES)
    is_sigm = _in_ranges(tile, _SIGM_TILES)

    def store(val):
        zp_ref[:, cols] = val[:ROW_TILE]
        zs_ref[:, cols] = val[ROW_TILE:]

    @pl.when(is_norm)
    def _():
        gain = gain_ref[:, cols]
        heads = [slice(hh * HEAD_DIM, (hh + 1) * HEAD_DIM) for hh in range(PROJ_TN // HEAD_DIM)]
        store(jnp.concatenate([_rms(acc[:, sl], gain[:, sl]) for sl in heads], axis=1))

    @pl.when(is_gelu)
    def _():
        store(_gelu(acc))

    @pl.when(is_silu)
    def _():
        store(_silu(acc) * ATT_SCALE)

    @pl.when(is_sigm)
    def _():
        store(_sigmoid(acc))

    @pl.when(jnp.logical_not(is_norm | is_gelu | is_silu | is_sigm))
    def _():
        store(acc)


def _proj_row_offsets():
    tn = PROJ_TILES_PER_STEP * PROJ_TN
    offs = []
    for name in ("qa", "u", "v", "qc", "qd", "fd", "vd", "gd", "kvc", "kvs", "kvw", "kvm"):
        a, w = _ORIG[name]
        offs += [a + k * tn for k in range(w // tn)]
    offs.append(_ORIG["ga"][0])
    assert len(offs) == N_PROJ // tn and all(o % 8 == 0 for o in offs)
    return offs


def _proj(xp, xs, g, mod_p, mod_s, idx, w_in, layer, gains, rows_per_group):
    m, d = xp.shape
    tn = PROJ_TILES_PER_STEP * PROJ_TN
    w_t = jnp.swapaxes(w_in, 1, 2)
    w_spec = pl.BlockSpec((pl.Element(1), pl.Element(tn), pl.Element(d)), lambda i, j, off: (layer, off[j] * 8, 0))
    zp, zs = pl.pallas_call(
        _proj_kernel,
        grid_spec=pltpu.PrefetchScalarGridSpec(
            num_scalar_prefetch=1, grid=(m // ROW_TILE, N_PROJ // tn),
            in_specs=_mod_specs(d, rows_per_group) + [w_spec,
                                                       pl.BlockSpec((None, 1, tn), lambda i, j, off: (j, 0, 0))],
            out_specs=[pl.BlockSpec((ROW_TILE, tn), lambda i, j, off: (i, j)), _sample_out_spec(tn)],
            scratch_shapes=[pltpu.VMEM((COMBINED_TILE, d), BF16)]),
        out_shape=[jax.ShapeDtypeStruct((m, N_PROJ), F32),
                   jax.ShapeDtypeStruct((m // ROW_TILE, SAMPLE_PAD, N_PROJ), F32)],
        compiler_params=_params(("parallel", "arbitrary"), 52),
    )(jnp.asarray(_proj_row_offsets(), jnp.int32) // 8, *_mod_args(xp, xs, g, mod_p, mod_s, idx), w_t, gains)
    return zp, zs[0]


GATES_PER_KV = 3 * GROUP


def _proj_gains(qkg):
    g = jnp.ones((N_PROJ // PROJ_TN, PROJ_TN), F32)
    two = lambda v: jnp.tile(v, PROJ_TN // HEAD_DIM)
    for name, idx, ntile in (("qa", 0, 4), ("qc", 4, 4), ("kvs", 2, 1), ("kvw", 3, 1), ("kvm", 5, 1)):
        t0 = SEG[name] // PROJ_TN
        g = g.at[t0:t0 + ntile].set(two(qkg[idx])[None, :])
    return g.reshape(-1, 1, PROJ_TILES_PER_STEP * PROJ_TN)


CMP_ROWS = 128


def _compress_body(x_of, w1_ref, b1_ref, w2_ref, pe_ref, gk_ref, o_ref, n_cmp):
    i = pl.program_id(1)
    rows = o_ref.shape[2]
    row = lax.broadcasted_iota(jnp.int32, (rows, 1), 0)
    valid = (i * rows + row) < n_cmp
    half = NSA_CMP_STRIDE * HEAD_DIM
    hid_w = NSA_CMP_HIDDEN
    for s in range(2):
        w1ab = w1_ref[s]
        w1b = w1ab[:, hid_w:]
        pe = pe_ref[s]
        bias = (_dot(pe[:, :half], w1ab[:, :hid_w]) + _dot(pe[:, half:], w1b))[0:1] + b1_ref[s]
        w2 = w2_ref[s]
        xs = [x_of(s * KV_HEADS + k) for k in range(KV_HEADS)]
        h = _dot(jnp.concatenate([x.astype(BF16) for x, _ in xs], axis=0), w1ab)
        hn = _dot(jnp.concatenate([xn.astype(BF16) for _, xn in xs], axis=0), w1b)
        for k in range(KV_HEADS):
            h1 = h[k * rows:(k + 1) * rows, :hid_w]
            h2 = h[k * rows:(k + 1) * rows, hid_w:]
            h2s = jnp.where(row == rows - 1, hn[k * 8:k * 8 + 1], pltpu.roll(h2, rows - 1, 0))
            hid = _gelu(h1 + h2s + bias)
            out = _dot(hid.astype(BF16), w2)
            if s == 0:
                out = _rms(out, gk_ref[...])
            o_ref[s, k] = jnp.where(valid, out, 0.0)


def _compress_kernel(r_ref, rn_ref, w1_ref, b1_ref, w2_ref, pe_ref, gk_ref, o_ref, *, n_cmp):
    row_w = 2 * KV_HEADS * HEAD_DIM

    def x_of(c):
        cols = [slice(p * row_w + c * HEAD_DIM, p * row_w + (c + 1) * HEAD_DIM) for p in range(NSA_CMP_STRIDE)]
        return (jnp.concatenate([r_ref[:, sl] for sl in cols], axis=1),
                jnp.concatenate([rn_ref[:, sl] for sl in cols], axis=1))

    _compress_body(x_of, w1_ref, b1_ref, w2_ref, pe_ref, gk_ref, o_ref, n_cmp)


PAGE_HALVES = PAGE_SIZE // NSA_CMP_STRIDE
PAGE_STREAMS = 2 * KV_HEADS


def _compress_paged_kernel(pt_ref, *refs, n_cmp):
    pages = refs[:PAGES_PER_STEP]
    nxt = refs[PAGES_PER_STEP]
    w1_ref, b1_ref, w2_ref, pe_ref, gk_ref, o_ref = refs[PAGES_PER_STEP + 1:]

    def flat(pg, c):
        return jnp.concatenate([pg[pl.ds(p * PAGE_STREAMS + c, PAGE_HALVES, stride=NSA_CMP_STRIDE * PAGE_STREAMS), :]
                                for p in range(NSA_CMP_STRIDE)], axis=1)

    def x_of(c):
        return jnp.concatenate([flat(pg, c) for pg in pages], axis=0), flat(nxt, c)

    _compress_body(x_of, w1_ref, b1_ref, w2_ref, pe_ref, gk_ref, o_ref, n_cmp)


def _compress_weights(w1, b1, w2, pe, gk):
    half = NSA_CMP_STRIDE * HEAD_DIM
    w1r = w1.reshape(2, 2, half, NSA_CMP_HIDDEN)
    w1ab = jnp.concatenate([w1r[:, 0], w1r[:, 1]], axis=-1)
    pe8 = jnp.broadcast_to(pe.reshape(2, 1, 2 * half), (2, 8, 2 * half))
    args = (w1ab.astype(BF16), b1.reshape(2, 1, NSA_CMP_HIDDEN), w2.astype(BF16), pe8.astype(BF16),
            gk.reshape(1, HEAD_DIM))
    return args, [a.shape for a in args]


def _compress(kv_flat, batch, w1, b1, w2, pe, gk):
    n_half = kv_flat.shape[0] // batch // NSA_CMP_STRIDE
    nblk = n_half // CMP_ROWS
    width = NSA_CMP_STRIDE * kv_flat.shape[1]
    r = kv_flat.reshape(batch * n_half, width)
    last8 = batch * n_half // 8 - 1
    wargs, wshapes = _compress_weights(w1, b1, w2, pe, gk)
    full = lambda shape: pl.BlockSpec(shape, lambda b, i: (0,) * len(shape))
    return pl.pallas_call(
        functools.partial(_compress_kernel, n_cmp=n_half - 1),
        grid=(batch, nblk),
        in_specs=[pl.BlockSpec((CMP_ROWS, width), lambda b, i: (b * nblk + i, 0)),
                  pl.BlockSpec((8, width), lambda b, i: (jnp.minimum((b * nblk + i + 1) * (CMP_ROWS // 8), last8), 0))]
                 + [full(s) for s in wshapes],
        out_specs=pl.BlockSpec((None, 2, KV_HEADS, CMP_ROWS, HEAD_DIM), lambda b, i: (b, 0, 0, i, 0)),
        out_shape=jax.ShapeDtypeStruct((batch, 2, KV_HEADS, n_half, HEAD_DIM), F32),
        compiler_params=_params(("parallel", "arbitrary"), 48),
    )(r, r, *wargs)


def _page_view(pool):
    depth, n_phys = pool.shape[:2]
    return pool.reshape(depth, n_phys, PAGE_SIZE * PAGE_STREAMS, HEAD_DIM)


def _compress_paged(pool, layer, page_table, w1, b1, w2, pe, gk):
    view = _page_view(pool)
    batch, n_pages = page_table.shape
    groups = n_pages // PAGES_PER_STEP
    n_half = n_pages * PAGE_HALVES
    wargs, wshapes = _compress_weights(w1, b1, w2, pe, gk)
    page = lambda fn: pl.BlockSpec((None, None, PAGE_SIZE * PAGE_STREAMS, HEAD_DIM), fn)
    in_specs = [page(lambda b, g, pt, r=r: (layer, pt[b, g * PAGES_PER_STEP + r], 0, 0)) for r in range(PAGES_PER_STEP)]
    in_specs.append(page(lambda b, g, pt: (layer, pt[b, jnp.minimum((g + 1) * PAGES_PER_STEP, n_pages - 1)], 0, 0)))
    in_specs += [pl.BlockSpec(s, lambda b, g, pt, n=len(s): (0,) * n) for s in wshapes]
    return pl.pallas_call(
        functools.partial(_compress_paged_kernel, n_cmp=n_half - 1),
        grid_spec=pltpu.PrefetchScalarGridSpec(
            num_scalar_prefetch=1, grid=(batch, groups), in_specs=in_specs,
            out_specs=pl.BlockSpec((None, 2, KV_HEADS, PAGES_PER_STEP * PAGE_HALVES, HEAD_DIM),
                                   lambda b, g, pt: (b, 0, 0, g, 0))),
        out_shape=jax.ShapeDtypeStruct((batch, 2, KV_HEADS, n_half, HEAD_DIM), F32),
        compiler_params=_params(("parallel", "arbitrary"), 56),
    )(page_table, *([view] * (PAGES_PER_STEP + 1)), *wargs)


def _stack_heads(q):
    return jnp.concatenate([q[:, g * HEAD_DIM:(g + 1) * HEAD_DIM] for g in range(GROUP)], axis=0)


def _block_onehot(n_blocks, kpos, shift):
    blk = lax.broadcasted_iota(jnp.int32, (n_blocks, 1), 0)
    return jnp.where((kpos >> shift) == blk, 1.0, 0.0).astype(BF16)


def _nsa_kernel(*refs, tq, qpos0, n_sel, extents, has_tail, win_pos0, win_len):
    if len(extents) > 1:
        refs, osel_ref = refs[:-1], refs[-1]
    if has_tail:
        q_ref, ga_ref, kc_ref, vc_ref, ks_ref, vs_ref, kt_ref, vt_ref, kw_ref, vw_ref, o_ref = refs
    else:
        q_ref, ga_ref, kc_ref, vc_ref, ks_ref, vs_ref, kw_ref, vw_ref, o_ref = refs
    q0 = qpos0 + pl.program_id(2) * tq
    rows = GROUP * tq
    q4 = _stack_heads(q_ref[...]).astype(BF16)
    qpos = q0 + (lax.broadcasted_iota(jnp.int32, (rows, 1), 0) & (tq - 1))

    kc = kc_ref[...].astype(BF16)
    vc = vc_ref[...].astype(BF16)
    n_cp = kc.shape[0]
    cidx = lax.broadcasted_iota(jnp.int32, (1, n_cp), 1)
    s_cmp = _dot_nt(q4, kc) * ATT_SCALE
    (e_cmp,), den = _softmax_parts([s_cmp], [(cidx * NSA_CMP_STRIDE + (NSA_CMP_BLOCK - 1)) <= qpos])
    p_cmp = e_cmp / den
    o_cmp = _dot(p_cmp.astype(BF16), vc)

    n_sp = -(-n_sel // LANE) * LANE
    psum = functools.reduce(lambda a, b: a + b, [p_cmp[g * tq:(g + 1) * tq] for g in range(GROUP)])
    ci = lax.broadcasted_iota(jnp.int32, (n_cp, 1), 0) * NSA_CMP_STRIDE
    sj = lax.broadcasted_iota(jnp.int32, (1, n_sp), 1) * NSA_SEL_BLOCK
    cover = jnp.where((ci <= sj + (NSA_SEL_BLOCK - 1)) & (ci + (NSA_CMP_BLOCK - 1) >= sj), 1.0, 0.0).astype(BF16)
    imp = _dot_f32_lhs(psum, cover)
    col = lax.broadcasted_iota(jnp.int32, (1, n_sp), 1)
    qpos_t = q0 + lax.broadcasted_iota(jnp.int32, (tq, 1), 0)
    back = (qpos_t >> 6) - col
    valid = back >= 0
    forced = (col == 0) | (valid & (back < NSA_N_LOCAL))
    score = jnp.where(valid, jnp.where(forced, FORCE_SCORE, imp), NEG_INF)
    rank = _rank(score, n_sel, col)
    sel = jnp.where((rank < NSA_SEL_TOP) & (score > 0.5 * NEG_INF), 1.0, 0.0).astype(BF16)
    sel4 = jnp.concatenate([sel] * GROUP, axis=0)

    def sel_attend(n_keys):
        s_list, m_list, v_list = [], [], []

        def add_tile(k_bf, v_bf, kpos):
            picked = _dot(sel4, _block_onehot(n_sp, kpos, 6))
            s_list.append(_dot_nt(q4, k_bf) * ATT_SCALE)
            m_list.append(jnp.where(kpos <= qpos, picked, 0.0) > 0.5)
            v_list.append(v_bf)

        kt = min(n_keys, KEY_TILE)
        for t0 in range(0, n_keys, kt):
            add_tile(ks_ref[t0:t0 + kt, :].astype(BF16), vs_ref[t0:t0 + kt, :].astype(BF16),
                     t0 + lax.broadcasted_iota(jnp.int32, (1, kt), 1))
        if has_tail:
            add_tile(kt_ref[...].astype(BF16), vt_ref[...].astype(BF16),
                     n_keys + lax.broadcasted_iota(jnp.int32, (1, TAIL_ROWS), 1))
        e_list, den = _softmax_parts(s_list, m_list, never_empty=True)
        return functools.reduce(lambda a, b: a + b, [_dot(e.astype(BF16), v) for e, v in zip(e_list, v_list)]) / den

    if len(extents) == 1:
        o_sel = sel_attend(extents[0])
    else:
        need = (q0 - qpos0 + tq - 1) // extents[0]
        for idx, ext in enumerate(extents):
            @pl.when(need == idx)
            def _(ext=ext):
                osel_ref[...] = sel_attend(ext)
        o_sel = osel_ref[...]

    start = jnp.clip(q0 - NSA_WINDOW - win_pos0, 0, win_len - WIN_BAND)
    start = pl.multiple_of(start, LANE)
    kw = kw_ref[pl.ds(start, WIN_BAND), :].astype(BF16)
    vw = vw_ref[pl.ds(start, WIN_BAND), :].astype(BF16)
    kposw = win_pos0 + start + lax.broadcasted_iota(jnp.int32, (1, WIN_BAND), 1)
    s_win = _dot_nt(q4, kw) * ATT_SCALE
    m_win = jnp.where(kposw <= qpos, qpos - kposw, NSA_WINDOW) < NSA_WINDOW
    (e_win,), den = _softmax_parts([s_win], [m_win], never_empty=True)
    o_win = _dot(e_win.astype(BF16), vw) / den

    ga = ga_ref[...]
    ga = jnp.where(pl.program_id(1) == 0, ga, pltpu.roll(ga, LANE - GATES_PER_KV, 1))
    outs = []
    for g in range(GROUP):
        r = slice(g * tq, (g + 1) * tq)
        outs.append(ga[:, 3 * g:3 * g + 1] * o_cmp[r] + ga[:, 3 * g + 1:3 * g + 2] * o_sel[r]
                    + ga[:, 3 * g + 2:3 * g + 3] * o_win[r])
    o_ref[...] = jnp.concatenate(outs, axis=1).astype(o_ref.dtype)


CAUSAL_STEP = 256


def _causal_extents(tk, tq):
    if tk % CAUSAL_STEP or CAUSAL_STEP % tq:
        return (tk,)
    return tuple(range(CAUSAL_STEP, tk + 1, CAUSAL_STEP))


def _nsa(zq, tq_total, tq, qpos0, cmp_kv, sel_main, sel_cols, sel_tail, win, win_cols, win_pos0, n_total,
         causal_skip):
    batch = cmp_kv.shape[0]
    nq = tq_total // tq
    n_cp = cmp_kv.shape[3]
    tk = sel_main.shape[0] // batch
    lw = win.shape[0] // batch
    n_sel = -(-n_total // NSA_SEL_BLOCK)
    extents = _causal_extents(tk, tq) if causal_skip else (tk,)
    scratch = [pltpu.VMEM((GROUP * tq, HEAD_DIM), F32)] if len(extents) > 1 else []
    qa_blk = SEG["qa"] // (GROUP * HEAD_DIM)
    ga_blk = SEG["ga"] // LANE
    in_specs = [
        pl.BlockSpec((tq, GROUP * HEAD_DIM), lambda b, k, i: (b * nq + i, qa_blk + k)),
        pl.BlockSpec((tq, LANE), lambda b, k, i: (b * nq + i, ga_blk)),
        pl.BlockSpec((None, None, None, n_cp, HEAD_DIM), lambda b, k, i: (b, 0, k, 0, 0)),
        pl.BlockSpec((None, None, None, n_cp, HEAD_DIM), lambda b, k, i: (b, 1, k, 0, 0)),
        pl.BlockSpec((tk, HEAD_DIM), lambda b, k, i: (b, sel_cols[0] + k)),
        pl.BlockSpec((tk, HEAD_DIM), lambda b, k, i: (b, sel_cols[1] + k)),
    ]
    args = [zq, zq, cmp_kv, cmp_kv, sel_main, sel_main]
    if sel_tail is not None:
        in_specs += [pl.BlockSpec((TAIL_ROWS, HEAD_DIM), lambda b, k, i: (b, k)),
                     pl.BlockSpec((TAIL_ROWS, HEAD_DIM), lambda b, k, i: (b, KV_HEADS + k))]
        args += [sel_tail, sel_tail]
    in_specs += [pl.BlockSpec((lw, HEAD_DIM), lambda b, k, i: (b, win_cols[0] + k)),
                 pl.BlockSpec((lw, HEAD_DIM), lambda b, k, i: (b, win_cols[1] + k))]
    args += [win, win]
    return pl.pallas_call(
        functools.partial(_nsa_kernel, tq=tq, qpos0=qpos0, n_sel=n_sel, extents=extents,
                          has_tail=sel_tail is not None, win_pos0=win_pos0, win_len=lw),
        grid=(batch, KV_HEADS, nq),
        in_specs=in_specs,
        out_specs=pl.BlockSpec((tq, GROUP * HEAD_DIM), lambda b, k, i: (b * nq + i, k)),
        out_shape=jax.ShapeDtypeStruct((batch * tq_total, WIDTH), BF16),
        scratch_shapes=scratch,
        compiler_params=_params(("parallel", "parallel", "arbitrary"), 56),
    )(*args)


def _moba_kernel(*refs, tq, qpos0, extents, has_tail):
    if has_tail:
        q_ref, k_ref, v_ref, kt_ref, vt_ref, o_ref = refs
    else:
        q_ref, k_ref, v_ref, o_ref = refs
    q0 = qpos0 + pl.program_id(2) * tq
    rows = GROUP * tq
    q4f = _stack_heads(q_ref[...])
    q4 = q4f.astype(BF16)
    qpos = q0 + (lax.broadcasted_iota(jnp.int32, (rows, 1), 0) & (tq - 1))
    cur = qpos >> 8
    col = lax.broadcasted_iota(jnp.int32, (1, LANE), 1)

    def attend(n_keys):
        kt = min(n_keys, KEY_TILE)
        n_blk = n_keys // MOBA_BLOCK
        means = [jnp.sum(k_ref[t0:t0 + kt, :].reshape(kt // MOBA_BLOCK, MOBA_BLOCK, HEAD_DIM), axis=1)
                 * (1.0 / MOBA_BLOCK) for t0 in range(0, n_keys, kt)]
        if n_blk < LANE:
            means.append(jnp.zeros((LANE - n_blk, HEAD_DIM), F32))
        kmean = jnp.concatenate(means, axis=0)
        gate = jnp.where(col < cur, _dot_nt_f32(q4f, kmean), NEG_INF)
        rank = _rank(gate, n_blk, col)
        sel = jnp.where((rank < MOBA_TOP) & (gate > 0.5 * NEG_INF), 1.0, 0.0).astype(BF16)

        s_list, m_list, v_list = [], [], []

        def add_tile(k_bf, v_bf, kpos):
            picked = _dot(sel, _block_onehot(LANE, kpos, 8))
            own = jnp.where((kpos >> 8) == cur, jnp.where(kpos <= qpos, 1.0, 0.0), 0.0)
            s_list.append(_dot_nt(q4, k_bf) * ATT_SCALE)
            m_list.append((picked + own) > 0.5)
            v_list.append(v_bf)

        for t0 in range(0, n_keys, kt):
            add_tile(k_ref[t0:t0 + kt, :].astype(BF16), v_ref[t0:t0 + kt, :].astype(BF16),
                     t0 + lax.broadcasted_iota(jnp.int32, (1, kt), 1))
        if has_tail:
            add_tile(kt_ref[...].astype(BF16), vt_ref[...].astype(BF16),
                     n_keys + lax.broadcasted_iota(jnp.int32, (1, TAIL_ROWS), 1))
        e_list, den = _softmax_parts(s_list, m_list, never_empty=True)
        o = functools.reduce(lambda a, b: a + b, [_dot(e.astype(BF16), v) for e, v in zip(e_list, v_list)]) / den
        o_ref[...] = jnp.concatenate([o[g * tq:(g + 1) * tq] for g in range(GROUP)], axis=1).astype(o_ref.dtype)

    if len(extents) == 1:
        attend(extents[0])
    else:
        need = (q0 - qpos0 + tq - 1) // extents[0]
        for idx, ext in enumerate(extents):
            @pl.when(need == idx)
            def _(ext=ext):
                attend(ext)


def _moba(zq, tq_total, tq, qpos0, batch, main, cols, tail, causal_skip):
    nq = tq_total // tq
    tk = main.shape[0] // batch
    extents = _causal_extents(tk, tq) if causal_skip else (tk,)
    qc_blk = SEG["qc"] // (GROUP * HEAD_DIM)
    in_specs = [pl.BlockSpec((tq, GROUP * HEAD_DIM), lambda b, k, i: (b * nq + i, qc_blk + k)),
                pl.BlockSpec((tk, HEAD_DIM), lambda b, k, i: (b, cols[0] + k)),
                pl.BlockSpec((tk, HEAD_DIM), lambda b, k, i: (b, cols[1] + k))]
    args = [zq, main, main]
    if tail is not None:
        in_specs += [pl.BlockSpec((TAIL_ROWS, HEAD_DIM), lambda b, k, i: (b, k)),
                     pl.BlockSpec((TAIL_ROWS, HEAD_DIM), lambda b, k, i: (b, KV_HEADS + k))]
        args += [tail, tail]
    return pl.pallas_call(
        functools.partial(_moba_kernel, tq=tq, qpos0=qpos0, extents=extents, has_tail=tail is not None),
        grid=(batch, KV_HEADS, nq),
        in_specs=in_specs,
        out_specs=pl.BlockSpec((tq, GROUP * HEAD_DIM), lambda b, k, i: (b * nq + i, k)),
        out_shape=jax.ShapeDtypeStruct((batch * tq_total, WIDTH), BF16),
        compiler_params=_params(("parallel", "parallel", "arbitrary"), 56),
    )(*args)


def _layer_norm(v, g, b, eps=1e-5):
    mu = jnp.mean(v, axis=-1, keepdims=True)
    var = jnp.mean(jnp.square(v - mu), axis=-1, keepdims=True)
    return (v - mu) * lax.rsqrt(var + eps) * g + b


def _gmlp_kernel(u_ref, v_ref, ws_ref, bst_ref, g_ref, b_ref, o_ref):
    vn = _layer_norm(v_ref[...], g_ref[...], b_ref[...]).astype(BF16)
    r = lax.broadcasted_iota(jnp.int32, (GMLP_CHUNK, GMLP_CHUNK), 0)
    c = lax.broadcasted_iota(jnp.int32, (GMLP_CHUNK, GMLP_CHUNK), 1)
    bst = bst_ref[...]
    for g in range(N_HEADS):
        sl = slice(g * HEAD_DIM, (g + 1) * HEAD_DIM)
        wm = jnp.where(c <= r, ws_ref[g], 0.0).astype(BF16)
        sv = _dot(wm, vn[:, sl]) + bst[:, g:g + 1]
        o_ref[:, sl] = (u_ref[:, sl] * sv).astype(o_ref.dtype)


def _gmlp(z, ws, bs, ln_g, ln_b):
    m = z.shape[0]
    ub, vb = SEG["u"] // WIDTH, SEG["v"] // WIDTH
    full = lambda shape: pl.BlockSpec(shape, lambda i: (0,) * len(shape))
    return pl.pallas_call(
        _gmlp_kernel,
        grid=(m // GMLP_CHUNK,),
        in_specs=[pl.BlockSpec((GMLP_CHUNK, WIDTH), lambda i: (i, ub)),
                  pl.BlockSpec((GMLP_CHUNK, WIDTH), lambda i: (i, vb)),
                  full(ws.shape), full((GMLP_CHUNK, N_HEADS)), full((1, WIDTH)), full((1, WIDTH))],
        out_specs=pl.BlockSpec((GMLP_CHUNK, WIDTH), lambda i: (i, 0)),
        out_shape=jax.ShapeDtypeStruct((m, WIDTH), BF16),
        compiler_params=_params(("parallel",), 32),
    )(z, z, ws, bs.T, ln_g.reshape(1, WIDTH), ln_b.reshape(1, WIDTH))


def _hgrn_gates(fd, lb):
    f_gate = lb + (1.0 - lb) * _sigmoid(fd)
    return jnp.log(jnp.maximum(f_gate, TINY)), (1.0 - lb) * _sigmoid(-fd)


def _hgrn_readout(o, gd, g_out):
    return _rms(o, g_out) * _silu(gd)


HGRN_HEADS_PER_STEP = 4


def _hgrn_kernel(q_ref, f_ref, v_ref, gd_ref, lb_ref, go_ref, o_ref, s_ref):
    c = HGRN_CHUNK
    t_total = q_ref.shape[0]
    g_out = go_ref[...]
    ri = lax.broadcasted_iota(jnp.int32, (c, c), 0)
    ci = lax.broadcasted_iota(jnp.int32, (c, c), 1)
    tri = jnp.where(ci <= ri, 1.0, 0.0).astype(BF16)
    lane = lax.broadcasted_iota(jnp.int32, (1, c), 1)
    diag_mask = (ci <= ri) & ((ci >> 3) == (ri >> 3))

    def head_chunk(r0, hh, st):
        hs = slice(hh * HEAD_DIM, (hh + 1) * HEAD_DIM)
        q = q_ref[pl.ds(r0, c), hs]
        v = v_ref[pl.ds(r0, c), hs]
        lf, k = _hgrn_gates(f_ref[pl.ds(r0, c), hs], lb_ref[:, hs])
        cum = _dot_f32_rhs(tri, lf)
        a = jnp.zeros((c, c), F32)
        for bs in (32, 16, 8):
            nb = c // bs
            refq = jnp.concatenate(
                [jnp.zeros((bs, HEAD_DIM), F32)]
                + [jnp.broadcast_to(cum[b * bs - 1:b * bs], (bs, HEAD_DIM)) for b in range(1, nb)], axis=0)
            refk = jnp.concatenate(
                [jnp.broadcast_to(cum[(b + 1) * bs - 1:(b + 1) * bs], (bs, HEAD_DIM)) for b in range(nb)], axis=0)
            qt = (q * jnp.exp(jnp.minimum(cum - refq, 0.0))).astype(BF16)
            kt = (k * jnp.exp(jnp.minimum(refk - cum, 0.0))).astype(BF16)
            sh = bs.bit_length() - 1
            lvl = (((ri >> sh) & 1) == 1) & ((ci >> sh) == (ri >> sh) - 1)
            a = a + jnp.where(lvl, _dot_nt(qt, kt), 0.0)
        rows = []
        for blk in range(c // 8):
            b0 = blk * 8
            qb, kb, cb = q[b0:b0 + 8], k[b0:b0 + 8], cum[b0:b0 + 8]
            acc = jnp.zeros((8, c), F32)
            for s in range(8):
                w = qb * kb[s:s + 1] * jnp.exp(jnp.minimum(cb - cb[s:s + 1], 0.0))
                acc = jnp.where(lane == b0 + s, jnp.sum(w, axis=1, keepdims=True), acc)
            rows.append(acc)
        a = a + jnp.where(diag_mask, jnp.concatenate(rows, axis=0), 0.0)
        o = _dot(a.astype(BF16), v.astype(BF16)) + _dot_nt((q * jnp.exp(cum)).astype(BF16), st.astype(BF16))
        o_ref[pl.ds(r0, c), hs] = _hgrn_readout(o, gd_ref[pl.ds(r0, c), hs], g_out).astype(o_ref.dtype)
        last = cum[c - 1:c]
        kk = (k * jnp.exp(last - cum)).astype(BF16)
        return st * jnp.exp(last) + _dot_tn(v.astype(BF16), kk)

    def chunk(n, sts):
        r0 = pl.multiple_of(n * c, c)
        return tuple(head_chunk(r0, hh, st) for hh, st in enumerate(sts))

    zero = jnp.zeros((HEAD_DIM, HEAD_DIM), F32)
    sts = lax.fori_loop(0, t_total // c, chunk, (zero,) * HGRN_HEADS_PER_STEP, unroll=2)
    for hh, st in enumerate(sts):
        s_ref[hh] = st.T


def _hgrn(z, batch, lb, g_out):
    t = z.shape[0] // batch
    hw = HGRN_HEADS_PER_STEP * HEAD_DIM
    blk = lambda name: SEG[name] // hw
    col = lambda name: pl.BlockSpec((t, hw), lambda b, h: (b, blk(name) + h))
    return pl.pallas_call(
        _hgrn_kernel,
        grid=(batch, N_HEADS // HGRN_HEADS_PER_STEP),
        in_specs=[col("qd"), col("fd"), col("vd"), col("gd"),
                  pl.BlockSpec((1, hw), lambda b, h: (0, h)),
                  pl.BlockSpec((1, HEAD_DIM), lambda b, h: (0, 0))],
        out_specs=[pl.BlockSpec((t, hw), lambda b, h: (b, h)),
                   pl.BlockSpec((None, HGRN_HEADS_PER_STEP, HEAD_DIM, HEAD_DIM), lambda b, h: (b, h, 0, 0))],
        out_shape=[jax.ShapeDtypeStruct((batch * t, WIDTH), BF16),
                   jax.ShapeDtypeStruct((batch, N_HEADS, HEAD_DIM, HEAD_DIM), F32)],
        compiler_params=_params(("parallel", "parallel"), 48),
    )(z, z, z, z, lb.reshape(1, WIDTH), g_out.reshape(1, HEAD_DIM))


def _gmlp_step_kernel(u_ref, v_ref, w_ref, b_ref, g_ref, bb_ref, o_ref, vn_ref):
    vn = _layer_norm(v_ref[...], g_ref[...], bb_ref[...])
    vn_ref[...] = vn
    o_ref[...] = (u_ref[...] * (w_ref[...] * vn + b_ref[...])).astype(o_ref.dtype)


def _gmlp_step(u, v, ws, bs, ln_g, ln_b):
    rows = u.shape[0]
    w_row = jnp.repeat(ws[:, 0, 0], HEAD_DIM).reshape(1, WIDTH)
    b_row = jnp.repeat(bs[:, 0], HEAD_DIM).reshape(1, WIDTH)
    return pl.pallas_call(
        _gmlp_step_kernel,
        out_shape=[jax.ShapeDtypeStruct((rows, WIDTH), BF16), jax.ShapeDtypeStruct((rows, WIDTH), F32)],
    )(u, v, w_row, b_row, ln_g.reshape(1, WIDTH), ln_b.reshape(1, WIDTH))


def _hgrn_step_kernel(qc_ref, fc_ref, lbc_ref, v_ref, gd_ref, go_ref, s0_ref, o_ref, s_ref):
    lf, k = _hgrn_gates(fc_ref[...], lbc_ref[...])
    s_new = jnp.exp(lf) * s0_ref[...] + k * v_ref[...]
    s_ref[...] = s_new
    o = jnp.sum(qc_ref[...] * s_new, axis=0, keepdims=True)
    o_ref[...] = _hgrn_readout(o, gd_ref[...], go_ref[...]).astype(o_ref.dtype)


def _hgrn_step(qd, fd, vd, gd, lb, g_out, state, layer):
    batch = qd.shape[0]
    colv = lambda a: a.reshape(batch, N_HEADS, HEAD_DIM, 1)
    rowv = lambda a: a.reshape(batch, N_HEADS, 1, HEAD_DIM)
    cspec = pl.BlockSpec((None, None, HEAD_DIM, 1), lambda b, h: (b, h, 0, 0))
    rspec = pl.BlockSpec((None, None, 1, HEAD_DIM), lambda b, h: (b, h, 0, 0))
    return pl.pallas_call(
        _hgrn_step_kernel,
        grid=(batch, N_HEADS),
        in_specs=[cspec, cspec, pl.BlockSpec((None, HEAD_DIM, 1), lambda b, h: (h, 0, 0)), rspec, rspec,
                  pl.BlockSpec((1, HEAD_DIM), lambda b, h: (0, 0)),
                  pl.BlockSpec((None, None, None, HEAD_DIM, HEAD_DIM), lambda b, h: (layer, b, h, 0, 0))],
        out_specs=[rspec, pl.BlockSpec((None, None, HEAD_DIM, HEAD_DIM), lambda b, h: (b, h, 0, 0))],
        out_shape=[jax.ShapeDtypeStruct((batch, N_HEADS, 1, HEAD_DIM), BF16),
                   jax.ShapeDtypeStruct((batch, N_HEADS, HEAD_DIM, HEAD_DIM), F32)],
        compiler_params=_params(("parallel", "parallel"), 32),
    )(colv(qd), colv(fd), lb.reshape(N_HEADS, HEAD_DIM, 1), rowv(vd), rowv(gd), g_out.reshape(1, HEAD_DIM), state)


GATHER_PAGES = 32


def _gather_kernel(pt_ref, *refs):
    o_ref = refs[-1]
    for r in range(GATHER_PAGES):
        for c in range(PAGE_STREAMS):
            o_ref[r * PAGE_SIZE:(r + 1) * PAGE_SIZE, c * HEAD_DIM:(c + 1) * HEAD_DIM] = (
                refs[r][pl.ds(c, PAGE_SIZE, stride=PAGE_STREAMS), :])


def _gather_pages(pool, layer, page_table):
    view = _page_view(pool)
    width = PAGE_STREAMS * HEAD_DIM
    batch, n_pages = page_table.shape
    groups = n_pages // GATHER_PAGES
    rows = GATHER_PAGES * PAGE_SIZE
    in_specs = [pl.BlockSpec((None, None, PAGE_SIZE * PAGE_STREAMS, HEAD_DIM),
                             lambda b, g, pt, r=r: (layer, pt[b, g * GATHER_PAGES + r], 0, 0))
                for r in range(GATHER_PAGES)]
    return pl.pallas_call(
        _gather_kernel,
        grid_spec=pltpu.PrefetchScalarGridSpec(
            num_scalar_prefetch=1, grid=(batch, groups), in_specs=in_specs,
            out_specs=pl.BlockSpec((rows, width), lambda b, g, pt: (b * groups + g, 0))),
        out_shape=jax.ShapeDtypeStruct((batch * n_pages * PAGE_SIZE, width), pool.dtype),
        compiler_params=_params(("parallel", "arbitrary"), 40),
    )(page_table, *([view] * GATHER_PAGES))


def _ffn(xp, xs, mod_p, mod_s, sub, mod_idx, g, wg, wu, wd, layer, rows_per_group):
    act = _norm_glu(xp, xs, g, mod_p, mod_s, mod_idx, wg, wu, layer, sub, rows_per_group)
    return _down(act, wd, layer, sub, xp, xs, mod_p[:, mod_idx, 2], mod_s[:, mod_idx, 2], 0.5)


def _kv5(a, batch):
    return a.reshape(batch, -1, 2, KV_HEADS, HEAD_DIM)


def _cache_rows_kernel(*refs):
    z_refs, o_ref = refs[:-1], refs[-1]
    rows = z_refs[0].shape[0]
    for li, z_ref in enumerate(z_refs):
        @pl.when(pl.program_id(0) == li)
        def _(z_ref=z_ref):
            for c in range(PAGE_STREAMS):
                o_ref[pl.ds(c, rows, stride=PAGE_STREAMS), :] = z_ref[:, c * HEAD_DIM:(c + 1) * HEAD_DIM]


def _cache_rows(z_layers, name, batch, keep):
    depth = len(z_layers)
    t = z_layers[0].shape[0] // batch
    rows = min(keep, ROW_TILE)
    per_batch, first = keep // rows, (t - keep) // rows
    n_steps = batch * per_batch
    width = PAGE_STREAMS * HEAD_DIM
    blk = SEG[name] // width

    def z_spec(li):
        def index(l, s):
            s_eff = jnp.where(l == li, s, jnp.where(l < li, 0, n_steps - 1))
            return ((s_eff // per_batch) * (t // rows) + first + s_eff % per_batch, blk)
        return pl.BlockSpec((rows, width), index)

    out = pl.pallas_call(
        _cache_rows_kernel,
        grid=(depth, n_steps),
        in_specs=[z_spec(li) for li in range(depth)],
        out_specs=pl.BlockSpec((None, rows * PAGE_STREAMS, HEAD_DIM), lambda l, s: (l, s, 0)),
        out_shape=jax.ShapeDtypeStruct((depth, batch * keep * PAGE_STREAMS, HEAD_DIM), F32),
        compiler_params=_params(("arbitrary", "arbitrary"), 32),
    )(*z_layers)
    return out.reshape(depth, batch, keep, 2, KV_HEADS, HEAD_DIM)


def _prompt_mix(z, batch, mixw):
    t = z.shape[0] // batch
    kvc = z[:, SEG["kvc"]:SEG["kvc"] + PAGE_STREAMS * HEAD_DIM]
    cmp_kv = _compress(kvc, batch, mixw["cw1"], mixw["cb1"], mixw["cw2"], mixw["cpe"], mixw["qkg"][1])
    kb = lambda name: SEG[name] // HEAD_DIM
    o_a = _nsa(z, t, 128, 0, cmp_kv, z, (kb("kvs"), kb("kvs") + KV_HEADS), None,
               z, (kb("kvw"), kb("kvw") + KV_HEADS), 0, t, True)
    o_b = _gmlp(z, mixw["ws"], mixw["bs"], mixw["ln_g"], mixw["ln_b"])
    o_c = _moba(z, t, 128, 0, batch, z, (kb("kvm"), kb("kvm") + KV_HEADS), None, True)
    o_d, s_d = _hgrn(z, batch, mixw["lb"], mixw["out_g"])
    return (o_a, o_b, o_c, o_d), s_d


def _pad_rows(a, batch, rows):
    out = jnp.zeros((batch, rows, a.shape[1]), a.dtype).at[:, 0].set(a)
    return out.reshape(batch * rows, a.shape[1])


def _sample_mix(z_pad, batch, mixw, pool_cmp, pool_sel, pool_moba, win_buf, state, page_table, layer):
    past = page_table.shape[1] * PAGE_SIZE
    z = z_pad[:batch]
    seg = lambda name, w: z[:, SEG[name]:SEG[name] + w]
    kvc, kvs, kvw, kvm = seg("kvc", 512), seg("kvs", 512), seg("kvw", 512), seg("kvm", 512)
    zq = _pad_rows(z, batch, SAMPLE_ROWS)
    dense_sel = _gather_pages(pool_sel, layer, page_table)
    dense_moba = _gather_pages(pool_moba, layer, page_table)
    cmp_kv = _compress_paged(pool_cmp, layer, page_table, mixw["cw1"], mixw["cb1"], mixw["cw2"], mixw["cpe"],
                             mixw["qkg"][1])
    wlen = win_buf.shape[2]
    band = jnp.concatenate([win_buf[layer].reshape(batch, wlen, 512), kvw[:, None, :]], axis=1)
    win = jnp.concatenate([band, jnp.zeros((batch, WIN_BAND - wlen - 1, 512), F32)], axis=1)
    o_a = _nsa(zq, SAMPLE_ROWS, SAMPLE_ROWS, past, cmp_kv, dense_sel, (0, KV_HEADS),
               _pad_rows(kvs, batch, TAIL_ROWS), win.reshape(batch * WIN_BAND, 512), (0, KV_HEADS),
               past - wlen, past + 1, False)
    o_c = _moba(zq, SAMPLE_ROWS, SAMPLE_ROWS, past, batch, dense_moba, (0, KV_HEADS),
                _pad_rows(kvm, batch, TAIL_ROWS), False)
    o_b, v_n = _gmlp_step(seg("u", WIDTH), seg("v", WIDTH), mixw["ws"], mixw["bs"], mixw["ln_g"], mixw["ln_b"])
    o_d, s_d = _hgrn_step(seg("qd", WIDTH), seg("fd", WIDTH), seg("vd", WIDTH), seg("gd", WIDTH),
                          mixw["lb"], mixw["out_g"], state, layer)
    mixed = jnp.concatenate([o_a[::SAMPLE_ROWS], o_b, o_c[::SAMPLE_ROWS], o_d.reshape(batch, WIDTH)], axis=1)
    mixed = jnp.concatenate([mixed, jnp.zeros((z_pad.shape[0] - batch, mixed.shape[1]), mixed.dtype)], axis=0)
    new_state = (_kv5(kvc, batch), _kv5(kvs, batch), _kv5(kvm, batch),
                 band[:, 1:].reshape(batch, wlen, 2, KV_HEADS, HEAD_DIM), s_d, v_n.reshape(batch, 1, WIDTH))
    return mixed, new_state


def kernel(x_prompt, x_sample, c_prompt, c_sample, cache_nsa_cmp_kv, cache_nsa_sel_kv, cache_moba_kv, cache_nsa_win_kv, state_hgrn, page_table, w_ada, b_ada, norm_g, w_ffn_gate, w_ffn_up, w_ffn_down, w_in, w_out, qk_norm_g, nsa_cmp_w1, nsa_cmp_b1, nsa_cmp_w2, nsa_cmp_pos, gmlp_ln_g, gmlp_ln_b, gmlp_ws, gmlp_bs, hgrn_lb_logits, hgrn_out_g):
    bp, t, d = x_prompt.shape
    bs = x_sample.shape[0]
    depth = w_in.shape[0]
    srows = SAMPLE_PAD
    lb_p = jax.nn.softmax(hgrn_lb_logits.astype(F32), axis=0)
    lb_all = jnp.clip(jnp.cumsum(lb_p, axis=0) - lb_p[0:1], 0.0, 1.0)

    c_all = jnp.concatenate([c_prompt, c_sample, jnp.zeros((16 - bp - bs, d), F32)], axis=0)
    mod = _ada_mod(c_all, w_ada, b_ada).reshape(depth, 16, 3, 3, d)

    xp = x_prompt.reshape(bp * t, d)
    xs = jnp.concatenate([x_sample.reshape(bs, d), jnp.zeros((srows - bs, d), F32)], axis=0)
    z_layers, hgrn_p, st_s = [], [], []
    for l in range(depth):
        mod_p = mod[l, :bp]
        mod_s = jnp.concatenate([mod[l, bp:bp + bs], jnp.zeros((srows - bs, 3, 3, d), F32)], axis=0)
        gains = _proj_gains(qk_norm_g[l])
        mixw = dict(qkg=qk_norm_g[l], cw1=nsa_cmp_w1[l], cb1=nsa_cmp_b1[l], cw2=nsa_cmp_w2[l], cpe=nsa_cmp_pos[l],
                    ln_g=gmlp_ln_g[l], ln_b=gmlp_ln_b[l], ws=gmlp_ws[l], bs=gmlp_bs[l], lb=lb_all[l],
                    out_g=hgrn_out_g[l])
        g = norm_g[l]

        xp, xs = _ffn(xp, xs, mod_p, mod_s, 0, 0, g[0], w_ffn_gate, w_ffn_up, w_ffn_down, l, t)
        zp, zs = _proj(xp, xs, g[1], mod_p, mod_s, 1, w_in, l, gains, t)
        mixed_p, sp = _prompt_mix(zp, bp, mixw)
        mixed_s, ss = _sample_mix(zs, bs, mixw, cache_nsa_cmp_kv, cache_nsa_sel_kv, cache_moba_kv,
                                  cache_nsa_win_kv, state_hgrn, page_table, l)
        xp, xs = _out_proj(mixed_p, mixed_s, w_out, l, xp, xs, mod_p[:, 1, 2], mod_s[:, 1, 2], t)
        xp, xs = _ffn(xp, xs, mod_p, mod_s, 1, 2, g[2], w_ffn_gate, w_ffn_up, w_ffn_down, l, t)
        z_layers.append(zp)
        hgrn_p.append(sp)
        st_s.append(ss)

    stack = lambda sts, i: jnp.stack([s[i] for s in sts])
    return (xp.reshape(bp, t, d), xs[:bs].reshape(bs, 1, d),
            _cache_rows(z_layers, "kvc", bp, t), _cache_rows(z_layers, "kvs", bp, t),
            _cache_rows(z_layers, "kvm", bp, t), _cache_rows(z_layers, "kvw", bp, min(NSA_WINDOW, t)),
            jnp.stack(hgrn_p),
            stack(st_s, 0), stack(st_s, 1), stack(st_s, 2), stack(st_s, 3), stack(st_s, 4), stack(st_s, 5))
```
